```python
import math
import jax, jax.numpy as jnp
from jax import lax
import numpy as np

D_MODEL = 1024
BATCH = 8
SEQ = 2048
DEPTH = 2

EPS = 1e-6
RET_HEADS = 4
RET_QK_DIM = 128
RET_V_DIM = 128
RET_WIDTH = RET_HEADS * RET_V_DIM
RET_CHUNK = 128
ROPE_THETA = 10000.0
SSM_WIDTH = D_MODEL
SSM_HEAD_DIM = 64
SSM_HEADS = SSM_WIDTH // SSM_HEAD_DIM
SSM_GROUPS = 2
SSM_STATE = 128
SSM_CONV = 4
SSM_CHUNK = 128
SSM_XBC = SSM_WIDTH + 2 * SSM_GROUPS * SSM_STATE
SC_WIDTH = D_MODEL // 2
SC_CONV = 3
N_BRANCH = 3

IN_SIZES = (
    RET_HEADS * RET_QK_DIM,
    RET_HEADS * RET_QK_DIM,
    RET_WIDTH,
    RET_WIDTH,
    SSM_WIDTH,
    SSM_XBC,
    SSM_HEADS,
    SC_WIDTH,
    SC_WIDTH,
    SC_WIDTH,
    SC_WIDTH,
    N_BRANCH * D_MODEL,
)
IN_WIDTH = sum(IN_SIZES)

kernel_name = "hybrid_retention_ssd_shortconv_gated_merge"


def rms_norm(x, g):
    xf = x.astype(jnp.float32)
    y = xf * lax.rsqrt(jnp.mean(xf * xf, axis=-1, keepdims=True) + EPS)
    return (y * g.astype(jnp.float32)).astype(x.dtype)


def causal_depthwise_conv(u, w):
    K = w.shape[0]
    L = u.shape[1]
    up = jnp.pad(u, ((0, 0), (K - 1, 0), (0, 0)))
    return sum(up[:, k:k + L] * w[k] for k in range(K))


def rotary(x, pos):
    dh = x.shape[-1]
    inv_freq = ROPE_THETA ** (-jnp.arange(0, dh, 2, dtype=jnp.float32) / dh)
    ang = pos.astype(jnp.float32)[..., None] * inv_freq
    cos = jnp.cos(ang)[:, :, None, :]
    sin = jnp.sin(ang)[:, :, None, :]
    x1, x2 = jnp.split(x, 2, axis=-1)
    return jnp.concatenate([x1 * cos - x2 * sin, x1 * sin + x2 * cos], axis=-1)


def chunkwise_retention(q, k, v):
    Bsz, L, H, dk = q.shape
    dv = v.shape[-1]
    C = RET_CHUNK
    n = L // C
    gamma = 1.0 - jnp.exp2(-5.0 - jnp.arange(H, dtype=jnp.float32))
    log_g = jnp.log(gamma)
    idx = jnp.arange(C, dtype=jnp.float32)
    diff = idx[:, None] - idx[None, :]
    decay_mask = jnp.exp(jnp.where(diff[None] >= 0, diff[None] * log_g[:, None, None], -jnp.inf))
    k_decay = jnp.exp((C - 1 - idx)[None, :] * log_g[:, None])
    q_decay = jnp.exp((idx + 1)[None, :] * log_g[:, None])
    chunk_decay = jnp.exp(C * log_g)

    def to_chunks(t):
        return t.reshape(Bsz, n, C, H, t.shape[-1]).transpose(1, 0, 3, 2, 4)

    qc, kc, vc = to_chunks(q), to_chunks(k), to_chunks(v)
    scores = jnp.einsum('nbhid,nbhjd->nbhij', qc, kc) * decay_mask[None, None]
    inner = jnp.einsum('nbhij,nbhje->nbhie', scores, vc)
    kv = jnp.einsum('nbhjd,hj,nbhje->nbhde', kc, k_decay, vc)

    def step(state, kv_n):
        return chunk_decay[None, :, None, None] * state + kv_n, state

    _, prev = lax.scan(step, jnp.zeros((Bsz, H, dk, dv), jnp.float32), kv)
    cross = jnp.einsum('nbhid,nbhde->nbhie', qc, prev) * q_decay[None, None, :, :, None]
    out = (inner + cross).transpose(1, 0, 3, 2, 4)
    return out.reshape(Bsz, L, H, dv)


def ssd_chunked(x, dt, A, Bm, Cm):
    b, l, h, p = x.shape
    g, nst = Bm.shape[-2], Bm.shape[-1]
    hg = h // g
    CH = SSM_CHUNK
    nc = l // CH
    xc = x.reshape(b, nc, CH, g, hg, p)
    dtc = dt.reshape(b, nc, CH, g, hg)
    Bc = Bm.reshape(b, nc, CH, g, nst)
    Cc = Cm.reshape(b, nc, CH, g, nst)
    a = (dtc * A.reshape(g, hg)).transpose(0, 3, 4, 1, 2)
    a_cs = jnp.cumsum(a, axis=-1)
    seg = a_cs[..., :, None] - a_cs[..., None, :]
    tril = jnp.tril(jnp.ones((CH, CH), dtype=bool))
    Lm = jnp.exp(jnp.where(tril, seg, -jnp.inf))
    xdt = xc * dtc[..., None]
    y_diag = jnp.einsum('bclgn,bcsgn,bghcls,bcsghp->bclghp', Cc, Bc, Lm, xdt)
    decay_states = jnp.exp(a_cs[..., -1:] - a_cs)
    states = jnp.einsum('bclgn,bghcl,bclghp->bcghpn', Bc, decay_states, xdt)
    chunk_decay = jnp.exp(a_cs[..., -1])

    def step(hstate, inp):
        s, d = inp
        return d[..., None, None] * hstate + s, hstate

    _, prev = lax.scan(step, jnp.zeros((b, g, hg, p, nst), jnp.float32),
                       (jnp.moveaxis(states, 1, 0), jnp.moveaxis(chunk_decay, 3, 0)))
    prev = jnp.moveaxis(prev, 0, 1)
    y_off = jnp.einsum('bclgn,bcghpn,bghcl->bclghp', Cc, prev, jnp.exp(a_cs))
    return (y_diag + y_off).reshape(b, l, h, p)


def retention_branch(q, k, v, gate, positions):
    Bsz, L, _ = q.shape
    f32 = jnp.float32
    qh = rotary(q.astype(f32).reshape(Bsz, L, RET_HEADS, RET_QK_DIM), positions) * (RET_QK_DIM ** -0.5)
    kh = rotary(k.astype(f32).reshape(Bsz, L, RET_HEADS, RET_QK_DIM), positions)
    vh = v.astype(f32).reshape(Bsz, L, RET_HEADS, RET_V_DIM)
    o = chunkwise_retention(qh, kh, vh)
    mu = jnp.mean(o, axis=-1, keepdims=True)
    var = jnp.mean(jnp.square(o - mu), axis=-1, keepdims=True)
    o = ((o - mu) * lax.rsqrt(var + EPS)).reshape(Bsz, L, RET_WIDTH)
    return (o * jax.nn.silu(gate.astype(f32))).astype(q.dtype)


def ssd_branch(z, xbc, dt_raw, conv_w, conv_b, dt_bias, a_log, d_skip, norm_g):
    Bsz, L, _ = z.shape
    f32 = jnp.float32
    xbc = jax.nn.silu(causal_depthwise_conv(xbc, conv_w) + conv_b)
    xs, Bm, Cm = jnp.split(xbc, [SSM_WIDTH, SSM_WIDTH + SSM_GROUPS * SSM_STATE], axis=-1)
    xs = xs.astype(f32).reshape(Bsz, L, SSM_HEADS, SSM_HEAD_DIM)
    Bm = Bm.astype(f32).reshape(Bsz, L, SSM_GROUPS, SSM_STATE)
    Cm = Cm.astype(f32).reshape(Bsz, L, SSM_GROUPS, SSM_STATE)
    dt = jax.nn.softplus(dt_raw.astype(f32) + dt_bias.astype(f32))
    A = -jnp.exp(a_log.astype(f32))
    y = ssd_chunked(xs, dt, A, Bm, Cm) + d_skip.astype(f32)[:, None] * xs
    y = y.reshape(Bsz, L, SSM_WIDTH) * jax.nn.silu(z.astype(f32))
    return rms_norm(y, norm_g).astype(z.dtype)


def short_conv_branch(bg, cg, hv, gate, conv_w):
    u = cg * hv
    return bg * causal_depthwise_conv(u, conv_w) * jax.nn.silu(gate)


def setup_inputs(seed: int = 0) -> dict:
    key = jax.random.key(seed)
    ks = jax.random.split(key, 24)
    D = D_MODEL
    nrm = jax.random.normal
    dt0 = jnp.exp(jax.random.uniform(ks[8], (DEPTH, SSM_HEADS)) * (math.log(0.1) - math.log(0.001)) + math.log(0.001))
    return {
        "x": nrm(ks[0], (BATCH, SEQ, D), jnp.float32),
        "c": nrm(ks[1], (BATCH, D), jnp.float32),
        "positions": (jnp.arange(SEQ, dtype=jnp.int32)[None, :]
                      + jax.random.randint(ks[2], (BATCH, 1), 0, 1024, dtype=jnp.int32)),
        "norm_g": 1.0 + 0.02 * nrm(ks[3], (DEPTH, D)),
        "w_ada": nrm(ks[4], (DEPTH, D, 3 * D)) * (0.5 * D ** -0.5),
        "b_ada": 0.02 * nrm(ks[5], (DEPTH, 3 * D)),
        "w_in": nrm(ks[6], (DEPTH, D, IN_WIDTH)) * D ** -0.5,
        "ssm_conv_w": nrm(ks[7], (DEPTH, SSM_CONV, SSM_XBC)) * SSM_CONV ** -0.5,
        "ssm_conv_b": 0.02 * nrm(ks[9], (DEPTH, SSM_XBC)),
        "ssm_dt_bias": dt0 + jnp.log(-jnp.expm1(-dt0)),
        "ssm_a_log": jnp.log(jax.random.uniform(ks[10], (DEPTH, SSM_HEADS), minval=1.0, maxval=16.0)),
        "ssm_d": 1.0 + 0.02 * nrm(ks[11], (DEPTH, SSM_HEADS)),
        "ssm_norm_g": 1.0 + 0.02 * nrm(ks[12], (DEPTH, SSM_WIDTH)),
        "sc_conv_w": nrm(ks[13], (DEPTH, SC_CONV, SC_WIDTH)) * SC_CONV ** -0.5,
        "w_br_ret": nrm(ks[14], (DEPTH, RET_WIDTH, D)) * RET_WIDTH ** -0.5,
        "w_br_ssm": nrm(ks[15], (DEPTH, SSM_WIDTH, D)) * SSM_WIDTH ** -0.5,
        "w_br_sc": nrm(ks[16], (DEPTH, SC_WIDTH, D)) * SC_WIDTH ** -0.5,
        "w_out": nrm(ks[17], (DEPTH, D, D)) * D ** -0.5,
        "final_norm_g": 1.0 + 0.02 * nrm(ks[18], (D,)),
    }


def reference(x, c, positions, norm_g, w_ada, b_ada, w_in, ssm_conv_w, ssm_conv_b, ssm_dt_bias,
              ssm_a_log, ssm_d, ssm_norm_g, sc_conv_w, w_br_ret, w_br_ssm, w_br_sc, w_out, final_norm_g):
    Bsz, L, D = x.shape
    split_idx = np.cumsum(IN_SIZES)[:-1].tolist()
    c_act = jax.nn.silu(c)
    for layer in range(DEPTH):
        ada = c_act @ w_ada[layer] + b_ada[layer]
        shift, scale, res_gate = jnp.split(ada, 3, axis=-1)
        h = rms_norm(x, norm_g[layer]) * (1.0 + scale[:, None]) + shift[:, None]
        proj = h @ w_in[layer]
        (r_q, r_k, r_v, r_g, s_z, s_xbc, s_dt,
         c_b, c_c, c_h, c_g, merge_logits) = jnp.split(proj, split_idx, axis=-1)
        y_ret = retention_branch(r_q, r_k, r_v, r_g, positions)
        y_ssm = ssd_branch(s_z, s_xbc, s_dt, ssm_conv_w[layer], ssm_conv_b[layer], ssm_dt_bias[layer],
                           ssm_a_log[layer], ssm_d[layer], ssm_norm_g[layer])
        y_sc = short_conv_branch(c_b, c_c, c_h, c_g, sc_conv_w[layer])
        gates = jax.nn.sigmoid(merge_logits).reshape(Bsz, L, N_BRANCH, D)
        merged = (gates[:, :, 0] * (y_ret @ w_br_ret[layer])
                  + gates[:, :, 1] * (y_ssm @ w_br_ssm[layer])
                  + gates[:, :, 2] * (y_sc @ w_br_sc[layer]))
        out = merged @ w_out[layer]
        x = x + res_gate[:, None] * out
    return rms_norm(x, final_norm_g)
```

```python
import functools
import math

import jax
import jax.numpy as jnp
from jax import lax
from jax.experimental import pallas as pl
from jax.experimental.pallas import tpu as pltpu

F32 = jnp.float32
BF16 = jnp.bfloat16

EPS = 1e-6
RET_HEADS = 4
RET_DIM = 128
RET_WIDTH = RET_HEADS * RET_DIM
ROPE_THETA = 10000.0
SSM_HEADS = 16
SSM_HEAD_DIM = 64
SSM_GROUPS = 2
SSM_STATE = 128
SSM_CONV = 4
SSM_WIDTH = SSM_HEADS * SSM_HEAD_DIM
SSM_XBC = SSM_WIDTH + 2 * SSM_GROUPS * SSM_STATE
SC_CONV = 3
CHUNK = 128
LANES = 128
SUBLANES = 8
DT_PAD = LANES
VMEM_LIMIT_BYTES = 56 * 1024 * 1024

NEG_INF = float("-inf")


def _sigmoid(x):
    return 1.0 / (1.0 + jnp.exp(-x))


def _silu(x):
    return x * _sigmoid(x)


def _dot(a, b):
    return jnp.dot(a, b, preferred_element_type=F32)


def _modulated_norm(x, g, scale, shift):
    ms = jnp.mean(x * x, axis=-1, keepdims=True)
    y = x * lax.rsqrt(ms + EPS) * g
    return (y * (1.0 + scale) + shift).astype(BF16)


def _lane_bcast(a, col):
    return jnp.broadcast_to(a[:, col:col + 1], a.shape)


def _ada_kernel(c_ref, w_ref, b_ref, o_ref):
    c = c_ref[...]
    o_ref[...] = jnp.dot(_silu(c), w_ref[...], preferred_element_type=F32,
                         precision=lax.Precision.HIGHEST) + b_ref[...]


def _ada_call(c, w_ada, b_ada):
    depth, d, d3 = w_ada.shape
    bsz = c.shape[0]
    nblk = d3 // d
    return pl.pallas_call(
        _ada_kernel,
        grid=(depth, nblk),
        in_specs=[
            pl.BlockSpec((bsz, d), lambda l, n: (0, 0)),
            pl.BlockSpec((None, d, d), lambda l, n: (l, 0, n)),
            pl.BlockSpec((None, 1, d), lambda l, n: (l, 0, n)),
        ],
        out_specs=pl.BlockSpec((None, bsz, d), lambda l, n: (l, 0, n)),
        out_shape=jax.ShapeDtypeStruct((depth, bsz, d3), F32),
        compiler_params=pltpu.CompilerParams(
            dimension_semantics=("parallel", "parallel"), vmem_limit_bytes=VMEM_LIMIT_BYTES),
        name="ada",
    )(c, w_ada, b_ada.reshape(depth, 1, d3))


def _rope_kernel(pos_ref, inv_ref, cos_ref, sin_ref):
    ang = pos_ref[...].astype(F32) * inv_ref[...]
    lane = lax.broadcasted_iota(jnp.int32, ang.shape, 1)
    cos_ref[...] = jnp.cos(ang)
    sin_ref[...] = jnp.where(lane < RET_DIM // 2, -jnp.sin(ang), jnp.sin(ang))


def _rope_call(positions):
    bsz, seq = positions.shape
    half = RET_DIM // 2
    inv_freq = ROPE_THETA ** (-jnp.arange(0, RET_DIM, 2, dtype=F32) / RET_DIM)
    inv2 = jnp.concatenate([inv_freq, inv_freq]).reshape(1, 2 * half)
    tbl = jax.ShapeDtypeStruct((bsz, seq, RET_DIM), F32)
    return pl.pallas_call(
        _rope_kernel,
        grid=(bsz,),
        in_specs=[
            pl.BlockSpec((None, seq, 1), lambda b: (b, 0, 0)),
            pl.BlockSpec((1, RET_DIM), lambda b: (0, 0)),
        ],
        out_specs=[pl.BlockSpec((None, seq, RET_DIM), lambda b: (b, 0, 0))] * 2,
        out_shape=[tbl, tbl],
        compiler_params=pltpu.CompilerParams(
            dimension_semantics=("parallel",), vmem_limit_bytes=VMEM_LIMIT_BYTES),
        name="rope_tables",
    )(positions.reshape(bsz, seq, 1), inv2)


def _ret_kernel(x_ref, g_ref, scale_ref, shift_ref, w_ref, cos_ref, sin_ref, o_ref, state_ref, *, tile):
    @pl.when(pl.program_id(1) == 0)
    def _():
        state_ref[...] = jnp.zeros_like(state_ref)

    h = _modulated_norm(x_ref[...], g_ref[...], scale_ref[...], shift_ref[...])
    qkvg = _dot(h, w_ref[...])
    cos = cos_ref[...]
    sin = sin_ref[...]
    row = lax.broadcasted_iota(jnp.int32, (CHUNK, CHUNK), 0)
    col = lax.broadcasted_iota(jnp.int32, (CHUNK, CHUNK), 1)
    rowf = row.astype(F32)
    half = RET_DIM // 2
    qk_scale = RET_DIM ** -0.5

    def rot(t):
        return t * cos + pltpu.roll(t, half, axis=1) * sin

    for hd in range(RET_HEADS):
        log_g = math.log(1.0 - 2.0 ** (-5.0 - hd))
        decay_mask = jnp.exp(jnp.where(row >= col, (row - col).astype(F32) * log_g, NEG_INF))
        q_decay = jnp.exp((rowf + 1.0) * log_g)
        k_decay = jnp.exp((CHUNK - 1.0 - rowf) * log_g)
        chunk_decay = math.exp(CHUNK * log_g)
        lo = hd * RET_DIM
        q = rot(qkvg[:, lo:lo + RET_DIM]) * qk_scale
        k = rot(qkvg[:, RET_WIDTH + lo:RET_WIDTH + lo + RET_DIM])
        v = qkvg[:, 2 * RET_WIDTH + lo:2 * RET_WIDTH + lo + RET_DIM]
        gate = qkvg[:, 3 * RET_WIDTH + lo:3 * RET_WIDTH + lo + RET_DIM]
        state = state_ref[hd]
        for c in range(tile // CHUNK):
            r0 = c * CHUNK
            qc = q[r0:r0 + CHUNK].astype(BF16)
            kc = k[r0:r0 + CHUNK]
            vc = v[r0:r0 + CHUNK].astype(BF16)
            scores = lax.dot_general(qc, kc.astype(BF16), (((1,), (1,)), ((), ())),
                                     preferred_element_type=F32)
            inner = _dot((scores * decay_mask).astype(BF16), vc)
            cross = _dot(qc, state.astype(BF16)) * q_decay
            kd = (kc * k_decay).astype(BF16)
            state = chunk_decay * state + lax.dot_general(kd, vc, (((0,), (0,)), ((), ())),
                                                          preferred_element_type=F32)
            o = inner + cross
            mu = jnp.mean(o, axis=-1, keepdims=True)
            oc = o - mu
            var = jnp.mean(oc * oc, axis=-1, keepdims=True)
            y = oc * lax.rsqrt(var + EPS) * _silu(gate[r0:r0 + CHUNK])
            o_ref[r0:r0 + CHUNK, lo:lo + RET_DIM] = y.astype(o_ref.dtype)
        state_ref[hd] = state


def _ret_call(x, ada5, norm_g3, w_r, cos2, sin2, layer, tile):
    bsz, seq, d = x.shape
    row_spec = lambda idx: pl.BlockSpec((None, None, None, 1, d), lambda b, j: (layer, b, idx, 0, 0))
    return pl.pallas_call(
        functools.partial(_ret_kernel, tile=tile),
        grid=(bsz, seq // tile),
        in_specs=[
            pl.BlockSpec((None, tile, d), lambda b, j: (b, j, 0)),
            pl.BlockSpec((None, 1, d), lambda b, j: (layer, 0, 0)),
            row_spec(1),
            row_spec(0),
            pl.BlockSpec(w_r.shape, lambda b, j: (0, 0)),
            pl.BlockSpec((None, tile, RET_DIM), lambda b, j: (b, j, 0)),
            pl.BlockSpec((None, tile, RET_DIM), lambda b, j: (b, j, 0)),
        ],
        out_specs=pl.BlockSpec((None, tile, RET_WIDTH), lambda b, j: (b, j, 0)),
        out_shape=jax.ShapeDtypeStruct((bsz, seq, RET_WIDTH), BF16),
        scratch_shapes=[pltpu.VMEM((RET_HEADS, RET_DIM, RET_DIM), F32)],
        compiler_params=pltpu.CompilerParams(
            dimension_semantics=("parallel", "arbitrary"), vmem_limit_bytes=VMEM_LIMIT_BYTES),
        name="retention",
    )(x, norm_g3, ada5, ada5, w_r, cos2, sin2)


def _block_diag_pair(slab):
    lane = lax.broadcasted_iota(jnp.int32, slab.shape, 1)
    first = lane < SSM_HEAD_DIM
    top = jnp.where(first, slab, 0.0)
    bot = jnp.where(first, 0.0, slab)
    return jnp.concatenate([top, bot], axis=0).astype(BF16)


def _ssd_kernel(x_ref, g_ref, scale_ref, shift_ref, w_ref, convw_ref, convb_ref, dtb_ref, alog_ref,
                dskip_ref, ng_ref, o_ref, xbc_ref, state_ref, y_ref, *, tile):
    @pl.when(pl.program_id(1) == 0)
    def _():
        xbc_ref[0:SUBLANES, :] = jnp.zeros((SUBLANES, SSM_XBC), F32)
        state_ref[...] = jnp.zeros_like(state_ref)

    h = _modulated_norm(x_ref[...], g_ref[...], scale_ref[...], shift_ref[...])
    zxd = _dot(h, w_ref[...])
    z = zxd[:, :SSM_WIDTH]
    dt_raw = zxd[:, SSM_WIDTH + SSM_XBC:]

    xbc_ref[SUBLANES:SUBLANES + tile, :] = zxd[:, SSM_WIDTH:SSM_WIDTH + SSM_XBC]
    conv = convb_ref[...]
    for kk in range(SSM_CONV):
        off = SUBLANES - (SSM_CONV - 1) + kk
        conv = conv + convw_ref[kk:kk + 1, :] * xbc_ref[off:off + tile, :]
    xbc_ref[0:SUBLANES, :] = xbc_ref[tile:tile + SUBLANES, :]
    xbc = _silu(conv)

    dt_in = dt_raw + dtb_ref[...]
    dt = jnp.maximum(dt_in, 0.0) + jnp.log1p(jnp.exp(-jnp.abs(dt_in)))
    a = dt * (-jnp.exp(alog_ref[...]))

    row = lax.broadcasted_iota(jnp.int32, (CHUNK, CHUNK), 0)
    col = lax.broadcasted_iota(jnp.int32, (CHUNK, CHUNK), 1)
    tril = row >= col
    tri_f = tril.astype(F32)
    lane1 = lax.broadcasted_iota(jnp.int32, (1, LANES), 1)
    pair = 2 * SSM_HEAD_DIM
    heads_per_group = SSM_HEADS // SSM_GROUPS

    for c in range(tile // CHUNK):
        r0 = c * CHUNK
        dt_c = dt[r0:r0 + CHUNK]
        a_cs = jnp.dot(tri_f, a[r0:r0 + CHUNK], preferred_element_type=F32,
                       precision=lax.Precision.HIGHEST)
        a_last = a_cs[CHUNK - 1:CHUNK, :]
        a_cs_t = a_cs.T
        dt_t = dt_c.T
        w_t = (jnp.exp(a_last - a_cs) * dt_c).T
        e_acs = jnp.exp(a_cs)
        cdec = jnp.exp(a_last)
        for g in range(SSM_GROUPS):
            b_g = xbc[r0:r0 + CHUNK, SSM_WIDTH + g * SSM_STATE:SSM_WIDTH + (g + 1) * SSM_STATE]
            c0 = SSM_WIDTH + SSM_GROUPS * SSM_STATE + g * SSM_STATE
            c_g = xbc[r0:r0 + CHUNK, c0:c0 + SSM_STATE]
            b_t = b_g.T
            cb = _dot(c_g.astype(BF16), b_t.astype(BF16))
            for p in range(heads_per_group // 2):
                h1 = g * heads_per_group + 2 * p
                lo = h1 * SSM_HEAD_DIM
                xs_pair = xbc[r0:r0 + CHUNK, lo:lo + pair]
                rhs = _block_diag_pair(xs_pair)
                m_parts, s_parts, o_parts = [], [], []
                for hh in (h1, h1 + 1):
                    seg = _lane_bcast(a_cs, hh) - a_cs_t[hh:hh + 1, :]
                    lm = jnp.exp(jnp.where(tril, seg, NEG_INF))
                    m_parts.append(cb * lm * dt_t[hh:hh + 1, :])
                    s_parts.append(b_t * w_t[hh:hh + 1, :])
                    o_parts.append(c_g * _lane_bcast(e_acs, hh))
                y_diag = _dot(jnp.concatenate(m_parts, axis=1).astype(BF16), rhs)
                s_new = _dot(jnp.concatenate(s_parts, axis=1).astype(BF16), rhs)
                s_prev = state_ref[:, lo:lo + pair]
                y_off = _dot(jnp.concatenate(o_parts, axis=1).astype(BF16), _block_diag_pair(s_prev))
                cd_pair = jnp.where(lane1 < SSM_HEAD_DIM, _lane_bcast(cdec, h1), _lane_bcast(cdec, h1 + 1))
                state_ref[:, lo:lo + pair] = cd_pair * s_prev + s_new
                y_ref[:, lo:lo + pair] = y_diag + y_off + dskip_ref[:, lo:lo + pair] * xs_pair
        y = y_ref[...] * _silu(z[r0:r0 + CHUNK])
        ms = jnp.mean(y * y, axis=-1, keepdims=True)
        o_ref[r0:r0 + CHUNK, :] = (y * lax.rsqrt(ms + EPS) * ng_ref[...]).astype(o_ref.dtype)


def _ssd_call(x, ada5, norm_g3, w_s, conv_w, conv_b, dt_bias, a_log, d_skip, ssm_norm_g, layer, tile):
    bsz, seq, d = x.shape
    row_spec = lambda idx: pl.BlockSpec((None, None, None, 1, d), lambda b, j: (layer, b, idx, 0, 0))
    full = lambda a: pl.BlockSpec(a.shape, lambda b, j: (0,) * a.ndim)
    small = [conv_w, conv_b, dt_bias, a_log, d_skip, ssm_norm_g]
    return pl.pallas_call(
        functools.partial(_ssd_kernel, tile=tile),
        grid=(bsz, seq // tile),
        in_specs=[
            pl.BlockSpec((None, tile, d), lambda b, j: (b, j, 0)),
            pl.BlockSpec((None, 1, d), lambda b, j: (layer, 0, 0)),
            row_spec(1),
            row_spec(0),
            full(w_s),
        ] + [full(a) for a in small],
        out_specs=pl.BlockSpec((None, tile, SSM_WIDTH), lambda b, j: (b, j, 0)),
        out_shape=jax.ShapeDtypeStruct((bsz, seq, SSM_WIDTH), BF16),
        scratch_shapes=[
            pltpu.VMEM((tile + SUBLANES, SSM_XBC), F32),
            pltpu.VMEM((SSM_STATE, SSM_WIDTH), F32),
            pltpu.VMEM((CHUNK, SSM_WIDTH), F32),
        ],
        compiler_params=pltpu.CompilerParams(
            dimension_semantics=("parallel", "arbitrary"), vmem_limit_bytes=VMEM_LIMIT_BYTES),
        name="ssd",
    )(x, norm_g3, ada5, ada5, w_s, *small)


def _merge_kernel(x_ref, g_ref, scale_ref, shift_ref, rgate_ref, wc_ref, wg_ref, scw_ref,
                  yret_ref, yssm_ref, wbr_ret_ref, wbr_ssm_ref, wbr_sc_ref, wout_ref, fg_ref,
                  o_ref, u_ref, *, tile, final):
    @pl.when(pl.program_id(1) == 0)
    def _():
        u_ref[0:SUBLANES, :] = jnp.zeros((SUBLANES, u_ref.shape[1]), F32)

    x = x_ref[...]
    d = x.shape[1]
    width = u_ref.shape[1]
    h = _modulated_norm(x, g_ref[...], scale_ref[...], shift_ref[...])

    cp = _dot(h, wc_ref[...])
    u_ref[SUBLANES:SUBLANES + tile, :] = cp[:, width:2 * width] * cp[:, 2 * width:3 * width]
    conv = jnp.zeros((tile, width), F32)
    for kk in range(SC_CONV):
        off = SUBLANES - (SC_CONV - 1) + kk
        conv = conv + scw_ref[kk:kk + 1, :] * u_ref[off:off + tile, :]
    u_ref[0:SUBLANES, :] = u_ref[tile:tile + SUBLANES, :]
    y_sc = (cp[:, :width] * conv * _silu(cp[:, 3 * width:])).astype(BF16)

    gl = _dot(h, wg_ref[...])
    merged = (_sigmoid(gl[:, :d]) * _dot(yret_ref[...], wbr_ret_ref[...])
              + _sigmoid(gl[:, d:2 * d]) * _dot(yssm_ref[...], wbr_ssm_ref[...])
              + _sigmoid(gl[:, 2 * d:]) * _dot(y_sc, wbr_sc_ref[...]))
    out = _dot(merged.astype(BF16), wout_ref[...])
    xn = x + rgate_ref[...] * out
    if final:
        ms = jnp.mean(xn * xn, axis=-1, keepdims=True)
        xn = xn * lax.rsqrt(ms + EPS) * fg_ref[...]
    o_ref[...] = xn


def _merge_call(x, ada5, norm_g3, w_c, w_g, sc_w, y_ret, y_ssm, wbr_ret, wbr_ssm, wbr_sc, w_out,
                final_g, layer, tile, final):
    bsz, seq, d = x.shape
    width = sc_w.shape[1]
    row_spec = lambda idx: pl.BlockSpec((None, None, None, 1, d), lambda b, j: (layer, b, idx, 0, 0))
    full = lambda a: pl.BlockSpec(a.shape, lambda b, j: (0,) * a.ndim)
    tok = lambda w: pl.BlockSpec((None, tile, w), lambda b, j: (b, j, 0))
    return pl.pallas_call(
        functools.partial(_merge_kernel, tile=tile, final=final),
        grid=(bsz, seq // tile),
        in_specs=[
            tok(d),
            pl.BlockSpec((None, 1, d), lambda b, j: (layer, 0, 0)),
            row_spec(1),
            row_spec(0),
            row_spec(2),
            full(w_c), full(w_g), full(sc_w),
            tok(y_ret.shape[2]), tok(y_ssm.shape[2]),
            full(wbr_ret), full(wbr_ssm), full(wbr_sc), full(w_out), full(final_g),
        ],
        out_specs=tok(d),
        out_shape=jax.ShapeDtypeStruct((bsz, seq, d), F32),
        scratch_shapes=[pltpu.VMEM((tile + SUBLANES, width), F32)],
        compiler_params=pltpu.CompilerParams(
            dimension_semantics=("parallel", "arbitrary"), vmem_limit_bytes=VMEM_LIMIT_BYTES),
        name="merge_out",
    )(x, norm_g3, ada5, ada5, ada5, w_c, w_g, sc_w, y_ret, y_ssm, wbr_ret, wbr_ssm, wbr_sc, w_out, final_g)


def kernel(x, c, positions, norm_g, w_ada, b_ada, w_in, ssm_conv_w, ssm_conv_b, ssm_dt_bias, ssm_a_log,
           ssm_d, ssm_norm_g, sc_conv_w, w_br_ret, w_br_ssm, w_br_sc, w_out, final_norm_g):
    bsz, seq, d = x.shape
    depth = w_in.shape[0]
    sc_width = sc_conv_w.shape[2]
    tile = 256
    assert seq % tile == 0 and tile % CHUNK == 0

    sizes = (RET_WIDTH, RET_WIDTH, RET_WIDTH, RET_WIDTH, SSM_WIDTH, SSM_XBC, SSM_HEADS,
             sc_width, sc_width, sc_width, sc_width, 3 * d)
    offs = [0]
    for s in sizes:
        offs.append(offs[-1] + s)
    assert offs[-1] == w_in.shape[2]

    ada = _ada_call(c, w_ada, b_ada)
    ada5 = ada.reshape(depth, bsz, 3, 1, d)
    cos2, sin2 = _rope_call(positions)
    norm_g3 = norm_g.reshape(depth, 1, d)
    final_g = final_norm_g.reshape(1, d)

    for layer in range(depth):
        w = w_in[layer]
        w_r = w[:, offs[0]:offs[4]].astype(BF16)
        w_s = jnp.concatenate(
            [w[:, offs[4]:offs[7]], jnp.zeros((d, DT_PAD - SSM_HEADS), w.dtype)], axis=1).astype(BF16)
        w_c = w[:, offs[7]:offs[11]].astype(BF16)
        w_g = w[:, offs[11]:offs[12]].astype(BF16)
        pad_heads = lambda v: jnp.pad(v, (0, DT_PAD - SSM_HEADS)).reshape(1, DT_PAD)

        y_ret = _ret_call(x, ada5, norm_g3, w_r, cos2, sin2, layer, tile)
        y_ssm = _ssd_call(
            x, ada5, norm_g3, w_s, ssm_conv_w[layer], ssm_conv_b[layer].reshape(1, SSM_XBC),
            pad_heads(ssm_dt_bias[layer]), pad_heads(ssm_a_log[layer]),
            jnp.repeat(ssm_d[layer], SSM_HEAD_DIM).reshape(1, SSM_WIDTH),
            ssm_norm_g[layer].reshape(1, SSM_WIDTH), layer, tile)
        x = _merge_call(
            x, ada5, norm_g3, w_c, w_g, sc_conv_w[layer], y_ret, y_ssm,
            w_br_ret[layer].astype(BF16), w_br_ssm[layer].astype(BF16), w_br_sc[layer].astype(BF16),
            w_out[layer].astype(BF16), final_g, layer, tile, final=(layer == depth - 1))
    return x
```

```python
import functools
import math

import jax
import jax.numpy as jnp
from jax import lax
from jax.experimental import pallas as pl
from jax.experimental.pallas import tpu as pltpu

F32 = jnp.float32
BF16 = jnp.bfloat16

EPS = 1e-6
RET_HEADS = 4
RET_DIM = 128
RET_WIDTH = RET_HEADS * RET_DIM
ROPE_THETA = 10000.0
SSM_HEADS = 16
SSM_HEAD_DIM = 64
SSM_GROUPS = 2
SSM_STATE = 128
SSM_CONV = 4
SSM_WIDTH = SSM_HEADS * SSM_HEAD_DIM
SSM_XBC = SSM_WIDTH + 2 * SSM_GROUPS * SSM_STATE
SC_CONV = 3
CHUNK = 128
LANES = 128
SUBLANES = 8
DT_PAD = LANES
FILL_BLOCK = 512
VMEM_LIMIT_BYTES = 60 * 1024 * 1024

NEG_INF = float("-inf")


def _sigmoid(x):
    return 1.0 / (1.0 + jnp.exp(-x))


def _silu(x):
    return x * _sigmoid(x)


def _dot(a, b):
    return jnp.dot(a, b, preferred_element_type=F32)


def _modulated_norm(x, g, scale, shift):
    ms = jnp.mean(x * x, axis=-1, keepdims=True)
    y = x * lax.rsqrt(ms + EPS) * g
    return (y * (1.0 + scale) + shift).astype(BF16)


def _lane_bcast(a, col):
    return jnp.broadcast_to(a[:, col:col + 1], a.shape)


def _ada_kernel(c_ref, w_ref, b_ref, o_ref):
    c = c_ref[...]
    o_ref[...] = jnp.dot(_silu(c), w_ref[...], preferred_element_type=F32,
                         precision=lax.Precision.HIGHEST) + b_ref[...]


def _ada_call(c, w_ada, b_ada):
    depth, d, d3 = w_ada.shape
    bsz = c.shape[0]
    nblk = d3 // d
    return pl.pallas_call(
        _ada_kernel,
        grid=(depth, nblk),
        in_specs=[
            pl.BlockSpec((bsz, d), lambda l, n: (0, 0)),
            pl.BlockSpec((None, d, d), lambda l, n: (l, 0, n)),
            pl.BlockSpec((None, 1, d), lambda l, n: (l, 0, n)),
        ],
        out_specs=pl.BlockSpec((None, bsz, d), lambda l, n: (l, 0, n)),
        out_shape=jax.ShapeDtypeStruct((depth, bsz, d3), F32),
        compiler_params=pltpu.CompilerParams(
            dimension_semantics=("parallel", "parallel"), vmem_limit_bytes=VMEM_LIMIT_BYTES),
        name="ada",
    )(c, w_ada, b_ada.reshape(depth, 1, d3))


def _rope_kernel(pos_ref, inv_ref, cos_ref, sin_ref):
    ang = pos_ref[...].astype(F32) * inv_ref[...]
    lane = lax.broadcasted_iota(jnp.int32, ang.shape, 1)
    cos_ref[...] = jnp.cos(ang)
    sin_ref[...] = jnp.where(lane < RET_DIM // 2, -jnp.sin(ang), jnp.sin(ang))


def _rope_call(positions):
    bsz, seq = positions.shape
    half = RET_DIM // 2
    inv_freq = ROPE_THETA ** (-jnp.arange(0, RET_DIM, 2, dtype=F32) / RET_DIM)
    inv2 = jnp.concatenate([inv_freq, inv_freq]).reshape(1, 2 * half)
    tbl = jax.ShapeDtypeStruct((bsz, seq, RET_DIM), F32)
    return pl.pallas_call(
        _rope_kernel,
        grid=(bsz,),
        in_specs=[
            pl.BlockSpec((None, seq, 1), lambda b: (b, 0, 0)),
            pl.BlockSpec((1, RET_DIM), lambda b: (0, 0)),
        ],
        out_specs=[pl.BlockSpec((None, seq, RET_DIM), lambda b: (b, 0, 0))] * 2,
        out_shape=[tbl, tbl],
        compiler_params=pltpu.CompilerParams(
            dimension_semantics=("parallel",), vmem_limit_bytes=VMEM_LIMIT_BYTES),
        name="rope_tables",
    )(positions.reshape(bsz, seq, 1), inv2)


def _retention_branch(h, w_ref, cos, sin, state_ref, y_ref, tile):
    qkvg = _dot(h, w_ref[...])
    row = lax.broadcasted_iota(jnp.int32, (CHUNK, CHUNK), 0)
    col = lax.broadcasted_iota(jnp.int32, (CHUNK, CHUNK), 1)
    rowf = row.astype(F32)
    half = RET_DIM // 2
    qk_scale = RET_DIM ** -0.5

    def rot(t):
        return t * cos + pltpu.roll(t, half, axis=1) * sin

    for hd in range(RET_HEADS):
        log_g = math.log(1.0 - 2.0 ** (-5.0 - hd))
        decay_mask = jnp.exp(jnp.where(row >= col, (row - col).astype(F32) * log_g, NEG_INF))
        q_decay = jnp.exp((rowf + 1.0) * log_g)
        k_decay = jnp.exp((CHUNK - 1.0 - rowf) * log_g)
        chunk_decay = math.exp(CHUNK * log_g)
        lo = hd * RET_DIM
        q = rot(qkvg[:, lo:lo + RET_DIM]) * qk_scale
        k = rot(qkvg[:, RET_WIDTH + lo:RET_WIDTH + lo + RET_DIM])
        v = qkvg[:, 2 * RET_WIDTH + lo:2 * RET_WIDTH + lo + RET_DIM]
        gate = qkvg[:, 3 * RET_WIDTH + lo:3 * RET_WIDTH + lo + RET_DIM]
        state = state_ref[hd]
        for c in range(tile // CHUNK):
            r0 = c * CHUNK
            qc = q[r0:r0 + CHUNK].astype(BF16)
            kc = k[r0:r0 + CHUNK]
            vc = v[r0:r0 + CHUNK].astype(BF16)
            scores = lax.dot_general(qc, kc.astype(BF16), (((1,), (1,)), ((), ())),
                                     preferred_element_type=F32)
            inner = _dot((scores * decay_mask).astype(BF16), vc)
            cross = _dot(qc, state.astype(BF16)) * q_decay
            kd = (kc * k_decay).astype(BF16)
            state = chunk_decay * state + lax.dot_general(kd, vc, (((0,), (0,)), ((), ())),
                                                          preferred_element_type=F32)
            o = inner + cross
            mu = jnp.mean(o, axis=-1, keepdims=True)
            oc = o - mu
            var = jnp.mean(oc * oc, axis=-1, keepdims=True)
            y = oc * lax.rsqrt(var + EPS) * _silu(gate[r0:r0 + CHUNK])
            y_ref[r0:r0 + CHUNK, lo:lo + RET_DIM] = y.astype(y_ref.dtype)
        state_ref[hd] = state


def _block_diag_pair(slab):
    lane = lax.broadcasted_iota(jnp.int32, slab.shape, 1)
    first = lane < SSM_HEAD_DIM
    top = jnp.where(first, slab, 0.0)
    bot = jnp.where(first, 0.0, slab)
    return jnp.concatenate([top, bot], axis=0).astype(BF16)


def _ssd_branch(h, w_ref, convw_ref, convb_ref, dtb_ref, alog_ref, dskip_ref, ng_ref,
                xbc_ref, state_ref, ychunk_ref, y_ref, tile, fillers):
    zxd = _dot(h, w_ref[...])
    z = zxd[:, :SSM_WIDTH]
    dt_raw = zxd[:, SSM_WIDTH + SSM_XBC:]

    xbc_ref[SUBLANES:SUBLANES + tile, :] = zxd[:, SSM_WIDTH:SSM_WIDTH + SSM_XBC]
    conv = convb_ref[...]
    for kk in range(SSM_CONV):
        off = SUBLANES - (SSM_CONV - 1) + kk
        conv = conv + convw_ref[kk:kk + 1, :] * xbc_ref[off:off + tile, :]
    xbc_ref[0:SUBLANES, :] = xbc_ref[tile:tile + SUBLANES, :]
    xbc = _silu(conv)

    dt_in = dt_raw + dtb_ref[...]
    dt = jnp.maximum(dt_in, 0.0) + jnp.log1p(jnp.exp(-jnp.abs(dt_in)))
    a = dt * (-jnp.exp(alog_ref[...]))

    row = lax.broadcasted_iota(jnp.int32, (CHUNK, CHUNK), 0)
    col = lax.broadcasted_iota(jnp.int32, (CHUNK, CHUNK), 1)
    tril = row >= col
    tri_f = tril.astype(F32)
    lane1 = lax.broadcasted_iota(jnp.int32, (1, LANES), 1)
    pair = 2 * SSM_HEAD_DIM
    heads_per_group = SSM_HEADS // SSM_GROUPS

    for c in range(tile // CHUNK):
        r0 = c * CHUNK
        dt_c = dt[r0:r0 + CHUNK]
        a_cs = jnp.dot(tri_f, a[r0:r0 + CHUNK], preferred_element_type=F32,
                       precision=lax.Precision.HIGHEST)
        a_last = a_cs[CHUNK - 1:CHUNK, :]
        a_cs_t = a_cs.T
        dt_t = dt_c.T
        w_t = (jnp.exp(a_last - a_cs) * dt_c).T
        e_acs = jnp.exp(a_cs)
        cdec = jnp.exp(a_last)
        for g in range(SSM_GROUPS):
            b_g = xbc[r0:r0 + CHUNK, SSM_WIDTH + g * SSM_STATE:SSM_WIDTH + (g + 1) * SSM_STATE]
            c0 = SSM_WIDTH + SSM_GROUPS * SSM_STATE + g * SSM_STATE
            c_g = xbc[r0:r0 + CHUNK, c0:c0 + SSM_STATE]
            b_t = b_g.T
            cb = _dot(c_g.astype(BF16), b_t.astype(BF16))
            for p in range(heads_per_group // 2):
                h1 = g * heads_per_group + 2 * p
                lo = h1 * SSM_HEAD_DIM
                xs_pair = xbc[r0:r0 + CHUNK, lo:lo + pair]
                rhs = _block_diag_pair(xs_pair)
                m_parts, s_parts, o_parts = [], [], []
                for hh in (h1, h1 + 1):
                    seg = _lane_bcast(a_cs, hh) - a_cs_t[hh:hh + 1, :]
                    lm = jnp.exp(jnp.where(tril, seg, NEG_INF))
                    m_parts.append(cb * lm * dt_t[hh:hh + 1, :])
                    s_parts.append(b_t * w_t[hh:hh + 1, :])
                    o_parts.append(c_g * _lane_bcast(e_acs, hh))
                y_diag = _dot(jnp.concatenate(m_parts, axis=1).astype(BF16), rhs)
                s_new = _dot(jnp.concatenate(s_parts, axis=1).astype(BF16), rhs)
                s_prev = state_ref[:, lo:lo + pair]
                y_off = _dot(jnp.concatenate(o_parts, axis=1).astype(BF16), _block_diag_pair(s_prev))
                cd_pair = jnp.where(lane1 < SSM_HEAD_DIM, _lane_bcast(cdec, h1), _lane_bcast(cdec, h1 + 1))
                state_ref[:, lo:lo + pair] = cd_pair * s_prev + s_new
                ychunk_ref[:, lo:lo + pair] = y_diag + y_off + dskip_ref[:, lo:lo + pair] * xs_pair
                _run_one(fillers)
        y = ychunk_ref[...] * _silu(z[r0:r0 + CHUNK])
        ms = jnp.mean(y * y, axis=-1, keepdims=True)
        y_ref[r0:r0 + CHUNK, :] = (y * lax.rsqrt(ms + EPS) * ng_ref[...]).astype(y_ref.dtype)


def _run_one(fillers):
    if fillers:
        fillers.pop(0)()


def _short_conv_branch(cp_ref, scw_ref, u_ref, tile):
    width = u_ref.shape[1]
    cp = cp_ref[...]
    u_ref[SUBLANES:SUBLANES + tile, :] = cp[:, width:2 * width] * cp[:, 2 * width:3 * width]
    conv = jnp.zeros((tile, width), F32)
    for kk in range(SC_CONV):
        off = SUBLANES - (SC_CONV - 1) + kk
        conv = conv + scw_ref[kk:kk + 1, :] * u_ref[off:off + tile, :]
    u_ref[0:SUBLANES, :] = u_ref[tile:tile + SUBLANES, :]
    return (cp[:, :width] * conv * _silu(cp[:, 3 * width:])).astype(BF16)


def _layer_kernel(x_ref, g_ref, scale_ref, shift_ref, rgate_ref, cos_ref, sin_ref,
                  wr_ref, ws_ref, wc_ref, wg_ref,
                  convw_ref, convb_ref, dtb_ref, alog_ref, dskip_ref, sng_ref, scw_ref,
                  wbr_ret_ref, wbr_ssm_ref, wbr_sc_ref, wout_ref, fg_ref,
                  o_ref,
                  rstate_ref, xbc_ref, sstate_ref, ychunk_ref, u_ref, yret_ref, yssm_ref, cp_ref, gates_ref,
                  *, tile, final):
    @pl.when(pl.program_id(1) == 0)
    def _():
        rstate_ref[...] = jnp.zeros_like(rstate_ref)
        sstate_ref[...] = jnp.zeros_like(sstate_ref)
        xbc_ref[0:SUBLANES, :] = jnp.zeros((SUBLANES, xbc_ref.shape[1]), F32)
        u_ref[0:SUBLANES, :] = jnp.zeros((SUBLANES, u_ref.shape[1]), F32)

    x = x_ref[...]
    d = x.shape[1]
    h = _modulated_norm(x, g_ref[...], scale_ref[...], shift_ref[...])

    def proj_block(dst_ref, w_ref, n, act):
        def run():
            cols = slice(n * FILL_BLOCK, (n + 1) * FILL_BLOCK)
            dst_ref[:, cols] = act(_dot(h, w_ref[:, cols]))
        return run

    fillers = [proj_block(cp_ref, wc_ref, n, lambda t: t) for n in range(cp_ref.shape[1] // FILL_BLOCK)]
    fillers += [proj_block(gates_ref, wg_ref, n, _sigmoid) for n in range(gates_ref.shape[1] // FILL_BLOCK)]

    _retention_branch(h, wr_ref, cos_ref[...], sin_ref[...], rstate_ref, yret_ref, tile)
    _ssd_branch(h, ws_ref, convw_ref, convb_ref, dtb_ref, alog_ref, dskip_ref, sng_ref,
                xbc_ref, sstate_ref, ychunk_ref, yssm_ref, tile, fillers)
    while fillers:
        _run_one(fillers)
    y_sc = _short_conv_branch(cp_ref, scw_ref, u_ref, tile)

    merged = (gates_ref[:, :d] * _dot(yret_ref[...], wbr_ret_ref[...])
              + gates_ref[:, d:2 * d] * _dot(yssm_ref[...], wbr_ssm_ref[...])
              + gates_ref[:, 2 * d:] * _dot(y_sc, wbr_sc_ref[...]))
    out = _dot(merged.astype(BF16), wout_ref[...])
    xn = x + rgate_ref[...] * out
    if final:
        ms = jnp.mean(xn * xn, axis=-1, keepdims=True)
        xn = xn * lax.rsqrt(ms + EPS) * fg_ref[...]
    o_ref[...] = xn


def _layer_call(x, ada5, norm_g3, cos2, sin2, weights, smalls, layer, tile, final):
    bsz, seq, d = x.shape
    row_spec = lambda idx: pl.BlockSpec((None, None, None, 1, d), lambda b, j: (layer, b, idx, 0, 0))
    const = lambda a: pl.BlockSpec(a.shape, lambda b, j: (0,) * a.ndim, pipeline_mode=pl.Buffered(1))
    tok = lambda w: pl.BlockSpec((None, tile, w), lambda b, j: (b, j, 0))
    w_r, w_s, w_c, w_g, wbr_ret, wbr_ssm, wbr_sc, w_out = weights
    conv_w, conv_b, dt_bias, a_log, d_skip, ssm_ng, sc_w, final_g = smalls
    width = sc_w.shape[1]
    return pl.pallas_call(
        functools.partial(_layer_kernel, tile=tile, final=final),
        grid=(bsz, seq // tile),
        in_specs=[
            tok(d),
            pl.BlockSpec((None, 1, d), lambda b, j: (layer, 0, 0)),
            row_spec(1),
            row_spec(0),
            row_spec(2),
            tok(RET_DIM), tok(RET_DIM),
            const(w_r), const(w_s), const(w_c), const(w_g),
            const(conv_w), const(conv_b), const(dt_bias), const(a_log), const(d_skip), const(ssm_ng),
            const(sc_w),
            const(wbr_ret), const(wbr_ssm), const(wbr_sc), const(w_out), const(final_g),
        ],
        out_specs=tok(d),
        out_shape=jax.ShapeDtypeStruct((bsz, seq, d), F32),
        scratch_shapes=[
            pltpu.VMEM((RET_HEADS, RET_DIM, RET_DIM), F32),
            pltpu.VMEM((tile + SUBLANES, SSM_XBC), F32),
            pltpu.VMEM((SSM_STATE, SSM_WIDTH), F32),
            pltpu.VMEM((CHUNK, SSM_WIDTH), F32),
            pltpu.VMEM((tile + SUBLANES, width), F32),
            pltpu.VMEM((tile, RET_WIDTH), BF16),
            pltpu.VMEM((tile, SSM_WIDTH), BF16),
            pltpu.VMEM((tile, 4 * width), F32),
            pltpu.VMEM((tile, 3 * d), F32),
        ],
        compiler_params=pltpu.CompilerParams(
            dimension_semantics=("parallel", "arbitrary"), vmem_limit_bytes=VMEM_LIMIT_BYTES),
        name="layer",
    )(x, norm_g3, ada5, ada5, ada5, cos2, sin2, w_r, w_s, w_c, w_g,
      conv_w, conv_b, dt_bias, a_log, d_skip, ssm_ng, sc_w, wbr_ret, wbr_ssm, wbr_sc, w_out, final_g)


def kernel(x, c, positions, norm_g, w_ada, b_ada, w_in, ssm_conv_w, ssm_conv_b, ssm_dt_bias, ssm_a_log,
           ssm_d, ssm_norm_g, sc_conv_w, w_br_ret, w_br_ssm, w_br_sc, w_out, final_norm_g):
    bsz, seq, d = x.shape
    depth = w_in.shape[0]
    sc_width = sc_conv_w.shape[2]
    tile = 256
    assert seq % tile == 0 and tile % CHUNK == 0

    sizes = (RET_WIDTH, RET_WIDTH, RET_WIDTH, RET_WIDTH, SSM_WIDTH, SSM_XBC, SSM_HEADS,
             sc_width, sc_width, sc_width, sc_width, 3 * d)
    offs = [0]
    for s in sizes:
        offs.append(offs[-1] + s)
    assert offs[-1] == w_in.shape[2]

    ada = _ada_call(c, w_ada, b_ada)
    ada5 = ada.reshape(depth, bsz, 3, 1, d)
    cos2, sin2 = _rope_call(positions)
    norm_g3 = norm_g.reshape(depth, 1, d)
    final_g = final_norm_g.reshape(1, d)
    pad_heads = lambda v: jnp.pad(v, (0, DT_PAD - SSM_HEADS)).reshape(1, DT_PAD)

    for layer in range(depth):
        w = w_in[layer]
        weights = (
            w[:, offs[0]:offs[4]].astype(BF16),
            jnp.concatenate(
                [w[:, offs[4]:offs[7]], jnp.zeros((d, DT_PAD - SSM_HEADS), w.dtype)], axis=1).astype(BF16),
            w[:, offs[7]:offs[11]].astype(BF16),
            w[:, offs[11]:offs[12]].astype(BF16),
            w_br_ret[layer].astype(BF16), w_br_ssm[layer].astype(BF16), w_br_sc[layer].astype(BF16),
            w_out[layer].astype(BF16),
        )
        smalls = (
            ssm_conv_w[layer], ssm_conv_b[layer].reshape(1, SSM_XBC),
            pad_heads(ssm_dt_bias[layer]), pad_heads(ssm_a_log[layer]),
            jnp.repeat(ssm_d[layer], SSM_HEAD_DIM).reshape(1, SSM_WIDTH),
            ssm_norm_g[layer].reshape(1, SSM_WIDTH), sc_conv_w[layer], final_g,
        )
        x = _layer_call(x, ada5, norm_g3, cos2, sin2, weights, smalls, layer, tile,
                        final=(layer == depth - 1))
    return x
```

```python
import functools
import math

import jax
import jax.numpy as jnp
from jax import lax
from jax.experimental import pallas as pl
from jax.experimental.pallas import tpu as pltpu

F32 = jnp.float32
BF16 = jnp.bfloat16

EPS = 1e-6
RET_HEADS = 4
RET_DIM = 128
RET_WIDTH = RET_HEADS * RET_DIM
ROPE_THETA = 10000.0
SSM_HEADS = 16
SSM_HEAD_DIM = 64
SSM_GROUPS = 2
SSM_STATE = 128
SSM_CONV = 4
SSM_WIDTH = SSM_HEADS * SSM_HEAD_DIM
SSM_XBC = SSM_WIDTH + 2 * SSM_GROUPS * SSM_STATE
SC_CONV = 3
CHUNK = 128
LANES = 128
SUBLANES = 8
DT_PAD = LANES
FILL_BLOCK = 512
VMEM_LIMIT_BYTES = 60 * 1024 * 1024

NEG_INF = float("-inf")


def _sigmoid(x):
    return 1.0 / (1.0 + jnp.exp(-x))


def _silu(x):
    return x * _sigmoid(x)


def _dot(a, b):
    return jnp.dot(a, b, preferred_element_type=F32)


def _modulated_norm(x, g, scale, shift):
    ms = jnp.mean(x * x, axis=-1, keepdims=True)
    y = x * lax.rsqrt(ms + EPS) * g
    return (y * (1.0 + scale) + shift).astype(BF16)


def _lane_bcast(a, col):
    return jnp.broadcast_to(a[:, col:col + 1], a.shape)


def _ada_kernel(c_ref, w_ref, b_ref, o_ref):
    c = c_ref[...]
    o_ref[...] = jnp.dot(_silu(c), w_ref[...], preferred_element_type=F32,
                         precision=lax.Precision.HIGHEST) + b_ref[...]


def _ada_call(c, w_ada, b_ada):
    depth, d, d3 = w_ada.shape
    bsz = c.shape[0]
    nblk = d3 // d
    return pl.pallas_call(
        _ada_kernel,
        grid=(depth, nblk),
        in_specs=[
            pl.BlockSpec((bsz, d), lambda l, n: (0, 0)),
            pl.BlockSpec((None, d, d), lambda l, n: (l, 0, n)),
            pl.BlockSpec((None, 1, d), lambda l, n: (l, 0, n)),
        ],
        out_specs=pl.BlockSpec((None, bsz, d), lambda l, n: (l, 0, n)),
        out_shape=jax.ShapeDtypeStruct((depth, bsz, d3), F32),
        compiler_params=pltpu.CompilerParams(
            dimension_semantics=("parallel", "parallel"), vmem_limit_bytes=VMEM_LIMIT_BYTES),
        name="ada",
    )(c, w_ada, b_ada.reshape(depth, 1, d3))


def _rope_kernel(pos_ref, inv_ref, cos_ref, sin_ref):
    ang = pos_ref[...].astype(F32) * inv_ref[...]
    lane = lax.broadcasted_iota(jnp.int32, ang.shape, 1)
    cos_ref[...] = jnp.cos(ang)
    sin_ref[...] = jnp.where(lane < RET_DIM // 2, -jnp.sin(ang), jnp.sin(ang))


def _rope_call(positions):
    bsz, seq = positions.shape
    half = RET_DIM // 2
    inv_freq = ROPE_THETA ** (-jnp.arange(0, RET_DIM, 2, dtype=F32) / RET_DIM)
    inv2 = jnp.concatenate([inv_freq, inv_freq]).reshape(1, 2 * half)
    tbl = jax.ShapeDtypeStruct((bsz, seq, RET_DIM), F32)
    return pl.pallas_call(
        _rope_kernel,
        grid=(bsz,),
        in_specs=[
            pl.BlockSpec((None, seq, 1), lambda b: (b, 0, 0)),
            pl.BlockSpec((1, RET_DIM), lambda b: (0, 0)),
        ],
        out_specs=[pl.BlockSpec((None, seq, RET_DIM), lambda b: (b, 0, 0))] * 2,
        out_shape=[tbl, tbl],
        compiler_params=pltpu.CompilerParams(
            dimension_semantics=("parallel",), vmem_limit_bytes=VMEM_LIMIT_BYTES),
        name="rope_tables",
    )(positions.reshape(bsz, seq, 1), inv2)


def _run_fillers(fillers, slots_left):
    n = -(-len(fillers) // max(slots_left, 1))
    for _ in range(min(n, len(fillers))):
        fillers.pop(0)()


def _proj_block(dst_ref, h, w_ref, dst_lo, w_lo, width, act=None):
    def run():
        r = _dot(h, w_ref[:, w_lo:w_lo + width])
        dst_ref[:, dst_lo:dst_lo + width] = r if act is None else act(r)
    return run


def _retention_unit(hd, c, qkvg_ref, cos_ref, sin_ref, state_ref, y_ref):
    def run():
        row = lax.broadcasted_iota(jnp.int32, (CHUNK, CHUNK), 0)
        col = lax.broadcasted_iota(jnp.int32, (CHUNK, CHUNK), 1)
        rowf = row.astype(F32)
        half = RET_DIM // 2
        log_g = math.log(1.0 - 2.0 ** (-5.0 - hd))
        decay_mask = jnp.exp(jnp.where(row >= col, (row - col).astype(F32) * log_g, NEG_INF))
        q_decay = jnp.exp((rowf + 1.0) * log_g)
        k_decay = jnp.exp((CHUNK - 1.0 - rowf) * log_g)
        chunk_decay = math.exp(CHUNK * log_g)
        rows = slice(c * CHUNK, (c + 1) * CHUNK)
        lo = hd * RET_DIM
        cos = cos_ref[rows, :]
        sin = sin_ref[rows, :]

        def rot(t):
            return t * cos + pltpu.roll(t, half, axis=1) * sin

        qc = (rot(qkvg_ref[rows, lo:lo + RET_DIM]) * (RET_DIM ** -0.5)).astype(BF16)
        kc = rot(qkvg_ref[rows, RET_WIDTH + lo:RET_WIDTH + lo + RET_DIM])
        vc = qkvg_ref[rows, 2 * RET_WIDTH + lo:2 * RET_WIDTH + lo + RET_DIM].astype(BF16)
        gate = qkvg_ref[rows, 3 * RET_WIDTH + lo:3 * RET_WIDTH + lo + RET_DIM]
        state = state_ref[hd]
        scores = lax.dot_general(qc, kc.astype(BF16), (((1,), (1,)), ((), ())), preferred_element_type=F32)
        inner = _dot((scores * decay_mask).astype(BF16), vc)
        cross = _dot(qc, state.astype(BF16)) * q_decay
        kd = (kc * k_decay).astype(BF16)
        state_ref[hd] = chunk_decay * state + lax.dot_general(kd, vc, (((0,), (0,)), ((), ())),
                                                              preferred_element_type=F32)
        o = inner + cross
        mu = jnp.mean(o, axis=-1, keepdims=True)
        oc = o - mu
        var = jnp.mean(oc * oc, axis=-1, keepdims=True)
        y = oc * lax.rsqrt(var + EPS) * _silu(gate)
        y_ref[rows, lo:lo + RET_DIM] = y.astype(y_ref.dtype)
    return run


def _block_diag_pair(slab):
    lane = lax.broadcasted_iota(jnp.int32, slab.shape, 1)
    first = lane < SSM_HEAD_DIM
    top = jnp.where(first, slab, 0.0)
    bot = jnp.where(first, 0.0, slab)
    return jnp.concatenate([top, bot], axis=0).astype(BF16)


def _ssd_conv(xbc_ref, xact_ref, convw_ref, convb_ref, tile, fillers):
    nblk = SSM_XBC // FILL_BLOCK
    for n in range(nblk):
        cols = slice(n * FILL_BLOCK, (n + 1) * FILL_BLOCK)
        conv = convb_ref[:, cols]
        for kk in range(SSM_CONV):
            off = SUBLANES - (SSM_CONV - 1) + kk
            conv = conv + convw_ref[kk:kk + 1, cols] * xbc_ref[off:off + tile, cols]
        xact_ref[:, cols] = _silu(conv)
        _run_fillers(fillers, nblk - n)
    xbc_ref[0:SUBLANES, :] = xbc_ref[tile:tile + SUBLANES, :]


def _ssd_scan(xact_ref, z_ref, dt_ref, dtb_ref, alog_ref, dskip_ref, ng_ref,
              state_ref, ychunk_ref, y_ref, tile, fillers):
    row = lax.broadcasted_iota(jnp.int32, (CHUNK, CHUNK), 0)
    col = lax.broadcasted_iota(jnp.int32, (CHUNK, CHUNK), 1)
    tril = row >= col
    tri_f = tril.astype(F32)
    lane1 = lax.broadcasted_iota(jnp.int32, (1, LANES), 1)
    pair = 2 * SSM_HEAD_DIM
    heads_per_group = SSM_HEADS // SSM_GROUPS
    n_chunks = tile // CHUNK
    slots = n_chunks * SSM_HEADS // 2
    neg_a = -jnp.exp(alog_ref[...])

    for c in range(n_chunks):
        rows = slice(c * CHUNK, (c + 1) * CHUNK)
        dt_in = dt_ref[rows, :] + dtb_ref[...]
        dt_c = jnp.maximum(dt_in, 0.0) + jnp.log1p(jnp.exp(-jnp.abs(dt_in)))
        a_cs = jnp.dot(tri_f, dt_c * neg_a, preferred_element_type=F32,
                       precision=lax.Precision.HIGHEST)
        a_last = a_cs[CHUNK - 1:CHUNK, :]
        a_cs_t = a_cs.T
        dt_t = dt_c.T
        w_t = (jnp.exp(a_last - a_cs) * dt_c).T
        e_acs = jnp.exp(a_cs)
        cdec = jnp.exp(a_last)
        for g in range(SSM_GROUPS):
            b_lo = SSM_WIDTH + g * SSM_STATE
            c_lo = SSM_WIDTH + SSM_GROUPS * SSM_STATE + g * SSM_STATE
            b_g = xact_ref[rows, b_lo:b_lo + SSM_STATE]
            c_g = xact_ref[rows, c_lo:c_lo + SSM_STATE]
            b_t = b_g.T
            cb = _dot(c_g.astype(BF16), b_t.astype(BF16))
            for p in range(heads_per_group // 2):
                h1 = g * heads_per_group + 2 * p
                lo = h1 * SSM_HEAD_DIM
                xs_pair = xact_ref[rows, lo:lo + pair]
                rhs = _block_diag_pair(xs_pair)
                m_parts, s_parts, o_parts = [], [], []
                for hh in (h1, h1 + 1):
                    seg = _lane_bcast(a_cs, hh) - a_cs_t[hh:hh + 1, :]
                    lm = jnp.exp(jnp.where(tril, seg, NEG_INF))
                    m_parts.append(cb * lm * dt_t[hh:hh + 1, :])
                    s_parts.append(b_t * w_t[hh:hh + 1, :])
                    o_parts.append(c_g * _lane_bcast(e_acs, hh))
                y_diag = _dot(jnp.concatenate(m_parts, axis=1).astype(BF16), rhs)
                s_new = _dot(jnp.concatenate(s_parts, axis=1).astype(BF16), rhs)
                s_prev = state_ref[:, lo:lo + pair]
                y_off = _dot(jnp.concatenate(o_parts, axis=1).astype(BF16), _block_diag_pair(s_prev))
                cd_pair = jnp.where(lane1 < SSM_HEAD_DIM, _lane_bcast(cdec, h1), _lane_bcast(cdec, h1 + 1))
                state_ref[:, lo:lo + pair] = cd_pair * s_prev + s_new
                ychunk_ref[:, lo:lo + pair] = y_diag + y_off + dskip_ref[:, lo:lo + pair] * xs_pair
                _run_fillers(fillers, slots)
                slots -= 1
        y = ychunk_ref[...] * _silu(z_ref[rows, :])
        ms = jnp.mean(y * y, axis=-1, keepdims=True)
        y_ref[rows, :] = (y * lax.rsqrt(ms + EPS) * ng_ref[...]).astype(y_ref.dtype)


def _short_conv_task(cp_ref, scw_ref, u_ref, y_ref, tile):
    def run():
        width = u_ref.shape[1]
        u_ref[SUBLANES:SUBLANES + tile, :] = cp_ref[:, width:2 * width] * cp_ref[:, 2 * width:3 * width]
        conv = jnp.zeros((tile, width), F32)
        for kk in range(SC_CONV):
            off = SUBLANES - (SC_CONV - 1) + kk
            conv = conv + scw_ref[kk:kk + 1, :] * u_ref[off:off + tile, :]
        u_ref[0:SUBLANES, :] = u_ref[tile:tile + SUBLANES, :]
        y_ref[...] = (cp_ref[:, :width] * conv * _silu(cp_ref[:, 3 * width:])).astype(y_ref.dtype)
    return run


def _layer_kernel(x_ref, g_ref, scale_ref, shift_ref, rgate_ref, cos_ref, sin_ref,
                  wr_ref, ws_ref, wc_ref, wg_ref,
                  convw_ref, convb_ref, dtb_ref, alog_ref, dskip_ref, sng_ref, scw_ref,
                  wbr_ret_ref, wbr_ssm_ref, wbr_sc_ref, wout_ref, fg_ref,
                  o_ref,
                  rstate_ref, xbc_ref, sstate_ref, ychunk_ref, u_ref, yret_ref, yssm_ref, ysc_ref,
                  qkvg_ref, z_ref, dt_ref, xact_ref, cp_ref, gates_ref, acc_ref,
                  *, tile, final):
    @pl.when(pl.program_id(1) == 0)
    def _():
        rstate_ref[...] = jnp.zeros_like(rstate_ref)
        sstate_ref[...] = jnp.zeros_like(sstate_ref)
        xbc_ref[0:SUBLANES, :] = jnp.zeros((SUBLANES, xbc_ref.shape[1]), F32)
        u_ref[0:SUBLANES, :] = jnp.zeros((SUBLANES, u_ref.shape[1]), F32)

    x = x_ref[...]
    d = x.shape[1]
    h = _modulated_norm(x, g_ref[...], scale_ref[...], shift_ref[...])
    blk = FILL_BLOCK

    xbc_ref[SUBLANES:SUBLANES + tile, :] = _dot(h, ws_ref[:, SSM_WIDTH:SSM_WIDTH + SSM_XBC])
    fillers = [_proj_block(z_ref, h, ws_ref, n * blk, n * blk, blk) for n in range(SSM_WIDTH // blk)]
    fillers.append(_proj_block(dt_ref, h, ws_ref, 0, SSM_WIDTH + SSM_XBC, DT_PAD))
    fillers += [_proj_block(qkvg_ref, h, wr_ref, n * blk, n * blk, blk) for n in range(4 * RET_WIDTH // blk)]
    _ssd_conv(xbc_ref, xact_ref, convw_ref, convb_ref, tile, fillers)

    def ret_merge():
        acc_ref[...] = gates_ref[:, :d] * _dot(yret_ref[...], wbr_ret_ref[...])

    def sc_merge():
        acc_ref[...] += gates_ref[:, 2 * d:] * _dot(ysc_ref[...], wbr_sc_ref[...])

    fillers = [_retention_unit(hd, c, qkvg_ref, cos_ref, sin_ref, rstate_ref, yret_ref)
               for c in range(tile // CHUNK) for hd in range(RET_HEADS)]
    fillers += [_proj_block(cp_ref, h, wc_ref, n * blk, n * blk, blk) for n in range(cp_ref.shape[1] // blk)]
    fillers += [_proj_block(gates_ref, h, wg_ref, n * blk, n * blk, blk, _sigmoid) for n in range(3 * d // blk)]
    fillers += [ret_merge, _short_conv_task(cp_ref, scw_ref, u_ref, ysc_ref, tile), sc_merge]
    _ssd_scan(xact_ref, z_ref, dt_ref, dtb_ref, alog_ref, dskip_ref, sng_ref,
              sstate_ref, ychunk_ref, yssm_ref, tile, fillers)
    _run_fillers(fillers, 1)

    merged = acc_ref[...] + gates_ref[:, d:2 * d] * _dot(yssm_ref[...], wbr_ssm_ref[...])
    out = _dot(merged.astype(BF16), wout_ref[...])
    xn = x + rgate_ref[...] * out
    if final:
        ms = jnp.mean(xn * xn, axis=-1, keepdims=True)
        xn = xn * lax.rsqrt(ms + EPS) * fg_ref[...]
    o_ref[...] = xn


def _layer_call(x, ada5, norm_g3, cos2, sin2, weights, smalls, layer, tile, final):
    bsz, seq, d = x.shape
    row_spec = lambda idx: pl.BlockSpec((None, None, None, 1, d), lambda b, j: (layer, b, idx, 0, 0))
    const = lambda a: pl.BlockSpec(a.shape, lambda b, j: (0,) * a.ndim, pipeline_mode=pl.Buffered(1))
    tok = lambda w: pl.BlockSpec((None, tile, w), lambda b, j: (b, j, 0))
    w_r, w_s, w_c, w_g, wbr_ret, wbr_ssm, wbr_sc, w_out = weights
    conv_w, conv_b, dt_bias, a_log, d_skip, ssm_ng, sc_w, final_g = smalls
    width = sc_w.shape[1]
    return pl.pallas_call(
        functools.partial(_layer_kernel, tile=tile, final=final),
        grid=(bsz, seq // tile),
        in_specs=[
            tok(d),
            pl.BlockSpec((None, 1, d), lambda b, j: (layer, 0, 0)),
            row_spec(1),
            row_spec(0),
            row_spec(2),
            tok(RET_DIM), tok(RET_DIM),
            const(w_r), const(w_s), const(w_c), const(w_g),
            const(conv_w), const(conv_b), const(dt_bias), const(a_log), const(d_skip), const(ssm_ng),
            const(sc_w),
            const(wbr_ret), const(wbr_ssm), const(wbr_sc), const(w_out), const(final_g),
        ],
        out_specs=tok(d),
        out_shape=jax.ShapeDtypeStruct((bsz, seq, d), F32),
        scratch_shapes=[
            pltpu.VMEM((RET_HEADS, RET_DIM, RET_DIM), F32),
            pltpu.VMEM((tile + SUBLANES, SSM_XBC), F32),
            pltpu.VMEM((SSM_STATE, SSM_WIDTH), F32),
            pltpu.VMEM((CHUNK, SSM_WIDTH), F32),
            pltpu.VMEM((tile + SUBLANES, width), F32),
            pltpu.VMEM((tile, RET_WIDTH), BF16),
            pltpu.VMEM((tile, SSM_WIDTH), BF16),
            pltpu.VMEM((tile, width), BF16),
            pltpu.VMEM((tile, 4 * RET_WIDTH), F32),
            pltpu.VMEM((tile, SSM_WIDTH), F32),
            pltpu.VMEM((tile, DT_PAD), F32),
            pltpu.VMEM((tile, SSM_XBC), F32),
            pltpu.VMEM((tile, 4 * width), F32),
            pltpu.VMEM((tile, 3 * d), F32),
            pltpu.VMEM((tile, d), F32),
        ],
        compiler_params=pltpu.CompilerParams(
            dimension_semantics=("parallel", "arbitrary"), vmem_limit_bytes=VMEM_LIMIT_BYTES),
        name="layer",
    )(x, norm_g3, ada5, ada5, ada5, cos2, sin2, w_r, w_s, w_c, w_g,
      conv_w, conv_b, dt_bias, a_log, d_skip, ssm_ng, sc_w, wbr_ret, wbr_ssm, wbr_sc, w_out, final_g)


def kernel(x, c, positions, norm_g, w_ada, b_ada, w_in, ssm_conv_w, ssm_conv_b, ssm_dt_bias, ssm_a_log,
           ssm_d, ssm_norm_g, sc_conv_w, w_br_ret, w_br_ssm, w_br_sc, w_out, final_norm_g):
    bsz, seq, d = x.shape
    depth = w_in.shape[0]
    sc_width = sc_conv_w.shape[2]
    tile = 256
    assert seq % tile == 0 and tile % CHUNK == 0

    sizes = (RET_WIDTH, RET_WIDTH, RET_WIDTH, RET_WIDTH, SSM_WIDTH, SSM_XBC, SSM_HEADS,
             sc_width, sc_width, sc_width, sc_width, 3 * d)
    offs = [0]
    for s in sizes:
        offs.append(offs[-1] + s)
    assert offs[-1] == w_in.shape[2]

    ada = _ada_call(c, w_ada, b_ada)
    ada5 = ada.reshape(depth, bsz, 3, 1, d)
    cos2, sin2 = _rope_call(positions)
    norm_g3 = norm_g.reshape(depth, 1, d)
    final_g = final_norm_g.reshape(1, d)
    pad_heads = lambda v: jnp.pad(v, (0, DT_PAD - SSM_HEADS)).reshape(1, DT_PAD)

    for layer in range(depth):
        w = w_in[layer]
        weights = (
            w[:, offs[0]:offs[4]].astype(BF16),
            jnp.concatenate(
                [w[:, offs[4]:offs[7]], jnp.zeros((d, DT_PAD - SSM_HEADS), w.dtype)], axis=1).astype(BF16),
            w[:, offs[7]:offs[11]].astype(BF16),
            w[:, offs[11]:offs[12]].astype(BF16),
            w_br_ret[layer].astype(BF16), w_br_ssm[layer].astype(BF16), w_br_sc[layer].astype(BF16),
            w_out[layer].astype(BF16),
        )
        smalls = (
            ssm_conv_w[layer], ssm_conv_b[layer].reshape(1, SSM_XBC),
            pad_heads(ssm_dt_bias[layer]), pad_heads(ssm_a_log[layer]),
            jnp.repeat(ssm_d[layer], SSM_HEAD_DIM).reshape(1, SSM_WIDTH),
            ssm_norm_g[layer].reshape(1, SSM_WIDTH), sc_conv_w[layer], final_g,
        )
        x = _layer_call(x, ada5, norm_g3, cos2, sin2, weights, smalls, layer, tile,
                        final=(layer == depth - 1))
    return x
```

```python
import functools
import math

import jax
import jax.numpy as jnp
from jax import lax
from jax.experimental import pallas as pl
from jax.experimental.pallas import tpu as pltpu

F32 = jnp.float32
BF16 = jnp.bfloat16

EPS = 1e-6
RET_HEADS = 4
RET_DIM = 128
RET_WIDTH = RET_HEADS * RET_DIM
ROPE_THETA = 10000.0
SSM_HEADS = 16
SSM_HEAD_DIM = 64
SSM_GROUPS = 2
SSM_STATE = 128
SSM_CONV = 4
SSM_WIDTH = SSM_HEADS * SSM_HEAD_DIM
SSM_XBC = SSM_WIDTH + 2 * SSM_GROUPS * SSM_STATE
SC_CONV = 3
CHUNK = 128
LANES = 128
SUBLANES = 8
DT_PAD = LANES
FILL_BLOCK = 512
VMEM_LIMIT_BYTES = 60 * 1024 * 1024

NEG_INF = float("-inf")


def _sigmoid(x):
    return 1.0 / (1.0 + jnp.exp(-x))


def _silu(x):
    return x * _sigmoid(x)


def _dot(a, b):
    return jnp.dot(a, b, preferred_element_type=F32)


def _modulated_norm(x, g, scale, shift):
    ms = jnp.mean(x * x, axis=-1, keepdims=True)
    y = x * lax.rsqrt(ms + EPS) * g
    return (y * (1.0 + scale) + shift).astype(BF16)


def _lane_bcast(a, col):
    return jnp.broadcast_to(a[:, col:col + 1], a.shape)


def _ada_kernel(c_ref, w_ref, b_ref, o_ref):
    c = c_ref[...]
    o_ref[...] = jnp.dot(_silu(c), w_ref[...], preferred_element_type=F32,
                         precision=lax.Precision.HIGHEST) + b_ref[...]


def _ada_call(c, w_ada, b_ada):
    depth, d, d3 = w_ada.shape
    bsz = c.shape[0]
    nblk = d3 // d
    return pl.pallas_call(
        _ada_kernel,
        grid=(depth, nblk),
        in_specs=[
            pl.BlockSpec((bsz, d), lambda l, n: (0, 0)),
            pl.BlockSpec((None, d, d), lambda l, n: (l, 0, n)),
            pl.BlockSpec((None, 1, d), lambda l, n: (l, 0, n)),
        ],
        out_specs=pl.BlockSpec((None, bsz, d), lambda l, n: (l, 0, n)),
        out_shape=jax.ShapeDtypeStruct((depth, bsz, d3), F32),
        compiler_params=pltpu.CompilerParams(
            dimension_semantics=("parallel", "parallel"), vmem_limit_bytes=VMEM_LIMIT_BYTES),
        name="ada",
    )(c, w_ada, b_ada.reshape(depth, 1, d3))


def _rope_kernel(pos_ref, inv_ref, cos_ref, sin_ref):
    ang = pos_ref[...].astype(F32) * inv_ref[...]
    lane = lax.broadcasted_iota(jnp.int32, ang.shape, 1)
    cos_ref[...] = jnp.cos(ang)
    sin_ref[...] = jnp.where(lane < RET_DIM // 2, -jnp.sin(ang), jnp.sin(ang))


def _rope_call(positions):
    bsz, seq = positions.shape
    half = RET_DIM // 2
    inv_freq = ROPE_THETA ** (-jnp.arange(0, RET_DIM, 2, dtype=F32) / RET_DIM)
    inv2 = jnp.concatenate([inv_freq, inv_freq]).reshape(1, 2 * half)
    tbl = jax.ShapeDtypeStruct((bsz, seq, RET_DIM), F32)
    return pl.pallas_call(
        _rope_kernel,
        grid=(bsz,),
        in_specs=[
            pl.BlockSpec((None, seq, 1), lambda b: (b, 0, 0)),
            pl.BlockSpec((1, RET_DIM), lambda b: (0, 0)),
        ],
        out_specs=[pl.BlockSpec((None, seq, RET_DIM), lambda b: (b, 0, 0))] * 2,
        out_shape=[tbl, tbl],
        compiler_params=pltpu.CompilerParams(
            dimension_semantics=("parallel",), vmem_limit_bytes=VMEM_LIMIT_BYTES),
        name="rope_tables",
    )(positions.reshape(bsz, seq, 1), inv2)


def _run_fillers(fillers, slots_left):
    n = -(-len(fillers) // max(slots_left, 1))
    for _ in range(min(n, len(fillers))):
        fillers.pop(0)()


def _proj_block(dst_ref, h, w_ref, dst_lo, w_lo, width, act=None):
    def run():
        r = _dot(h, w_ref[:, w_lo:w_lo + width])
        dst_ref[:, dst_lo:dst_lo + width] = r if act is None else act(r)
    return run


def _retention_unit(hd, c, qkvg_ref, cos_ref, sin_ref, state_ref, y_ref):
    def run():
        row = lax.broadcasted_iota(jnp.int32, (CHUNK, CHUNK), 0)
        col = lax.broadcasted_iota(jnp.int32, (CHUNK, CHUNK), 1)
        rowf = row.astype(F32)
        half = RET_DIM // 2
        log_g = math.log(1.0 - 2.0 ** (-5.0 - hd))
        decay_mask = jnp.exp(jnp.where(row >= col, (row - col).astype(F32) * log_g, NEG_INF))
        q_decay = jnp.exp((rowf + 1.0) * log_g)
        k_decay = jnp.exp((CHUNK - 1.0 - rowf) * log_g)
        chunk_decay = math.exp(CHUNK * log_g)
        rows = slice(c * CHUNK, (c + 1) * CHUNK)
        lo = hd * RET_DIM
        cos = cos_ref[rows, :]
        sin = sin_ref[rows, :]

        def rot(t):
            return t * cos + pltpu.roll(t, half, axis=1) * sin

        qc = (rot(qkvg_ref[rows, lo:lo + RET_DIM]) * (RET_DIM ** -0.5)).astype(BF16)
        kc = rot(qkvg_ref[rows, RET_WIDTH + lo:RET_WIDTH + lo + RET_DIM])
        vc = qkvg_ref[rows, 2 * RET_WIDTH + lo:2 * RET_WIDTH + lo + RET_DIM].astype(BF16)
        gate = qkvg_ref[rows, 3 * RET_WIDTH + lo:3 * RET_WIDTH + lo + RET_DIM]
        state = state_ref[hd]
        scores = lax.dot_general(qc, kc.astype(BF16), (((1,), (1,)), ((), ())), preferred_element_type=F32)
        inner = _dot((scores * decay_mask).astype(BF16), vc)
        cross = _dot(qc, state.astype(BF16)) * q_decay
        kd = (kc * k_decay).astype(BF16)
        state_ref[hd] = chunk_decay * state + lax.dot_general(kd, vc, (((0,), (0,)), ((), ())),
                                                              preferred_element_type=F32)
        o = inner + cross
        mu = jnp.mean(o, axis=-1, keepdims=True)
        oc = o - mu
        var = jnp.mean(oc * oc, axis=-1, keepdims=True)
        y = oc * lax.rsqrt(var + EPS) * _silu(gate)
        y_ref[rows, lo:lo + RET_DIM] = y.astype(y_ref.dtype)
    return run


def _block_diag_pair(slab):
    lane = lax.broadcasted_iota(jnp.int32, slab.shape, 1)
    first = lane < SSM_HEAD_DIM
    top = jnp.where(first, slab, 0.0)
    bot = jnp.where(first, 0.0, slab)
    return jnp.concatenate([top, bot], axis=0).astype(BF16)


def _ssd_conv(xbc_ref, xact_ref, convw_ref, convb_ref, tile, fillers):
    nblk = SSM_XBC // FILL_BLOCK
    for n in range(nblk):
        cols = slice(n * FILL_BLOCK, (n + 1) * FILL_BLOCK)
        conv = convb_ref[:, cols]
        for kk in range(SSM_CONV):
            off = SUBLANES - (SSM_CONV - 1) + kk
            conv = conv + convw_ref[kk:kk + 1, cols] * xbc_ref[off:off + tile, cols]
        xact_ref[:, cols] = _silu(conv)
        _run_fillers(fillers, nblk - n)
    xbc_ref[0:SUBLANES, :] = xbc_ref[tile:tile + SUBLANES, :]


T_ACS, T_WT, T_EACS = 0, 1, 2


def _ssd_terms_task(c, dt_ref, dtb_ref, alog_ref, term_ref, trans_ref):
    def run():
        rows = slice(c * CHUNK, (c + 1) * CHUNK)
        row = lax.broadcasted_iota(jnp.int32, (CHUNK, CHUNK), 0)
        col = lax.broadcasted_iota(jnp.int32, (CHUNK, CHUNK), 1)
        tri_f = (row >= col).astype(F32)
        dt_in = dt_ref[rows, :] + dtb_ref[...]
        dt_c = jnp.maximum(dt_in, 0.0) + jnp.log1p(jnp.exp(-jnp.abs(dt_in)))
        a_cs = jnp.dot(tri_f, dt_c * (-jnp.exp(alog_ref[...])), preferred_element_type=F32,
                       precision=lax.Precision.HIGHEST)
        a_last = a_cs[CHUNK - 1:CHUNK, :]
        term_ref[c, T_ACS] = a_cs
        term_ref[c, T_WT] = (jnp.exp(a_last - a_cs) * dt_c).T
        term_ref[c, T_EACS] = jnp.exp(a_cs)
        trans_ref[c, 0] = a_cs.T
        trans_ref[c, 1] = dt_c.T
    return run


def _ssd_head_task(c, heads, term_ref, trans_ref, ld_ref, eb_ref, cd_ref):
    def run():
        row = lax.broadcasted_iota(jnp.int32, (CHUNK, CHUNK), 0)
        col = lax.broadcasted_iota(jnp.int32, (CHUNK, CHUNK), 1)
        tril = row >= col
        lane1 = lax.broadcasted_iota(jnp.int32, (1, LANES), 1)
        a_cs = term_ref[c, T_ACS]
        e_acs = term_ref[c, T_EACS]
        cdec = e_acs[CHUNK - 1:CHUNK, :]
        for hh in heads:
            seg = _lane_bcast(a_cs, hh) - trans_ref[c, 0, hh:hh + 1, :]
            ld_ref[c * SSM_HEADS + hh] = jnp.exp(jnp.where(tril, seg, NEG_INF)) * trans_ref[c, 1, hh:hh + 1, :]
            eb_ref[c * SSM_HEADS + hh] = _lane_bcast(e_acs, hh)
        for h1 in heads[::2]:
            cd_ref[c, h1 // 2:h1 // 2 + 1, :] = jnp.where(
                lane1 < SSM_HEAD_DIM, _lane_bcast(cdec, h1), _lane_bcast(cdec, h1 + 1))
    return run


def _ssd_scan(xact_ref, z_ref, dskip_ref, ng_ref, term_ref, ld_ref, eb_ref, cd_ref,
              state_ref, ychunk_ref, y_ref, tile, fillers):
    pair = 2 * SSM_HEAD_DIM
    heads_per_group = SSM_HEADS // SSM_GROUPS
    n_chunks = tile // CHUNK
    slots = n_chunks * SSM_HEADS // 2

    for c in range(n_chunks):
        rows = slice(c * CHUNK, (c + 1) * CHUNK)
        for g in range(SSM_GROUPS):
            b_lo = SSM_WIDTH + g * SSM_STATE
            c_lo = SSM_WIDTH + SSM_GROUPS * SSM_STATE + g * SSM_STATE
            b_g = xact_ref[rows, b_lo:b_lo + SSM_STATE]
            c_g = xact_ref[rows, c_lo:c_lo + SSM_STATE]
            b_t = b_g.T
            cb = _dot(c_g.astype(BF16), b_t.astype(BF16))
            for p in range(heads_per_group // 2):
                h1 = g * heads_per_group + 2 * p
                lo = h1 * SSM_HEAD_DIM
                xs_pair = xact_ref[rows, lo:lo + pair]
                rhs = _block_diag_pair(xs_pair)
                m_parts, s_parts, o_parts = [], [], []
                for hh in (h1, h1 + 1):
                    m_parts.append(cb * ld_ref[c * SSM_HEADS + hh])
                    s_parts.append(b_t * term_ref[c, T_WT, hh:hh + 1, :])
                    o_parts.append(c_g * eb_ref[c * SSM_HEADS + hh])
                y_diag = _dot(jnp.concatenate(m_parts, axis=1).astype(BF16), rhs)
                s_new = _dot(jnp.concatenate(s_parts, axis=1).astype(BF16), rhs)
                s_prev = state_ref[:, lo:lo + pair]
                y_off = _dot(jnp.concatenate(o_parts, axis=1).astype(BF16), _block_diag_pair(s_prev))
                state_ref[:, lo:lo + pair] = cd_ref[c, h1 // 2:h1 // 2 + 1, :] * s_prev + s_new
                ychunk_ref[:, lo:lo + pair] = y_diag + y_off + dskip_ref[:, lo:lo + pair] * xs_pair
                _run_fillers(fillers, slots)
                slots -= 1
        y = ychunk_ref[...] * _silu(z_ref[rows, :])
        ms = jnp.mean(y * y, axis=-1, keepdims=True)
        y_ref[rows, :] = (y * lax.rsqrt(ms + EPS) * ng_ref[...]).astype(y_ref.dtype)


def _short_conv_task(cp_ref, scw_ref, u_ref, y_ref, tile):
    def run():
        width = u_ref.shape[1]
        u_ref[SUBLANES:SUBLANES + tile, :] = cp_ref[:, width:2 * width] * cp_ref[:, 2 * width:3 * width]
        conv = jnp.zeros((tile, width), F32)
        for kk in range(SC_CONV):
            off = SUBLANES - (SC_CONV - 1) + kk
            conv = conv + scw_ref[kk:kk + 1, :] * u_ref[off:off + tile, :]
        u_ref[0:SUBLANES, :] = u_ref[tile:tile + SUBLANES, :]
        y_ref[...] = (cp_ref[:, :width] * conv * _silu(cp_ref[:, 3 * width:])).astype(y_ref.dtype)
    return run


def _layer_kernel(x_ref, g_ref, scale_ref, shift_ref, rgate_ref, cos_ref, sin_ref,
                  wr_ref, ws_ref, wc_ref, wg_ref,
                  convw_ref, convb_ref, dtb_ref, alog_ref, dskip_ref, sng_ref, scw_ref,
                  wbr_ret_ref, wbr_ssm_ref, wbr_sc_ref, wout_ref, fg_ref,
                  o_ref,
                  rstate_ref, xbc_ref, sstate_ref, ychunk_ref, u_ref, yret_ref, yssm_ref, ysc_ref,
                  qkvg_ref, z_ref, dt_ref, xact_ref, cp_ref, gates_ref, acc_ref,
                  term_ref, trans_ref, ld_ref, eb_ref, cd_ref,
                  *, tile, final):
    @pl.when(pl.program_id(1) == 0)
    def _():
        rstate_ref[...] = jnp.zeros_like(rstate_ref)
        sstate_ref[...] = jnp.zeros_like(sstate_ref)
        xbc_ref[0:SUBLANES, :] = jnp.zeros((SUBLANES, xbc_ref.shape[1]), F32)
        u_ref[0:SUBLANES, :] = jnp.zeros((SUBLANES, u_ref.shape[1]), F32)

    x = x_ref[...]
    d = x.shape[1]
    h = _modulated_norm(x, g_ref[...], scale_ref[...], shift_ref[...])
    blk = FILL_BLOCK

    xbc_ref[SUBLANES:SUBLANES + tile, :] = _dot(h, ws_ref[:, SSM_WIDTH:SSM_WIDTH + SSM_XBC])
    n_chunks = tile // CHUNK
    fillers = [_proj_block(dt_ref, h, ws_ref, 0, SSM_WIDTH + SSM_XBC, DT_PAD)]
    fillers += [_ssd_terms_task(c, dt_ref, dtb_ref, alog_ref, term_ref, trans_ref) for c in range(n_chunks)]
    ret_proj = [_proj_block(qkvg_ref, h, wr_ref, n * blk, n * blk, blk) for n in range(4 * RET_WIDTH // blk)]
    quad = 4
    head_tasks = [_ssd_head_task(c, list(range(q, q + quad)), term_ref, trans_ref, ld_ref, eb_ref, cd_ref)
                  for c in range(n_chunks) for q in range(0, SSM_HEADS, quad)]
    fillers += [t for duo in zip(ret_proj, head_tasks) for t in duo] + head_tasks[len(ret_proj):]
    _ssd_conv(xbc_ref, xact_ref, convw_ref, convb_ref, tile, fillers)

    def ret_merge():
        acc_ref[...] = gates_ref[:, :d] * _dot(yret_ref[...], wbr_ret_ref[...])

    def sc_merge():
        acc_ref[...] += gates_ref[:, 2 * d:] * _dot(ysc_ref[...], wbr_sc_ref[...])

    dense = [_proj_block(z_ref, h, ws_ref, n * blk, n * blk, blk) for n in range(SSM_WIDTH // blk)]
    dense += [_proj_block(cp_ref, h, wc_ref, n * blk, n * blk, blk) for n in range(cp_ref.shape[1] // blk)]
    dense += [_proj_block(gates_ref, h, wg_ref, n * blk, n * blk, blk, _sigmoid) for n in range(3 * d // blk)]
    ret_units = [_retention_unit(hd, c, qkvg_ref, cos_ref, sin_ref, rstate_ref, yret_ref)
                 for c in range(tile // CHUNK) for hd in range(RET_HEADS)]
    fillers = [t for duo in zip(dense, ret_units) for t in duo] + dense[len(ret_units):]
    fillers += [ret_merge, _short_conv_task(cp_ref, scw_ref, u_ref, ysc_ref, tile), sc_merge]
    _ssd_scan(xact_ref, z_ref, dskip_ref, sng_ref, term_ref, ld_ref, eb_ref, cd_ref,
              sstate_ref, ychunk_ref, yssm_ref, tile, fillers)
    _run_fillers(fillers, 1)

    merged = acc_ref[...] + gates_ref[:, d:2 * d] * _dot(yssm_ref[...], wbr_ssm_ref[...])
    out = _dot(merged.astype(BF16), wout_ref[...])
    xn = x + rgate_ref[...] * out
    if final:
        ms = jnp.mean(xn * xn, axis=-1, keepdims=True)
        xn = xn * lax.rsqrt(ms + EPS) * fg_ref[...]
    o_ref[...] = xn


def _layer_call(x, ada5, norm_g3, cos2, sin2, weights, smalls, layer, tile, final):
    bsz, seq, d = x.shape
    row_spec = lambda idx: pl.BlockSpec((None, None, None, 1, d), lambda b, j: (layer, b, idx, 0, 0))
    const = lambda a: pl.BlockSpec(a.shape, lambda b, j: (0,) * a.ndim, pipeline_mode=pl.Buffered(1))
    tok = lambda w: pl.BlockSpec((None, tile, w), lambda b, j: (b, j, 0))
    w_r, w_s, w_c, w_g, wbr_ret, wbr_ssm, wbr_sc, w_out = weights
    conv_w, conv_b, dt_bias, a_log, d_skip, ssm_ng, sc_w, final_g = smalls
    width = sc_w.shape[1]
    return pl.pallas_call(
        functools.partial(_layer_kernel, tile=tile, final=final),
        grid=(bsz, seq // tile),
        in_specs=[
            tok(d),
            pl.BlockSpec((None, 1, d), lambda b, j: (layer, 0, 0)),
            row_spec(1),
            row_spec(0),
            row_spec(2),
            tok(RET_DIM), tok(RET_DIM),
            const(w_r), const(w_s), const(w_c), const(w_g),
            const(conv_w), const(conv_b), const(dt_bias), const(a_log), const(d_skip), const(ssm_ng),
            const(sc_w),
            const(wbr_ret), const(wbr_ssm), const(wbr_sc), const(w_out), const(final_g),
        ],
        out_specs=tok(d),
        out_shape=jax.ShapeDtypeStruct((bsz, seq, d), F32),
        scratch_shapes=[
            pltpu.VMEM((RET_HEADS, RET_DIM, RET_DIM), F32),
            pltpu.VMEM((tile + SUBLANES, SSM_XBC), F32),
            pltpu.VMEM((SSM_STATE, SSM_WIDTH), F32),
            pltpu.VMEM((CHUNK, SSM_WIDTH), F32),
            pltpu.VMEM((tile + SUBLANES, width), F32),
            pltpu.VMEM((tile, RET_WIDTH), BF16),
            pltpu.VMEM((tile, SSM_WIDTH), BF16),
            pltpu.VMEM((tile, width), BF16),
            pltpu.VMEM((tile, 4 * RET_WIDTH), F32),
            pltpu.VMEM((tile, SSM_WIDTH), F32),
            pltpu.VMEM((tile, DT_PAD), F32),
            pltpu.VMEM((tile, SSM_XBC), F32),
            pltpu.VMEM((tile, 4 * width), F32),
            pltpu.VMEM((tile, 3 * d), F32),
            pltpu.VMEM((tile, d), F32),
            pltpu.VMEM((tile // CHUNK, 3, CHUNK, DT_PAD), F32),
            pltpu.VMEM((tile // CHUNK, 2, DT_PAD, CHUNK), F32),
            pltpu.VMEM((tile // CHUNK * SSM_HEADS, CHUNK, CHUNK), F32),
            pltpu.VMEM((tile // CHUNK * SSM_HEADS, CHUNK, LANES), F32),
            pltpu.VMEM((tile // CHUNK, SSM_HEADS // 2, LANES), F32),
        ],
        compiler_params=pltpu.CompilerParams(
            dimension_semantics=("parallel", "arbitrary"), vmem_limit_bytes=VMEM_LIMIT_BYTES),
        name="layer",
    )(x, norm_g3, ada5, ada5, ada5, cos2, sin2, w_r, w_s, w_c, w_g,
      conv_w, conv_b, dt_bias, a_log, d_skip, ssm_ng, sc_w, wbr_ret, wbr_ssm, wbr_sc, w_out, final_g)


def kernel(x, c, positions, norm_g, w_ada, b_ada, w_in, ssm_conv_w, ssm_conv_b, ssm_dt_bias, ssm_a_log,
           ssm_d, ssm_norm_g, sc_conv_w, w_br_ret, w_br_ssm, w_br_sc, w_out, final_norm_g):
    bsz, seq, d = x.shape
    depth = w_in.shape[0]
    sc_width = sc_conv_w.shape[2]
    tile = 256
    assert seq % tile == 0 and tile % CHUNK == 0

    sizes = (RET_WIDTH, RET_WIDTH, RET_WIDTH, RET_WIDTH, SSM_WIDTH, SSM_XBC, SSM_HEADS,
             sc_width, sc_width, sc_width, sc_width, 3 * d)
    offs = [0]
    for s in sizes:
        offs.append(offs[-1] + s)
    assert offs[-1] == w_in.shape[2]

    ada = _ada_call(c, w_ada, b_ada)
    ada5 = ada.reshape(depth, bsz, 3, 1, d)
    cos2, sin2 = _rope_call(positions)
    norm_g3 = norm_g.reshape(depth, 1, d)
    final_g = final_norm_g.reshape(1, d)
    pad_heads = lambda v: jnp.pad(v, (0, DT_PAD - SSM_HEADS)).reshape(1, DT_PAD)

    for layer in range(depth):
        w = w_in[layer]
        weights = (
            w[:, offs[0]:offs[4]].astype(BF16),
            jnp.concatenate(
                [w[:, offs[4]:offs[7]], jnp.zeros((d, DT_PAD - SSM_HEADS), w.dtype)], axis=1).astype(BF16),
            w[:, offs[7]:offs[11]].astype(BF16),
            w[:, offs[11]:offs[12]].astype(BF16),
            w_br_ret[layer].astype(BF16), w_br_ssm[layer].astype(BF16), w_br_sc[layer].astype(BF16),
            w_out[layer].astype(BF16),
        )
        smalls = (
            ssm_conv_w[layer], ssm_conv_b[layer].reshape(1, SSM_XBC),
            pad_heads(ssm_dt_bias[layer]), pad_heads(ssm_a_log[layer]),
            jnp.repeat(ssm_d[layer], SSM_HEAD_DIM).reshape(1, SSM_WIDTH),
            ssm_norm_g[layer].reshape(1, SSM_WIDTH), sc_conv_w[layer], final_g,
        )
        x = _layer_call(x, ada5, norm_g3, cos2, sin2, weights, smalls, layer, tile,
                        final=(layer == depth - 1))
    return x
```

```python
import functools
import math

import jax
import jax.numpy as jnp
from jax import lax
from jax.experimental import pallas as pl
from jax.experimental.pallas import tpu as pltpu

F32 = jnp.float32
BF16 = jnp.bfloat16

EPS = 1e-6
RET_HEADS = 4
RET_DIM = 128
RET_WIDTH = RET_HEADS * RET_DIM
ROPE_THETA = 10000.0
SSM_HEADS = 16
SSM_HEAD_DIM = 64
SSM_GROUPS = 2
SSM_STATE = 128
SSM_CONV = 4
SSM_WIDTH = SSM_HEADS * SSM_HEAD_DIM
SSM_XBC = SSM_WIDTH + 2 * SSM_GROUPS * SSM_STATE
SC_CONV = 3
CHUNK = 128
LANES = 128
SUBLANES = 8
DT_PAD = LANES
FILL_BLOCK = 512
VMEM_LIMIT_BYTES = 60 * 1024 * 1024

NEG_INF = float("-inf")


def _sigmoid(x):
    return 1.0 / (1.0 + jnp.exp(-x))


def _silu(x):
    return x * _sigmoid(x)


def _dot(a, b):
    return jnp.dot(a, b, preferred_element_type=F32)


def _modulated_norm(x, g, scale, shift):
    ms = jnp.mean(x * x, axis=-1, keepdims=True)
    y = x * lax.rsqrt(ms + EPS) * g
    return (y * (1.0 + scale) + shift).astype(BF16)


def _lane_bcast(a, col):
    return jnp.broadcast_to(a[:, col:col + 1], a.shape)


def _ada_kernel(c_ref, w_ref, b_ref, o_ref):
    act = _silu(c_ref[...])
    hi = act.astype(BF16)
    lo = (act - hi.astype(F32)).astype(BF16)
    w = w_ref[...].astype(BF16)
    o_ref[...] = _dot(hi, w) + _dot(lo, w) + b_ref[...]


def _ada_call(c, w_ada, b_ada):
    depth, d, d3 = w_ada.shape
    bsz = c.shape[0]
    nblk = d3 // d
    return pl.pallas_call(
        _ada_kernel,
        grid=(depth, nblk),
        in_specs=[
            pl.BlockSpec((bsz, d), lambda l, n: (0, 0)),
            pl.BlockSpec((None, d, d), lambda l, n: (l, 0, n)),
            pl.BlockSpec((None, 1, d), lambda l, n: (l, 0, n)),
        ],
        out_specs=pl.BlockSpec((None, bsz, d), lambda l, n: (l, 0, n)),
        out_shape=jax.ShapeDtypeStruct((depth, bsz, d3), F32),
        compiler_params=pltpu.CompilerParams(
            dimension_semantics=("parallel", "parallel"), vmem_limit_bytes=VMEM_LIMIT_BYTES),
        name="ada",
    )(c, w_ada, b_ada.reshape(depth, 1, d3))


def _rope_kernel(pos_ref, inv_ref, cos_ref, sin_ref):
    ang = pos_ref[...].astype(F32) * inv_ref[...]
    first = lax.broadcasted_iota(jnp.int32, ang.shape, 1) < RET_DIM // 2
    cs = jnp.cos(jnp.where(first, ang, ang - 0.5 * math.pi))
    sc = pltpu.roll(cs, RET_DIM // 2, axis=1)
    cos_ref[...] = jnp.where(first, cs, sc)
    sin_ref[...] = jnp.where(first, -sc, cs)


def _rope_call(positions):
    bsz, seq = positions.shape
    half = RET_DIM // 2
    inv_freq = ROPE_THETA ** (-jnp.arange(0, RET_DIM, 2, dtype=F32) / RET_DIM)
    inv2 = jnp.concatenate([inv_freq, inv_freq]).reshape(1, 2 * half)
    tbl = jax.ShapeDtypeStruct((bsz, seq, RET_DIM), F32)
    return pl.pallas_call(
        _rope_kernel,
        grid=(bsz,),
        in_specs=[
            pl.BlockSpec((None, seq, 1), lambda b: (b, 0, 0)),
            pl.BlockSpec((1, RET_DIM), lambda b: (0, 0)),
        ],
        out_specs=[pl.BlockSpec((None, seq, RET_DIM), lambda b: (b, 0, 0))] * 2,
        out_shape=[tbl, tbl],
        compiler_params=pltpu.CompilerParams(
            dimension_semantics=("parallel",), vmem_limit_bytes=VMEM_LIMIT_BYTES),
        name="rope_tables",
    )(positions.reshape(bsz, seq, 1), inv2)


def _run_fillers(fillers, slots_left):
    n = -(-len(fillers) // max(slots_left, 1))
    for _ in range(min(n, len(fillers))):
        fillers.pop(0)()


def _proj_block(dst_ref, h, w_ref, dst_lo, w_lo, width, act=None):
    def run():
        r = _dot(h, w_ref[:, w_lo:w_lo + width])
        dst_ref[:, dst_lo:dst_lo + width] = r if act is None else act(r)
    return run


def _retention_unit(hd, c, qkvg_ref, cos_ref, sin_ref, state_ref, y_ref):
    def run():
        row = lax.broadcasted_iota(jnp.int32, (CHUNK, CHUNK), 0)
        col = lax.broadcasted_iota(jnp.int32, (CHUNK, CHUNK), 1)
        rowf = row.astype(F32)
        half = RET_DIM // 2
        log_g = math.log(1.0 - 2.0 ** (-5.0 - hd))
        decay_mask = jnp.exp(jnp.where(row >= col, (row - col).astype(F32) * log_g, NEG_INF))
        q_decay = jnp.exp((rowf + 1.0) * log_g)
        k_decay = jnp.exp((CHUNK - 1.0 - rowf) * log_g)
        chunk_decay = math.exp(CHUNK * log_g)
        rows = slice(c * CHUNK, (c + 1) * CHUNK)
        lo = hd * RET_DIM
        cos = cos_ref[rows, :]
        sin = sin_ref[rows, :]

        def rot(t):
            return t * cos + pltpu.roll(t, half, axis=1) * sin

        qc = (rot(qkvg_ref[rows, lo:lo + RET_DIM]) * (RET_DIM ** -0.5)).astype(BF16)
        kc = rot(qkvg_ref[rows, RET_WIDTH + lo:RET_WIDTH + lo + RET_DIM])
        vc = qkvg_ref[rows, 2 * RET_WIDTH + lo:2 * RET_WIDTH + lo + RET_DIM].astype(BF16)
        gate = qkvg_ref[rows, 3 * RET_WIDTH + lo:3 * RET_WIDTH + lo + RET_DIM]
        state = state_ref[hd]
        scores = lax.dot_general(qc, kc.astype(BF16), (((1,), (1,)), ((), ())), preferred_element_type=F32)
        inner = _dot((scores * decay_mask).astype(BF16), vc)
        cross = _dot(qc, state.astype(BF16)) * q_decay
        kd = (kc * k_decay).astype(BF16)
        state_ref[hd] = chunk_decay * state + lax.dot_general(kd, vc, (((0,), (0,)), ((), ())),
                                                              preferred_element_type=F32)
        o = inner + cross
        mu = jnp.mean(o, axis=-1, keepdims=True)
        oc = o - mu
        var = jnp.mean(oc * oc, axis=-1, keepdims=True)
        y = oc * lax.rsqrt(var + EPS) * _silu(gate)
        y_ref[rows, lo:lo + RET_DIM] = y.astype(y_ref.dtype)
    return run


def _block_diag_pair(slab):
    lane = lax.broadcasted_iota(jnp.int32, slab.shape, 1)
    first = lane < SSM_HEAD_DIM
    top = jnp.where(first, slab, 0.0)
    bot = jnp.where(first, 0.0, slab)
    return jnp.concatenate([top, bot], axis=0).astype(BF16)


def _ssd_conv(xbc_ref, xact_ref, convw_ref, convb_ref, tile, fillers):
    nblk = SSM_XBC // FILL_BLOCK
    for n in range(nblk):
        cols = slice(n * FILL_BLOCK, (n + 1) * FILL_BLOCK)
        conv = convb_ref[:, cols]
        for kk in range(SSM_CONV):
            off = SUBLANES - (SSM_CONV - 1) + kk
            conv = conv + convw_ref[kk:kk + 1, cols] * xbc_ref[off:off + tile, cols]
        xact_ref[:, cols] = _silu(conv)
        _run_fillers(fillers, nblk - n)
    xbc_ref[0:SUBLANES, :] = xbc_ref[tile:tile + SUBLANES, :]


T_ACS, T_WT, T_EACS = 0, 1, 2


def _ssd_terms_task(c, dt_ref, dtb_ref, alog_ref, term_ref, trans_ref):
    def run():
        rows = slice(c * CHUNK, (c + 1) * CHUNK)
        row = lax.broadcasted_iota(jnp.int32, (CHUNK, CHUNK), 0)
        col = lax.broadcasted_iota(jnp.int32, (CHUNK, CHUNK), 1)
        tri_f = (row >= col).astype(F32)
        dt_in = dt_ref[rows, :] + dtb_ref[...]
        dt_c = jnp.maximum(dt_in, 0.0) + jnp.log1p(jnp.exp(-jnp.abs(dt_in)))
        a_cs = jnp.dot(tri_f, dt_c * (-jnp.exp(alog_ref[...])), preferred_element_type=F32,
                       precision=lax.Precision.HIGHEST)
        a_last = a_cs[CHUNK - 1:CHUNK, :]
        term_ref[c, T_ACS] = a_cs
        term_ref[c, T_WT] = (jnp.exp(a_last - a_cs) * dt_c).T
        term_ref[c, T_EACS] = jnp.exp(a_cs)
        trans_ref[c, 0] = a_cs.T
        trans_ref[c, 1] = dt_c.T
    return run


def _ssd_head_task(c, heads, term_ref, trans_ref, ld_ref, eb_ref, cd_ref):
    def run():
        row = lax.broadcasted_iota(jnp.int32, (CHUNK, CHUNK), 0)
        col = lax.broadcasted_iota(jnp.int32, (CHUNK, CHUNK), 1)
        tril = row >= col
        lane1 = lax.broadcasted_iota(jnp.int32, (1, LANES), 1)
        a_cs = term_ref[c, T_ACS]
        e_acs = term_ref[c, T_EACS]
        cdec = e_acs[CHUNK - 1:CHUNK, :]
        for hh in heads:
            seg = _lane_bcast(a_cs, hh) - trans_ref[c, 0, hh:hh + 1, :]
            ld_ref[c * SSM_HEADS + hh] = jnp.exp(jnp.where(tril, seg, NEG_INF)) * trans_ref[c, 1, hh:hh + 1, :]
            eb_ref[c * SSM_HEADS + hh] = _lane_bcast(e_acs, hh)
        for h1 in heads[::2]:
            cd_ref[c, h1 // 2:h1 // 2 + 1, :] = jnp.where(
                lane1 < SSM_HEAD_DIM, _lane_bcast(cdec, h1), _lane_bcast(cdec, h1 + 1))
    return run


def _ssd_scan(xact_ref, z_ref, dskip_ref, ng_ref, term_ref, ld_ref, eb_ref, cd_ref,
              state_ref, ychunk_ref, y_ref, tile, fillers):
    pair = 2 * SSM_HEAD_DIM
    heads_per_group = SSM_HEADS // SSM_GROUPS
    n_chunks = tile // CHUNK
    slots = n_chunks * SSM_HEADS // 2

    for c in range(n_chunks):
        rows = slice(c * CHUNK, (c + 1) * CHUNK)
        for g in range(SSM_GROUPS):
            b_lo = SSM_WIDTH + g * SSM_STATE
            c_lo = SSM_WIDTH + SSM_GROUPS * SSM_STATE + g * SSM_STATE
            b_g = xact_ref[rows, b_lo:b_lo + SSM_STATE]
            c_g = xact_ref[rows, c_lo:c_lo + SSM_STATE]
            b_t = b_g.T
            cb = _dot(c_g.astype(BF16), b_t.astype(BF16))
            for p in range(heads_per_group // 2):
                h1 = g * heads_per_group + 2 * p
                lo = h1 * SSM_HEAD_DIM
                xs_pair = xact_ref[rows, lo:lo + pair]
                rhs = _block_diag_pair(xs_pair)
                m_parts, s_parts, o_parts = [], [], []
                for hh in (h1, h1 + 1):
                    m_parts.append(cb * ld_ref[c * SSM_HEADS + hh])
                    s_parts.append(b_t * term_ref[c, T_WT, hh:hh + 1, :])
                    o_parts.append(c_g * eb_ref[c * SSM_HEADS + hh])
                y_diag = _dot(jnp.concatenate(m_parts, axis=1).astype(BF16), rhs)
                s_new = _dot(jnp.concatenate(s_parts, axis=1).astype(BF16), rhs)
                s_prev = state_ref[:, lo:lo + pair]
                y_off = _dot(jnp.concatenate(o_parts, axis=1).astype(BF16), _block_diag_pair(s_prev))
                state_ref[:, lo:lo + pair] = cd_ref[c, h1 // 2:h1 // 2 + 1, :] * s_prev + s_new
                ychunk_ref[:, lo:lo + pair] = y_diag + y_off + dskip_ref[:, lo:lo + pair] * xs_pair
                _run_fillers(fillers, slots)
                slots -= 1
        y = ychunk_ref[...] * _silu(z_ref[rows, :])
        ms = jnp.mean(y * y, axis=-1, keepdims=True)
        y_ref[rows, :] = (y * lax.rsqrt(ms + EPS) * ng_ref[...]).astype(y_ref.dtype)


def _short_conv_task(cp_ref, scw_ref, u_ref, y_ref, tile):
    def run():
        width = u_ref.shape[1]
        u_ref[SUBLANES:SUBLANES + tile, :] = cp_ref[:, width:2 * width] * cp_ref[:, 2 * width:3 * width]
        conv = jnp.zeros((tile, width), F32)
        for kk in range(SC_CONV):
            off = SUBLANES - (SC_CONV - 1) + kk
            conv = conv + scw_ref[kk:kk + 1, :] * u_ref[off:off + tile, :]
        u_ref[0:SUBLANES, :] = u_ref[tile:tile + SUBLANES, :]
        y_ref[...] = (cp_ref[:, :width] * conv * _silu(cp_ref[:, 3 * width:])).astype(y_ref.dtype)
    return run


def _layer_kernel(x_ref, g_ref, scale_ref, shift_ref, rgate_ref, cos_ref, sin_ref,
                  wr_ref, ws_ref, wc_ref, wg_ref,
                  convw_ref, convb_ref, dtb_ref, alog_ref, dskip_ref, sng_ref, scw_ref,
                  wbr_ret_ref, wbr_ssm_ref, wbr_sc_ref, wout_ref, fg_ref,
                  o_ref,
                  rstate_ref, xbc_ref, sstate_ref, ychunk_ref, u_ref, yret_ref, yssm_ref, ysc_ref,
                  qkvg_ref, z_ref, dt_ref, xact_ref, cp_ref, gates_ref, acc_ref,
                  term_ref, trans_ref, ld_ref, eb_ref, cd_ref,
                  *, tile, final):
    @pl.when(pl.program_id(1) == 0)
    def _():
        rstate_ref[...] = jnp.zeros_like(rstate_ref)
        sstate_ref[...] = jnp.zeros_like(sstate_ref)
        xbc_ref[0:SUBLANES, :] = jnp.zeros((SUBLANES, xbc_ref.shape[1]), F32)
        u_ref[0:SUBLANES, :] = jnp.zeros((SUBLANES, u_ref.shape[1]), F32)

    x = x_ref[...]
    d = x.shape[1]
    h = _modulated_norm(x, g_ref[...], scale_ref[...], shift_ref[...])
    blk = FILL_BLOCK

    xbc_ref[SUBLANES:SUBLANES + tile, :] = _dot(h, ws_ref[:, SSM_WIDTH:SSM_WIDTH + SSM_XBC])
    n_chunks = tile // CHUNK
    fillers = [_proj_block(dt_ref, h, ws_ref, 0, SSM_WIDTH + SSM_XBC, DT_PAD)]
    fillers += [_ssd_terms_task(c, dt_ref, dtb_ref, alog_ref, term_ref, trans_ref) for c in range(n_chunks)]
    ret_proj = [_proj_block(qkvg_ref, h, wr_ref, n * blk, n * blk, blk) for n in range(4 * RET_WIDTH // blk)]
    quad = 4
    head_tasks = [_ssd_head_task(c, list(range(q, q + quad)), term_ref, trans_ref, ld_ref, eb_ref, cd_ref)
                  for c in range(n_chunks) for q in range(0, SSM_HEADS, quad)]
    fillers += [t for duo in zip(ret_proj, head_tasks) for t in duo] + head_tasks[len(ret_proj):]
    _ssd_conv(xbc_ref, xact_ref, convw_ref, convb_ref, tile, fillers)

    def ret_merge():
        acc_ref[...] = gates_ref[:, :d] * _dot(yret_ref[...], wbr_ret_ref[...])

    def sc_merge():
        acc_ref[...] += gates_ref[:, 2 * d:] * _dot(ysc_ref[...], wbr_sc_ref[...])

    dense = [_proj_block(z_ref, h, ws_ref, n * blk, n * blk, blk) for n in range(SSM_WIDTH // blk)]
    dense += [_proj_block(cp_ref, h, wc_ref, n * blk, n * blk, blk) for n in range(cp_ref.shape[1] // blk)]
    dense += [_proj_block(gates_ref, h, wg_ref, n * blk, n * blk, blk, _sigmoid) for n in range(3 * d // blk)]
    ret_units = [_retention_unit(hd, c, qkvg_ref, cos_ref, sin_ref, rstate_ref, yret_ref)
                 for c in range(tile // CHUNK) for hd in range(RET_HEADS)]
    fillers = [t for duo in zip(dense, ret_units) for t in duo] + dense[len(ret_units):]
    fillers += [ret_merge, _short_conv_task(cp_ref, scw_ref, u_ref, ysc_ref, tile), sc_merge]
    _ssd_scan(xact_ref, z_ref, dskip_ref, sng_ref, term_ref, ld_ref, eb_ref, cd_ref,
              sstate_ref, ychunk_ref, yssm_ref, tile, fillers)
    _run_fillers(fillers, 1)

    merged = acc_ref[...] + gates_ref[:, d:2 * d] * _dot(yssm_ref[...], wbr_ssm_ref[...])
    out = _dot(merged.astype(BF16), wout_ref[...])
    xn = x + rgate_ref[...] * out
    if final:
        ms = jnp.mean(xn * xn, axis=-1, keepdims=True)
        xn = xn * lax.rsqrt(ms + EPS) * fg_ref[...]
    o_ref[...] = xn


def _layer_call(x, ada5, norm_g3, cos2, sin2, weights, smalls, layer, tile, final):
    bsz, seq, d = x.shape
    row_spec = lambda idx: pl.BlockSpec((None, None, None, 1, d), lambda b, j: (layer, b, idx, 0, 0))
    per_layer = lambda a: pl.BlockSpec((None,) + a.shape[1:], lambda b, j: (layer,) + (0,) * (a.ndim - 1),
                                       pipeline_mode=pl.Buffered(1))
    tok = lambda w: pl.BlockSpec((None, tile, w), lambda b, j: (b, j, 0))
    w_r, w_s, w_c, w_g, wbr_ret, wbr_ssm, wbr_sc, w_out = weights
    conv_w, conv_b, dt_bias, a_log, d_skip, ssm_ng, sc_w, final_g = smalls
    width = sc_w.shape[2]
    return pl.pallas_call(
        functools.partial(_layer_kernel, tile=tile, final=final),
        grid=(bsz, seq // tile),
        in_specs=[
            tok(d),
            per_layer(norm_g3),
            row_spec(1),
            row_spec(0),
            row_spec(2),
            tok(RET_DIM), tok(RET_DIM),
            per_layer(w_r), per_layer(w_s), per_layer(w_c), per_layer(w_g),
            per_layer(conv_w), per_layer(conv_b), per_layer(dt_bias), per_layer(a_log), per_layer(d_skip),
            per_layer(ssm_ng), per_layer(sc_w),
            per_layer(wbr_ret), per_layer(wbr_ssm), per_layer(wbr_sc), per_layer(w_out),
            pl.BlockSpec(final_g.shape, lambda b, j: (0, 0), pipeline_mode=pl.Buffered(1)),
        ],
        out_specs=tok(d),
        out_shape=jax.ShapeDtypeStruct((bsz, seq, d), F32),
        scratch_shapes=[
            pltpu.VMEM((RET_HEADS, RET_DIM, RET_DIM), F32),
            pltpu.VMEM((tile + SUBLANES, SSM_XBC), F32),
            pltpu.VMEM((SSM_STATE, SSM_WIDTH), F32),
            pltpu.VMEM((CHUNK, SSM_WIDTH), F32),
            pltpu.VMEM((tile + SUBLANES, width), F32),
            pltpu.VMEM((tile, RET_WIDTH), BF16),
            pltpu.VMEM((tile, SSM_WIDTH), BF16),
            pltpu.VMEM((tile, width), BF16),
            pltpu.VMEM((tile, 4 * RET_WIDTH), F32),
            pltpu.VMEM((tile, SSM_WIDTH), F32),
            pltpu.VMEM((tile, DT_PAD), F32),
            pltpu.VMEM((tile, SSM_XBC), F32),
            pltpu.VMEM((tile, 4 * width), F32),
            pltpu.VMEM((tile, 3 * d), F32),
            pltpu.VMEM((tile, d), F32),
            pltpu.VMEM((tile // CHUNK, 3, CHUNK, DT_PAD), F32),
            pltpu.VMEM((tile // CHUNK, 2, DT_PAD, CHUNK), F32),
            pltpu.VMEM((tile // CHUNK * SSM_HEADS, CHUNK, CHUNK), F32),
            pltpu.VMEM((tile // CHUNK * SSM_HEADS, CHUNK, LANES), F32),
            pltpu.VMEM((tile // CHUNK, SSM_HEADS // 2, LANES), F32),
        ],
        compiler_params=pltpu.CompilerParams(
            dimension_semantics=("parallel", "arbitrary"), vmem_limit_bytes=VMEM_LIMIT_BYTES),
        name="layer",
    )(x, norm_g3, ada5, ada5, ada5, cos2, sin2, w_r, w_s, w_c, w_g,
      conv_w, conv_b, dt_bias, a_log, d_skip, ssm_ng, sc_w, wbr_ret, wbr_ssm, wbr_sc, w_out, final_g)


def kernel(x, c, positions, norm_g, w_ada, b_ada, w_in, ssm_conv_w, ssm_conv_b, ssm_dt_bias, ssm_a_log,
           ssm_d, ssm_norm_g, sc_conv_w, w_br_ret, w_br_ssm, w_br_sc, w_out, final_norm_g):
    bsz, seq, d = x.shape
    depth = w_in.shape[0]
    sc_width = sc_conv_w.shape[2]
    tile = 256
    assert seq % tile == 0 and tile % CHUNK == 0

    sizes = (RET_WIDTH, RET_WIDTH, RET_WIDTH, RET_WIDTH, SSM_WIDTH, SSM_XBC, SSM_HEADS,
             sc_width, sc_width, sc_width, sc_width, 3 * d)
    offs = [0]
    for s in sizes:
        offs.append(offs[-1] + s)
    assert offs[-1] == w_in.shape[2]

    ada = _ada_call(c, w_ada, b_ada)
    ada5 = ada.reshape(depth, bsz, 3, 1, d)
    cos2, sin2 = _rope_call(positions)
    norm_g3 = norm_g.reshape(depth, 1, d)

    dt_pad = DT_PAD - SSM_HEADS
    weights = (
        w_in[:, :, offs[0]:offs[4]].astype(BF16),
        jnp.pad(w_in[:, :, offs[4]:offs[7]], ((0, 0), (0, 0), (0, dt_pad))).astype(BF16),
        w_in[:, :, offs[7]:offs[11]].astype(BF16),
        w_in[:, :, offs[11]:offs[12]].astype(BF16),
        w_br_ret.astype(BF16), w_br_ssm.astype(BF16), w_br_sc.astype(BF16), w_out.astype(BF16),
    )
    pad_heads = lambda v: jnp.pad(v, ((0, 0), (0, dt_pad))).reshape(depth, 1, DT_PAD)
    smalls = (
        ssm_conv_w, ssm_conv_b.reshape(depth, 1, SSM_XBC),
        pad_heads(ssm_dt_bias), pad_heads(ssm_a_log),
        jnp.repeat(ssm_d, SSM_HEAD_DIM, axis=1).reshape(depth, 1, SSM_WIDTH),
        ssm_norm_g.reshape(depth, 1, SSM_WIDTH), sc_conv_w, final_norm_g.reshape(1, d),
    )
    for layer in range(depth):
        x = _layer_call(x, ada5, norm_g3, cos2, sin2, weights, smalls, layer, tile,
                        final=(layer == depth - 1))
    return x
```

```python
import functools
import math

import jax
import jax.numpy as jnp
from jax import lax
from jax.experimental import pallas as pl
from jax.experimental.pallas import tpu as pltpu

F32 = jnp.float32
BF16 = jnp.bfloat16

EPS = 1e-6
RET_HEADS = 4
RET_DIM = 128
RET_WIDTH = RET_HEADS * RET_DIM
ROPE_THETA = 10000.0
SSM_HEADS = 16
SSM_HEAD_DIM = 64
SSM_GROUPS = 2
SSM_STATE = 128
SSM_CONV = 4
SSM_WIDTH = SSM_HEADS * SSM_HEAD_DIM
SSM_XBC = SSM_WIDTH + 2 * SSM_GROUPS * SSM_STATE
SC_CONV = 3
CHUNK = 128
LANES = 128
SUBLANES = 8
DT_PAD = LANES
FILL_BLOCK = 512
D_MODEL = 1024
SC_WIDTH = D_MODEL // 2
RET_BLOCK = 0
Z_BLOCK = RET_BLOCK + 4 * RET_WIDTH // FILL_BLOCK
XBC_BLOCK = Z_BLOCK + SSM_WIDTH // FILL_BLOCK
SC_BLOCK = XBC_BLOCK + SSM_XBC // FILL_BLOCK
GATE_BLOCK = SC_BLOCK + 4 * SC_WIDTH // FILL_BLOCK
W_IN_BLOCKS = GATE_BLOCK + 3 * D_MODEL // FILL_BLOCK
VMEM_LIMIT_BYTES = 60 * 1024 * 1024

NEG_INF = float("-inf")


def _sigmoid(x):
    return 1.0 / (1.0 + jnp.exp(-x))


def _silu(x):
    return x * _sigmoid(x)


def _dot(a, b):
    return jnp.dot(a, b, preferred_element_type=F32)


def _modulated_norm(x, g, scale, shift):
    ms = jnp.mean(x * x, axis=-1, keepdims=True)
    y = x * lax.rsqrt(ms + EPS) * g
    return (y * (1.0 + scale) + shift).astype(BF16)


def _lane_bcast(a, col):
    return jnp.broadcast_to(a[:, col:col + 1], a.shape)


def _ada_kernel(c_ref, w_ref, b_ref, o_ref):
    act = _silu(c_ref[...])
    hi = act.astype(BF16)
    lo = (act - hi.astype(F32)).astype(BF16)
    w = w_ref[...].astype(BF16)
    o_ref[...] = _dot(hi, w) + _dot(lo, w) + b_ref[...]


def _ada_call(c, w_ada, b_ada):
    depth, d, d3 = w_ada.shape
    bsz = c.shape[0]
    nblk = d3 // d
    return pl.pallas_call(
        _ada_kernel,
        grid=(depth, nblk),
        in_specs=[
            pl.BlockSpec((bsz, d), lambda l, n: (0, 0)),
            pl.BlockSpec((None, d, d), lambda l, n: (l, 0, n)),
            pl.BlockSpec((None, 1, d), lambda l, n: (l, 0, n)),
        ],
        out_specs=pl.BlockSpec((None, bsz, d), lambda l, n: (l, 0, n)),
        out_shape=jax.ShapeDtypeStruct((depth, bsz, d3), F32),
        compiler_params=pltpu.CompilerParams(
            dimension_semantics=("parallel", "parallel"), vmem_limit_bytes=VMEM_LIMIT_BYTES),
        name="ada",
    )(c, w_ada, b_ada.reshape(depth, 1, d3))


def _rope_kernel(pos_ref, inv_ref, cos_ref, sin_ref):
    ang = pos_ref[...].astype(F32) * inv_ref[...]
    lane = lax.broadcasted_iota(jnp.int32, ang.shape, 1)
    cos_ref[...] = jnp.cos(ang)
    sin_ref[...] = jnp.where(lane < RET_DIM // 2, -jnp.sin(ang), jnp.sin(ang))


def _rope_call(positions):
    bsz, seq = positions.shape
    half = RET_DIM // 2
    inv_freq = ROPE_THETA ** (-jnp.arange(0, RET_DIM, 2, dtype=F32) / RET_DIM)
    inv2 = jnp.concatenate([inv_freq, inv_freq]).reshape(1, 2 * half)
    tbl = jax.ShapeDtypeStruct((bsz, seq, RET_DIM), F32)
    return pl.pallas_call(
        _rope_kernel,
        grid=(bsz,),
        in_specs=[
            pl.BlockSpec((None, seq, 1), lambda b: (b, 0, 0)),
            pl.BlockSpec((1, RET_DIM), lambda b: (0, 0)),
        ],
        out_specs=[pl.BlockSpec((None, seq, RET_DIM), lambda b: (b, 0, 0))] * 2,
        out_shape=[tbl, tbl],
        compiler_params=pltpu.CompilerParams(
            dimension_semantics=("parallel",), vmem_limit_bytes=VMEM_LIMIT_BYTES),
        name="rope_tables",
    )(positions.reshape(bsz, seq, 1), inv2)


def _run_fillers(fillers, slots_left):
    n = -(-len(fillers) // max(slots_left, 1))
    for _ in range(min(n, len(fillers))):
        fillers.pop(0)()


def _proj_block(dst_ref, h, w_ref, dst_lo, w_lo, width, act=None):
    def run():
        r = _dot(h, w_ref[:, w_lo:w_lo + width])
        dst_ref[:, dst_lo:dst_lo + width] = r if act is None else act(r)
    return run


def _retention_unit(hd, c, qkvg_ref, cos_ref, sin_ref, state_ref, y_ref):
    def run():
        row = lax.broadcasted_iota(jnp.int32, (CHUNK, CHUNK), 0)
        col = lax.broadcasted_iota(jnp.int32, (CHUNK, CHUNK), 1)
        rowf = row.astype(F32)
        half = RET_DIM // 2
        log_g = math.log(1.0 - 2.0 ** (-5.0 - hd))
        decay_mask = jnp.exp(jnp.where(row >= col, (row - col).astype(F32) * log_g, NEG_INF))
        q_decay = jnp.exp((rowf + 1.0) * log_g)
        k_decay = jnp.exp((CHUNK - 1.0 - rowf) * log_g)
        chunk_decay = math.exp(CHUNK * log_g)
        rows = slice(c * CHUNK, (c + 1) * CHUNK)
        lo = hd * RET_DIM
        cos = cos_ref[rows, :]
        sin = sin_ref[rows, :]

        def rot(t):
            return t * cos + pltpu.roll(t, half, axis=1) * sin

        qc = (rot(qkvg_ref[rows, lo:lo + RET_DIM]) * (RET_DIM ** -0.5)).astype(BF16)
        kc = rot(qkvg_ref[rows, RET_WIDTH + lo:RET_WIDTH + lo + RET_DIM])
        vc = qkvg_ref[rows, 2 * RET_WIDTH + lo:2 * RET_WIDTH + lo + RET_DIM].astype(BF16)
        gate = qkvg_ref[rows, 3 * RET_WIDTH + lo:3 * RET_WIDTH + lo + RET_DIM]
        state = state_ref[hd]
        scores = lax.dot_general(qc, kc.astype(BF16), (((1,), (1,)), ((), ())), preferred_element_type=F32)
        inner = _dot((scores * decay_mask).astype(BF16), vc)
        cross = _dot(qc, state.astype(BF16)) * q_decay
        kd = (kc * k_decay).astype(BF16)
        state_ref[hd] = chunk_decay * state + lax.dot_general(kd, vc, (((0,), (0,)), ((), ())),
                                                              preferred_element_type=F32)
        o = inner + cross
        mu = jnp.mean(o, axis=-1, keepdims=True)
        oc = o - mu
        var = jnp.mean(oc * oc, axis=-1, keepdims=True)
        y = oc * lax.rsqrt(var + EPS) * _silu(gate)
        y_ref[rows, lo:lo + RET_DIM] = y.astype(y_ref.dtype)
    return run


def _block_diag_pair(slab):
    lane = lax.broadcasted_iota(jnp.int32, slab.shape, 1)
    first = lane < SSM_HEAD_DIM
    top = jnp.where(first, slab, 0.0)
    bot = jnp.where(first, 0.0, slab)
    return jnp.concatenate([top, bot], axis=0).astype(BF16)


def _ssd_conv(xbc_ref, xact_ref, convw_ref, convb_ref, tile, fillers):
    nblk = SSM_XBC // FILL_BLOCK
    for n in range(nblk):
        cols = slice(n * FILL_BLOCK, (n + 1) * FILL_BLOCK)
        conv = convb_ref[:, cols]
        for kk in range(SSM_CONV):
            off = SUBLANES - (SSM_CONV - 1) + kk
            conv = conv + convw_ref[kk:kk + 1, cols] * xbc_ref[off:off + tile, cols]
        xact_ref[:, cols] = _silu(conv)
        _run_fillers(fillers, nblk - n)
    xbc_ref[0:SUBLANES, :] = xbc_ref[tile:tile + SUBLANES, :]


T_ACS, T_WT, T_EACS = 0, 1, 2


def _ssd_terms_task(c, dt_ref, dtb_ref, alog_ref, term_ref, trans_ref):
    def run():
        rows = slice(c * CHUNK, (c + 1) * CHUNK)
        row = lax.broadcasted_iota(jnp.int32, (CHUNK, CHUNK), 0)
        col = lax.broadcasted_iota(jnp.int32, (CHUNK, CHUNK), 1)
        tri_f = (row >= col).astype(F32)
        dt_in = dt_ref[rows, :] + dtb_ref[...]
        dt_c = jnp.maximum(dt_in, 0.0) + jnp.log1p(jnp.exp(-jnp.abs(dt_in)))
        a_cs = jnp.dot(tri_f, dt_c * (-jnp.exp(alog_ref[...])), preferred_element_type=F32,
                       precision=lax.Precision.HIGHEST)
        a_last = a_cs[CHUNK - 1:CHUNK, :]
        term_ref[c, T_ACS] = a_cs
        term_ref[c, T_WT] = (jnp.exp(a_last - a_cs) * dt_c).T
        term_ref[c, T_EACS] = jnp.exp(a_cs)
        trans_ref[c, 0] = a_cs.T
        trans_ref[c, 1] = dt_c.T
    return run


def _ssd_head_task(c, heads, term_ref, trans_ref, ld_ref, eb_ref, cd_ref):
    def run():
        row = lax.broadcasted_iota(jnp.int32, (CHUNK, CHUNK), 0)
        col = lax.broadcasted_iota(jnp.int32, (CHUNK, CHUNK), 1)
        tril = row >= col
        lane1 = lax.broadcasted_iota(jnp.int32, (1, LANES), 1)
        a_cs = term_ref[c, T_ACS]
        e_acs = term_ref[c, T_EACS]
        cdec = e_acs[CHUNK - 1:CHUNK, :]
        for hh in heads:
            seg = _lane_bcast(a_cs, hh) - trans_ref[c, 0, hh:hh + 1, :]
            ld_ref[c * SSM_HEADS + hh] = jnp.exp(jnp.where(tril, seg, NEG_INF)) * trans_ref[c, 1, hh:hh + 1, :]
            eb_ref[c * SSM_HEADS + hh] = _lane_bcast(e_acs, hh)
        for h1 in heads[::2]:
            cd_ref[c, h1 // 2:h1 // 2 + 1, :] = jnp.where(
                lane1 < SSM_HEAD_DIM, _lane_bcast(cdec, h1), _lane_bcast(cdec, h1 + 1))
    return run


def _ssd_scan(xact_ref, z_ref, dskip_ref, ng_ref, term_ref, ld_ref, eb_ref, cd_ref,
              state_ref, ychunk_ref, y_ref, tile, fillers):
    pair = 2 * SSM_HEAD_DIM
    heads_per_group = SSM_HEADS // SSM_GROUPS
    n_chunks = tile // CHUNK
    slots = n_chunks * SSM_HEADS // 2

    for c in range(n_chunks):
        rows = slice(c * CHUNK, (c + 1) * CHUNK)
        for g in range(SSM_GROUPS):
            b_lo = SSM_WIDTH + g * SSM_STATE
            c_lo = SSM_WIDTH + SSM_GROUPS * SSM_STATE + g * SSM_STATE
            b_g = xact_ref[rows, b_lo:b_lo + SSM_STATE]
            c_g = xact_ref[rows, c_lo:c_lo + SSM_STATE]
            b_t = b_g.T
            cb = _dot(c_g.astype(BF16), b_t.astype(BF16))
            for p in range(heads_per_group // 2):
                h1 = g * heads_per_group + 2 * p
                lo = h1 * SSM_HEAD_DIM
                xs_pair = xact_ref[rows, lo:lo + pair]
                rhs = _block_diag_pair(xs_pair)
                m_parts, s_parts, o_parts = [], [], []
                for hh in (h1, h1 + 1):
                    m_parts.append(cb * ld_ref[c * SSM_HEADS + hh])
                    s_parts.append(b_t * term_ref[c, T_WT, hh:hh + 1, :])
                    o_parts.append(c_g * eb_ref[c * SSM_HEADS + hh])
                y_diag = _dot(jnp.concatenate(m_parts, axis=1).astype(BF16), rhs)
                s_new = _dot(jnp.concatenate(s_parts, axis=1).astype(BF16), rhs)
                s_prev = state_ref[:, lo:lo + pair]
                y_off = _dot(jnp.concatenate(o_parts, axis=1).astype(BF16), _block_diag_pair(s_prev))
                state_ref[:, lo:lo + pair] = cd_ref[c, h1 // 2:h1 // 2 + 1, :] * s_prev + s_new
                ychunk_ref[:, lo:lo + pair] = y_diag + y_off + dskip_ref[:, lo:lo + pair] * xs_pair
                _run_fillers(fillers, slots)
                slots -= 1
        y = ychunk_ref[...] * _silu(z_ref[rows, :])
        ms = jnp.mean(y * y, axis=-1, keepdims=True)
        y_ref[rows, :] = (y * lax.rsqrt(ms + EPS) * ng_ref[...]).astype(y_ref.dtype)


def _short_conv_task(cp_ref, scw_ref, u_ref, y_ref, tile):
    def run():
        width = u_ref.shape[1]
        u_ref[SUBLANES:SUBLANES + tile, :] = cp_ref[:, width:2 * width] * cp_ref[:, 2 * width:3 * width]
        conv = jnp.zeros((tile, width), F32)
        for kk in range(SC_CONV):
            off = SUBLANES - (SC_CONV - 1) + kk
            conv = conv + scw_ref[kk:kk + 1, :] * u_ref[off:off + tile, :]
        u_ref[0:SUBLANES, :] = u_ref[tile:tile + SUBLANES, :]
        y_ref[...] = (cp_ref[:, :width] * conv * _silu(cp_ref[:, 3 * width:])).astype(y_ref.dtype)
    return run


def _layer_kernel(x_ref, g_ref, scale_ref, shift_ref, rgate_ref, cos_ref, sin_ref, *refs, tile, final):
    win = refs[:W_IN_BLOCKS]
    (wdt_ref, convw_ref, convb_ref, dtb_ref, alog_ref, dskip_ref, sng_ref, scw_ref,
     wbr_ret_ref, wbr_ssm_ref, wbr_sc_ref, wout_ref, fg_ref,
     o_ref,
     rstate_ref, xbc_ref, sstate_ref, ychunk_ref, u_ref, yret_ref, yssm_ref, ysc_ref,
     qkvg_ref, z_ref, dt_ref, xact_ref, cp_ref, gates_ref, acc_ref,
     term_ref, trans_ref, ld_ref, eb_ref, cd_ref) = refs[W_IN_BLOCKS:]

    @pl.when(pl.program_id(1) == 0)
    def _():
        rstate_ref[...] = jnp.zeros_like(rstate_ref)
        sstate_ref[...] = jnp.zeros_like(sstate_ref)
        xbc_ref[0:SUBLANES, :] = jnp.zeros((SUBLANES, xbc_ref.shape[1]), F32)
        u_ref[0:SUBLANES, :] = jnp.zeros((SUBLANES, u_ref.shape[1]), F32)

    x = x_ref[...]
    d = x.shape[1]
    h = _modulated_norm(x, g_ref[...], scale_ref[...], shift_ref[...])
    blk = FILL_BLOCK

    def proj(dst_ref, first_block, n, act=None):
        return _proj_block(dst_ref, h, win[first_block + n], n * blk, 0, blk, act)

    for n in range(SSM_XBC // blk):
        xbc_ref[SUBLANES:SUBLANES + tile, n * blk:(n + 1) * blk] = _dot(h, win[XBC_BLOCK + n][...])
    n_chunks = tile // CHUNK
    fillers = [_proj_block(dt_ref, h, wdt_ref, 0, 0, DT_PAD)]
    fillers += [_ssd_terms_task(c, dt_ref, dtb_ref, alog_ref, term_ref, trans_ref) for c in range(n_chunks)]
    ret_proj = [proj(qkvg_ref, RET_BLOCK, n) for n in range(4 * RET_WIDTH // blk)]
    quad = 4
    head_tasks = [_ssd_head_task(c, list(range(q, q + quad)), term_ref, trans_ref, ld_ref, eb_ref, cd_ref)
                  for c in range(n_chunks) for q in range(0, SSM_HEADS, quad)]
    fillers += [t for duo in zip(ret_proj, head_tasks) for t in duo] + head_tasks[len(ret_proj):]
    _ssd_conv(xbc_ref, xact_ref, convw_ref, convb_ref, tile, fillers)

    def ret_merge():
        acc_ref[...] = gates_ref[:, :d] * _dot(yret_ref[...], wbr_ret_ref[...])

    def sc_merge():
        acc_ref[...] += gates_ref[:, 2 * d:] * _dot(ysc_ref[...], wbr_sc_ref[...])

    dense = [proj(z_ref, Z_BLOCK, n) for n in range(SSM_WIDTH // blk)]
    dense += [proj(cp_ref, SC_BLOCK, n) for n in range(cp_ref.shape[1] // blk)]
    dense += [proj(gates_ref, GATE_BLOCK, n, _sigmoid) for n in range(3 * d // blk)]
    ret_units = [_retention_unit(hd, c, qkvg_ref, cos_ref, sin_ref, rstate_ref, yret_ref)
                 for c in range(tile // CHUNK) for hd in range(RET_HEADS)]
    fillers = [t for duo in zip(dense, ret_units) for t in duo] + dense[len(ret_units):]
    fillers += [ret_merge, _short_conv_task(cp_ref, scw_ref, u_ref, ysc_ref, tile), sc_merge]
    _ssd_scan(xact_ref, z_ref, dskip_ref, sng_ref, term_ref, ld_ref, eb_ref, cd_ref,
              sstate_ref, ychunk_ref, yssm_ref, tile, fillers)
    _run_fillers(fillers, 1)

    merged = acc_ref[...] + gates_ref[:, d:2 * d] * _dot(yssm_ref[...], wbr_ssm_ref[...])
    out = _dot(merged.astype(BF16), wout_ref[...])
    xn = x + rgate_ref[...] * out
    if final:
        ms = jnp.mean(xn * xn, axis=-1, keepdims=True)
        xn = xn * lax.rsqrt(ms + EPS) * fg_ref[...]
    o_ref[...] = xn


def _layer_call(x, ada5, norm_g3, cos2, sin2, weights, smalls, layer, tile, final):
    bsz, seq, d = x.shape
    row_spec = lambda idx: pl.BlockSpec((None, None, None, 1, d), lambda b, j: (layer, b, idx, 0, 0))
    per_layer = lambda a: pl.BlockSpec((None,) + a.shape[1:], lambda b, j: (layer,) + (0,) * (a.ndim - 1),
                                       pipeline_mode=pl.Buffered(1))
    tok = lambda w: pl.BlockSpec((None, tile, w), lambda b, j: (b, j, 0))
    w_all, wbr_ret, wbr_ssm, wbr_sc, w_out = weights
    conv_w, conv_b, dt_bias, a_log, d_skip, ssm_ng, sc_w, final_g = smalls
    width = sc_w.shape[2]
    w_in_block = lambda cols, n: pl.BlockSpec((None, d, cols), lambda b, j: (layer, 0, n),
                                              pipeline_mode=pl.Buffered(1))
    dt_block = W_IN_BLOCKS * FILL_BLOCK // DT_PAD
    return pl.pallas_call(
        functools.partial(_layer_kernel, tile=tile, final=final),
        grid=(bsz, seq // tile),
        in_specs=[
            tok(d),
            per_layer(norm_g3),
            row_spec(1),
            row_spec(0),
            row_spec(2),
            tok(RET_DIM), tok(RET_DIM),
        ] + [w_in_block(FILL_BLOCK, n) for n in range(W_IN_BLOCKS)] + [
            w_in_block(DT_PAD, dt_block),
            per_layer(conv_w), per_layer(conv_b), per_layer(dt_bias), per_layer(a_log), per_layer(d_skip),
            per_layer(ssm_ng), per_layer(sc_w),
            per_layer(wbr_ret), per_layer(wbr_ssm), per_layer(wbr_sc), per_layer(w_out),
            pl.BlockSpec(final_g.shape, lambda b, j: (0, 0), pipeline_mode=pl.Buffered(1)),
        ],
        out_specs=tok(d),
        out_shape=jax.ShapeDtypeStruct((bsz, seq, d), F32),
        scratch_shapes=[
            pltpu.VMEM((RET_HEADS, RET_DIM, RET_DIM), F32),
            pltpu.VMEM((tile + SUBLANES, SSM_XBC), F32),
            pltpu.VMEM((SSM_STATE, SSM_WIDTH), F32),
            pltpu.VMEM((CHUNK, SSM_WIDTH), F32),
            pltpu.VMEM((tile + SUBLANES, width), F32),
            pltpu.VMEM((tile, RET_WIDTH), BF16),
            pltpu.VMEM((tile, SSM_WIDTH), BF16),
            pltpu.VMEM((tile, width), BF16),
            pltpu.VMEM((tile, 4 * RET_WIDTH), F32),
            pltpu.VMEM((tile, SSM_WIDTH), F32),
            pltpu.VMEM((tile, DT_PAD), F32),
            pltpu.VMEM((tile, SSM_XBC), F32),
            pltpu.VMEM((tile, 4 * width), F32),
            pltpu.VMEM((tile, 3 * d), F32),
            pltpu.VMEM((tile, d), F32),
            pltpu.VMEM((tile // CHUNK, 3, CHUNK, DT_PAD), F32),
            pltpu.VMEM((tile // CHUNK, 2, DT_PAD, CHUNK), F32),
            pltpu.VMEM((tile // CHUNK * SSM_HEADS, CHUNK, CHUNK), F32),
            pltpu.VMEM((tile // CHUNK * SSM_HEADS, CHUNK, LANES), F32),
            pltpu.VMEM((tile // CHUNK, SSM_HEADS // 2, LANES), F32),
        ],
        compiler_params=pltpu.CompilerParams(
            dimension_semantics=("parallel", "arbitrary"), vmem_limit_bytes=VMEM_LIMIT_BYTES),
        name="layer",
    )(x, norm_g3, ada5, ada5, ada5, cos2, sin2, *([w_all] * (W_IN_BLOCKS + 1)),
      conv_w, conv_b, dt_bias, a_log, d_skip, ssm_ng, sc_w, wbr_ret, wbr_ssm, wbr_sc, w_out, final_g)


def kernel(x, c, positions, norm_g, w_ada, b_ada, w_in, ssm_conv_w, ssm_conv_b, ssm_dt_bias, ssm_a_log,
           ssm_d, ssm_norm_g, sc_conv_w, w_br_ret, w_br_ssm, w_br_sc, w_out, final_norm_g):
    bsz, seq, d = x.shape
    depth = w_in.shape[0]
    sc_width = sc_conv_w.shape[2]
    tile = 256
    assert seq % tile == 0 and tile % CHUNK == 0

    sizes = (RET_WIDTH, RET_WIDTH, RET_WIDTH, RET_WIDTH, SSM_WIDTH, SSM_XBC, SSM_HEADS,
             sc_width, sc_width, sc_width, sc_width, 3 * d)
    offs = [0]
    for s in sizes:
        offs.append(offs[-1] + s)
    assert offs[-1] == w_in.shape[2] and d == D_MODEL and sc_width == SC_WIDTH

    ada = _ada_call(c, w_ada, b_ada)
    ada5 = ada.reshape(depth, bsz, 3, 1, d)
    cos2, sin2 = _rope_call(positions)
    norm_g3 = norm_g.reshape(depth, 1, d)

    dt_pad = DT_PAD - SSM_HEADS
    w_all = jnp.concatenate(
        [w_in[:, :, :offs[6]], w_in[:, :, offs[7]:], w_in[:, :, offs[6]:offs[7]],
         jnp.zeros((depth, d, dt_pad), w_in.dtype)], axis=2).astype(BF16)
    weights = (w_all, w_br_ret.astype(BF16), w_br_ssm.astype(BF16), w_br_sc.astype(BF16), w_out.astype(BF16))
    pad_heads = lambda v: jnp.pad(v, ((0, 0), (0, dt_pad))).reshape(depth, 1, DT_PAD)
    smalls = (
        ssm_conv_w, ssm_conv_b.reshape(depth, 1, SSM_XBC),
        pad_heads(ssm_dt_bias), pad_heads(ssm_a_log),
        jnp.repeat(ssm_d, SSM_HEAD_DIM, axis=1).reshape(depth, 1, SSM_WIDTH),
        ssm_norm_g.reshape(depth, 1, SSM_WIDTH), sc_conv_w, final_norm_g.reshape(1, d),
    )
    for layer in range(depth):
        x = _layer_call(x, ada5, norm_g3, cos2, sin2, weights, smalls, layer, tile,
                        final=(layer == depth - 1))
    return x
```

```python
import functools
import math

import jax
import jax.numpy as jnp
from jax import lax
from jax.experimental import pallas as pl
from jax.experimental.pallas import tpu as pltpu

F32 = jnp.float32
BF16 = jnp.bfloat16

EPS = 1e-6
RET_HEADS = 4
RET_DIM = 128
RET_WIDTH = RET_HEADS * RET_DIM
ROPE_THETA = 10000.0
SSM_HEADS = 16
SSM_HEAD_DIM = 64
SSM_GROUPS = 2
SSM_STATE = 128
SSM_CONV = 4
SSM_WIDTH = SSM_HEADS * SSM_HEAD_DIM
SSM_XBC = SSM_WIDTH + 2 * SSM_GROUPS * SSM_STATE
SC_CONV = 3
CHUNK = 128
LANES = 128
SUBLANES = 8
DT_PAD = LANES
FILL_BLOCK = 512
D_MODEL = 1024
SC_WIDTH = D_MODEL // 2
RET_BLOCK = 0
Z_BLOCK = RET_BLOCK + 4 * RET_WIDTH // FILL_BLOCK
XBC_BLOCK = Z_BLOCK + SSM_WIDTH // FILL_BLOCK
SC_BLOCK = XBC_BLOCK + SSM_XBC // FILL_BLOCK
GATE_BLOCK = SC_BLOCK + 4 * SC_WIDTH // FILL_BLOCK
W_IN_BLOCKS = GATE_BLOCK + 3 * D_MODEL // FILL_BLOCK
VMEM_LIMIT_BYTES = 60 * 1024 * 1024

NEG_INF = float("-inf")


def _sigmoid(x):
    return 1.0 / (1.0 + jnp.exp(-x))


def _silu(x):
    return x * _sigmoid(x)


def _dot(a, b):
    return jnp.dot(a, b, preferred_element_type=F32)


def _modulated_norm(x, g, scale, shift):
    ms = jnp.mean(x * x, axis=-1, keepdims=True)
    y = x * lax.rsqrt(ms + EPS) * g
    return (y * (1.0 + scale) + shift).astype(BF16)


def _lane_bcast(a, col):
    return jnp.broadcast_to(a[:, col:col + 1], a.shape)


def _ada_kernel(c_ref, w_ref, b_ref, o_ref):
    act = _silu(c_ref[...])
    hi = act.astype(BF16)
    lo = (act - hi.astype(F32)).astype(BF16)
    w = w_ref[...].astype(BF16)
    o_ref[...] = _dot(hi, w) + _dot(lo, w) + b_ref[...]


def _ada_call(c, w_ada, b_ada):
    depth, d, d3 = w_ada.shape
    bsz = c.shape[0]
    nblk = d3 // d
    return pl.pallas_call(
        _ada_kernel,
        grid=(depth, nblk),
        in_specs=[
            pl.BlockSpec((bsz, d), lambda l, n: (0, 0)),
            pl.BlockSpec((None, d, d), lambda l, n: (l, 0, n)),
            pl.BlockSpec((None, 1, d), lambda l, n: (l, 0, n)),
        ],
        out_specs=pl.BlockSpec((None, bsz, d), lambda l, n: (l, 0, n)),
        out_shape=jax.ShapeDtypeStruct((depth, bsz, d3), F32),
        compiler_params=pltpu.CompilerParams(
            dimension_semantics=("parallel", "parallel"), vmem_limit_bytes=VMEM_LIMIT_BYTES),
        name="ada",
    )(c, w_ada, b_ada.reshape(depth, 1, d3))


def _w_in_prep_kernel(a_ref, b_ref, o_ref):
    n = pl.program_id(1)

    @pl.when(n < SC_BLOCK)
    def _():
        o_ref[...] = a_ref[...].astype(o_ref.dtype)

    @pl.when((n >= SC_BLOCK) & (n < W_IN_BLOCKS))
    def _():
        wide = jnp.concatenate([a_ref[...], b_ref[...]], axis=1)
        o_ref[...] = wide[:, SSM_HEADS:SSM_HEADS + FILL_BLOCK].astype(o_ref.dtype)

    @pl.when(n == W_IN_BLOCKS)
    def _():
        lane = lax.broadcasted_iota(jnp.int32, b_ref.shape, 1)
        o_ref[...] = jnp.zeros(o_ref.shape, o_ref.dtype)
        o_ref[:, :DT_PAD] = jnp.where(lane < SSM_HEADS, b_ref[...], 0.0).astype(o_ref.dtype)


def _w_in_prep_call(w_in):
    depth, d, width = w_in.shape
    dt_lo = SC_BLOCK * FILL_BLOCK
    assert width == W_IN_BLOCKS * FILL_BLOCK + SSM_HEADS
    per_block = FILL_BLOCK // DT_PAD
    last_tail = (width - 1) // DT_PAD
    tail_idx = lambda n: jnp.where(n == W_IN_BLOCKS, dt_lo // DT_PAD,
                                   jnp.minimum((n + 1) * per_block, last_tail))
    return pl.pallas_call(
        _w_in_prep_kernel,
        grid=(depth, W_IN_BLOCKS + 1),
        in_specs=[
            pl.BlockSpec((None, d, FILL_BLOCK), lambda l, n: (l, 0, jnp.minimum(n, W_IN_BLOCKS - 1))),
            pl.BlockSpec((None, d, DT_PAD), lambda l, n: (l, 0, tail_idx(n))),
        ],
        out_specs=pl.BlockSpec((None, d, FILL_BLOCK), lambda l, n: (l, 0, n)),
        out_shape=jax.ShapeDtypeStruct((depth, d, (W_IN_BLOCKS + 1) * FILL_BLOCK), BF16),
        compiler_params=pltpu.CompilerParams(
            dimension_semantics=("parallel", "parallel"), vmem_limit_bytes=VMEM_LIMIT_BYTES),
        name="w_in_prep",
    )(w_in, w_in)


def _rope_kernel(pos_ref, inv_ref, cos_ref, sin_ref):
    ang = pos_ref[...].astype(F32) * inv_ref[...]
    lane = lax.broadcasted_iota(jnp.int32, ang.shape, 1)
    cos_ref[...] = jnp.cos(ang)
    sin_ref[...] = jnp.where(lane < RET_DIM // 2, -jnp.sin(ang), jnp.sin(ang))


def _rope_call(positions):
    bsz, seq = positions.shape
    half = RET_DIM // 2
    inv_freq = ROPE_THETA ** (-jnp.arange(0, RET_DIM, 2, dtype=F32) / RET_DIM)
    inv2 = jnp.concatenate([inv_freq, inv_freq]).reshape(1, 2 * half)
    tbl = jax.ShapeDtypeStruct((bsz, seq, RET_DIM), F32)
    return pl.pallas_call(
        _rope_kernel,
        grid=(bsz,),
        in_specs=[
            pl.BlockSpec((None, seq, 1), lambda b: (b, 0, 0)),
            pl.BlockSpec((1, RET_DIM), lambda b: (0, 0)),
        ],
        out_specs=[pl.BlockSpec((None, seq, RET_DIM), lambda b: (b, 0, 0))] * 2,
        out_shape=[tbl, tbl],
        compiler_params=pltpu.CompilerParams(
            dimension_semantics=("parallel",), vmem_limit_bytes=VMEM_LIMIT_BYTES),
        name="rope_tables",
    )(positions.reshape(bsz, seq, 1), inv2)


def _run_fillers(fillers, slots_left):
    n = -(-len(fillers) // max(slots_left, 1))
    for _ in range(min(n, len(fillers))):
        fillers.pop(0)()


def _proj_block(dst_ref, h, w_ref, dst_lo, w_lo, width, act=None):
    def run():
        r = _dot(h, w_ref[:, w_lo:w_lo + width])
        dst_ref[:, dst_lo:dst_lo + width] = r if act is None else act(r)
    return run


def _retention_unit(hd, c, qkvg_ref, cos_ref, sin_ref, state_ref, y_ref):
    def run():
        row = lax.broadcasted_iota(jnp.int32, (CHUNK, CHUNK), 0)
        col = lax.broadcasted_iota(jnp.int32, (CHUNK, CHUNK), 1)
        rowf = row.astype(F32)
        half = RET_DIM // 2
        log_g = math.log(1.0 - 2.0 ** (-5.0 - hd))
        decay_mask = jnp.exp(jnp.where(row >= col, (row - col).astype(F32) * log_g, NEG_INF))
        q_decay = jnp.exp((rowf + 1.0) * log_g)
        k_decay = jnp.exp((CHUNK - 1.0 - rowf) * log_g)
        chunk_decay = math.exp(CHUNK * log_g)
        rows = slice(c * CHUNK, (c + 1) * CHUNK)
        lo = hd * RET_DIM
        cos = cos_ref[rows, :]
        sin = sin_ref[rows, :]

        def rot(t):
            return t * cos + pltpu.roll(t, half, axis=1) * sin

        qc = (rot(qkvg_ref[rows, lo:lo + RET_DIM]) * (RET_DIM ** -0.5)).astype(BF16)
        kc = rot(qkvg_ref[rows, RET_WIDTH + lo:RET_WIDTH + lo + RET_DIM])
        vc = qkvg_ref[rows, 2 * RET_WIDTH + lo:2 * RET_WIDTH + lo + RET_DIM].astype(BF16)
        gate = qkvg_ref[rows, 3 * RET_WIDTH + lo:3 * RET_WIDTH + lo + RET_DIM]
        state = state_ref[hd]
        scores = lax.dot_general(qc, kc.astype(BF16), (((1,), (1,)), ((), ())), preferred_element_type=F32)
        inner = _dot((scores * decay_mask).astype(BF16), vc)
        cross = _dot(qc, state.astype(BF16)) * q_decay
        kd = (kc * k_decay).astype(BF16)
        state_ref[hd] = chunk_decay * state + lax.dot_general(kd, vc, (((0,), (0,)), ((), ())),
                                                              preferred_element_type=F32)
        o = inner + cross
        mu = jnp.mean(o, axis=-1, keepdims=True)
        oc = o - mu
        var = jnp.mean(oc * oc, axis=-1, keepdims=True)
        y = oc * lax.rsqrt(var + EPS) * _silu(gate)
        y_ref[rows, lo:lo + RET_DIM] = y.astype(y_ref.dtype)
    return run


def _block_diag_pair(slab):
    lane = lax.broadcasted_iota(jnp.int32, slab.shape, 1)
    first = lane < SSM_HEAD_DIM
    top = jnp.where(first, slab, 0.0)
    bot = jnp.where(first, 0.0, slab)
    return jnp.concatenate([top, bot], axis=0).astype(BF16)


def _ssd_conv(xbc_ref, xact_ref, convw_ref, convb_ref, tile, fillers):
    nblk = SSM_XBC // FILL_BLOCK
    for n in range(nblk):
        cols = slice(n * FILL_BLOCK, (n + 1) * FILL_BLOCK)
        conv = convb_ref[:, cols]
        for kk in range(SSM_CONV):
            off = SUBLANES - (SSM_CONV - 1) + kk
            conv = conv + convw_ref[kk:kk + 1, cols] * xbc_ref[off:off + tile, cols]
        xact_ref[:, cols] = _silu(conv)
        _run_fillers(fillers, nblk - n)
    xbc_ref[0:SUBLANES, :] = xbc_ref[tile:tile + SUBLANES, :]


T_ACS, T_WT, T_EACS = 0, 1, 2


def _ssd_terms_task(c, dt_ref, dtb_ref, alog_ref, term_ref, trans_ref):
    def run():
        rows = slice(c * CHUNK, (c + 1) * CHUNK)
        row = lax.broadcasted_iota(jnp.int32, (CHUNK, CHUNK), 0)
        col = lax.broadcasted_iota(jnp.int32, (CHUNK, CHUNK), 1)
        tri_f = (row >= col).astype(F32)
        dt_in = dt_ref[rows, :] + dtb_ref[...]
        dt_c = jnp.maximum(dt_in, 0.0) + jnp.log1p(jnp.exp(-jnp.abs(dt_in)))
        a_cs = jnp.dot(tri_f, dt_c * (-jnp.exp(alog_ref[...])), preferred_element_type=F32,
                       precision=lax.Precision.HIGHEST)
        a_last = a_cs[CHUNK - 1:CHUNK, :]
        term_ref[c, T_ACS] = a_cs
        term_ref[c, T_WT] = (jnp.exp(a_last - a_cs) * dt_c).T
        term_ref[c, T_EACS] = jnp.exp(a_cs)
        trans_ref[c, 0] = a_cs.T
        trans_ref[c, 1] = dt_c.T
    return run


def _ssd_head_task(c, heads, term_ref, trans_ref, ld_ref, eb_ref, cd_ref):
    def run():
        row = lax.broadcasted_iota(jnp.int32, (CHUNK, CHUNK), 0)
        col = lax.broadcasted_iota(jnp.int32, (CHUNK, CHUNK), 1)
        tril = row >= col
        lane1 = lax.broadcasted_iota(jnp.int32, (1, LANES), 1)
        a_cs = term_ref[c, T_ACS]
        e_acs = term_ref[c, T_EACS]
        cdec = e_acs[CHUNK - 1:CHUNK, :]
        for hh in heads:
            seg = _lane_bcast(a_cs, hh) - trans_ref[c, 0, hh:hh + 1, :]
            ld_ref[c * SSM_HEADS + hh] = jnp.exp(jnp.where(tril, seg, NEG_INF)) * trans_ref[c, 1, hh:hh + 1, :]
            eb_ref[c * SSM_HEADS + hh] = _lane_bcast(e_acs, hh)
        for h1 in heads[::2]:
            cd_ref[c, h1 // 2:h1 // 2 + 1, :] = jnp.where(
                lane1 < SSM_HEAD_DIM, _lane_bcast(cdec, h1), _lane_bcast(cdec, h1 + 1))
    return run


def _ssd_scan(xact_ref, z_ref, dskip_ref, ng_ref, term_ref, ld_ref, eb_ref, cd_ref,
              state_ref, ychunk_ref, y_ref, tile, fillers):
    pair = 2 * SSM_HEAD_DIM
    heads_per_group = SSM_HEADS // SSM_GROUPS
    n_chunks = tile // CHUNK
    slots = n_chunks * SSM_HEADS // 2

    for c in range(n_chunks):
        rows = slice(c * CHUNK, (c + 1) * CHUNK)
        for g in range(SSM_GROUPS):
            b_lo = SSM_WIDTH + g * SSM_STATE
            c_lo = SSM_WIDTH + SSM_GROUPS * SSM_STATE + g * SSM_STATE
            b_g = xact_ref[rows, b_lo:b_lo + SSM_STATE]
            c_g = xact_ref[rows, c_lo:c_lo + SSM_STATE]
            b_t = b_g.T
            cb = _dot(c_g.astype(BF16), b_t.astype(BF16))
            for p in range(heads_per_group // 2):
                h1 = g * heads_per_group + 2 * p
                lo = h1 * SSM_HEAD_DIM
                xs_pair = xact_ref[rows, lo:lo + pair]
                rhs = _block_diag_pair(xs_pair)
                m_parts, s_parts, o_parts = [], [], []
                for hh in (h1, h1 + 1):
                    m_parts.append(cb * ld_ref[c * SSM_HEADS + hh])
                    s_parts.append(b_t * term_ref[c, T_WT, hh:hh + 1, :])
                    o_parts.append(c_g * eb_ref[c * SSM_HEADS + hh])
                y_diag = _dot(jnp.concatenate(m_parts, axis=1).astype(BF16), rhs)
                s_new = _dot(jnp.concatenate(s_parts, axis=1).astype(BF16), rhs)
                s_prev = state_ref[:, lo:lo + pair]
                y_off = _dot(jnp.concatenate(o_parts, axis=1).astype(BF16), _block_diag_pair(s_prev))
                state_ref[:, lo:lo + pair] = cd_ref[c, h1 // 2:h1 // 2 + 1, :] * s_prev + s_new
                ychunk_ref[:, lo:lo + pair] = y_diag + y_off + dskip_ref[:, lo:lo + pair] * xs_pair
                _run_fillers(fillers, slots)
                slots -= 1
        y = ychunk_ref[...] * _silu(z_ref[rows, :])
        ms = jnp.mean(y * y, axis=-1, keepdims=True)
        y_ref[rows, :] = (y * lax.rsqrt(ms + EPS) * ng_ref[...]).astype(y_ref.dtype)


def _short_conv_task(cp_ref, scw_ref, u_ref, y_ref, tile):
    def run():
        width = u_ref.shape[1]
        u_ref[SUBLANES:SUBLANES + tile, :] = cp_ref[:, width:2 * width] * cp_ref[:, 2 * width:3 * width]
        conv = jnp.zeros((tile, width), F32)
        for kk in range(SC_CONV):
            off = SUBLANES - (SC_CONV - 1) + kk
            conv = conv + scw_ref[kk:kk + 1, :] * u_ref[off:off + tile, :]
        u_ref[0:SUBLANES, :] = u_ref[tile:tile + SUBLANES, :]
        y_ref[...] = (cp_ref[:, :width] * conv * _silu(cp_ref[:, 3 * width:])).astype(y_ref.dtype)
    return run


def _layer_kernel(x_ref, g_ref, scale_ref, shift_ref, rgate_ref, cos_ref, sin_ref, *refs, tile, final):
    win = refs[:W_IN_BLOCKS]
    (wdt_ref, convw_ref, convb_ref, dtb_ref, alog_ref, dskip_ref, sng_ref, scw_ref,
     wbr_ret_ref, wbr_ssm_ref, wbr_sc_ref, wout_ref, fg_ref,
     o_ref,
     rstate_ref, xbc_ref, sstate_ref, ychunk_ref, u_ref, yret_ref, yssm_ref, ysc_ref,
     qkvg_ref, z_ref, dt_ref, xact_ref, cp_ref, gates_ref, acc_ref,
     term_ref, trans_ref, ld_ref, eb_ref, cd_ref) = refs[W_IN_BLOCKS:]

    @pl.when(pl.program_id(1) == 0)
    def _():
        rstate_ref[...] = jnp.zeros_like(rstate_ref)
        sstate_ref[...] = jnp.zeros_like(sstate_ref)
        xbc_ref[0:SUBLANES, :] = jnp.zeros((SUBLANES, xbc_ref.shape[1]), F32)
        u_ref[0:SUBLANES, :] = jnp.zeros((SUBLANES, u_ref.shape[1]), F32)

    x = x_ref[...]
    d = x.shape[1]
    h = _modulated_norm(x, g_ref[...], scale_ref[...], shift_ref[...])
    blk = FILL_BLOCK

    def proj(dst_ref, first_block, n, act=None):
        return _proj_block(dst_ref, h, win[first_block + n], n * blk, 0, blk, act)

    for n in range(SSM_XBC // blk):
        xbc_ref[SUBLANES:SUBLANES + tile, n * blk:(n + 1) * blk] = _dot(h, win[XBC_BLOCK + n][...])
    n_chunks = tile // CHUNK
    fillers = [_proj_block(dt_ref, h, wdt_ref, 0, 0, DT_PAD)]
    fillers += [_ssd_terms_task(c, dt_ref, dtb_ref, alog_ref, term_ref, trans_ref) for c in range(n_chunks)]
    ret_proj = [proj(qkvg_ref, RET_BLOCK, n) for n in range(4 * RET_WIDTH // blk)]
    quad = 4
    head_tasks = [_ssd_head_task(c, list(range(q, q + quad)), term_ref, trans_ref, ld_ref, eb_ref, cd_ref)
                  for c in range(n_chunks) for q in range(0, SSM_HEADS, quad)]
    fillers += [t for duo in zip(ret_proj, head_tasks) for t in duo] + head_tasks[len(ret_proj):]
    _ssd_conv(xbc_ref, xact_ref, convw_ref, convb_ref, tile, fillers)

    def ret_merge():
        acc_ref[...] = gates_ref[:, :d] * _dot(yret_ref[...], wbr_ret_ref[...])

    def sc_merge():
        acc_ref[...] += gates_ref[:, 2 * d:] * _dot(ysc_ref[...], wbr_sc_ref[...])

    dense = [proj(z_ref, Z_BLOCK, n) for n in range(SSM_WIDTH // blk)]
    dense += [proj(cp_ref, SC_BLOCK, n) for n in range(cp_ref.shape[1] // blk)]
    dense += [proj(gates_ref, GATE_BLOCK, n, _sigmoid) for n in range(3 * d // blk)]
    ret_units = [_retention_unit(hd, c, qkvg_ref, cos_ref, sin_ref, rstate_ref, yret_ref)
                 for c in range(tile // CHUNK) for hd in range(RET_HEADS)]
    fillers = [t for duo in zip(dense, ret_units) for t in duo] + dense[len(ret_units):]
    fillers += [ret_merge, _short_conv_task(cp_ref, scw_ref, u_ref, ysc_ref, tile), sc_merge]
    _ssd_scan(xact_ref, z_ref, dskip_ref, sng_ref, term_ref, ld_ref, eb_ref, cd_ref,
              sstate_ref, ychunk_ref, yssm_ref, tile, fillers)
    _run_fillers(fillers, 1)

    merged = acc_ref[...] + gates_ref[:, d:2 * d] * _dot(yssm_ref[...], wbr_ssm_ref[...])
    out = _dot(merged.astype(BF16), wout_ref[...])
    xn = x + rgate_ref[...] * out
    if final:
        ms = jnp.mean(xn * xn, axis=-1, keepdims=True)
        xn = xn * lax.rsqrt(ms + EPS) * fg_ref[...]
    o_ref[...] = xn


def _layer_call(x, ada5, norm_g3, cos2, sin2, weights, smalls, layer, tile, final):
    bsz, seq, d = x.shape
    row_spec = lambda idx: pl.BlockSpec((None, None, None, 1, d), lambda b, j: (layer, b, idx, 0, 0))
    per_layer = lambda a: pl.BlockSpec((None,) + a.shape[1:], lambda b, j: (layer,) + (0,) * (a.ndim - 1),
                                       pipeline_mode=pl.Buffered(1))
    tok = lambda w: pl.BlockSpec((None, tile, w), lambda b, j: (b, j, 0))
    w_all, wbr_ret, wbr_ssm, wbr_sc, w_out = weights
    conv_w, conv_b, dt_bias, a_log, d_skip, ssm_ng, sc_w, final_g = smalls
    width = sc_w.shape[2]
    w_in_block = lambda cols, n: pl.BlockSpec((None, d, cols), lambda b, j: (layer, 0, n),
                                              pipeline_mode=pl.Buffered(1))
    dt_block = W_IN_BLOCKS * FILL_BLOCK // DT_PAD
    return pl.pallas_call(
        functools.partial(_layer_kernel, tile=tile, final=final),
        grid=(bsz, seq // tile),
        in_specs=[
            tok(d),
            per_layer(norm_g3),
            row_spec(1),
            row_spec(0),
            row_spec(2),
            tok(RET_DIM), tok(RET_DIM),
        ] + [w_in_block(FILL_BLOCK, n) for n in range(W_IN_BLOCKS)] + [
            w_in_block(DT_PAD, dt_block),
            per_layer(conv_w), per_layer(conv_b), per_layer(dt_bias), per_layer(a_log), per_layer(d_skip),
            per_layer(ssm_ng), per_layer(sc_w),
            per_layer(wbr_ret), per_layer(wbr_ssm), per_layer(wbr_sc), per_layer(w_out),
            pl.BlockSpec(final_g.shape, lambda b, j: (0, 0), pipeline_mode=pl.Buffered(1)),
        ],
        out_specs=tok(d),
        out_shape=jax.ShapeDtypeStruct((bsz, seq, d), F32),
        scratch_shapes=[
            pltpu.VMEM((RET_HEADS, RET_DIM, RET_DIM), F32),
            pltpu.VMEM((tile + SUBLANES, SSM_XBC), F32),
            pltpu.VMEM((SSM_STATE, SSM_WIDTH), F32),
            pltpu.VMEM((CHUNK, SSM_WIDTH), F32),
            pltpu.VMEM((tile + SUBLANES, width), F32),
            pltpu.VMEM((tile, RET_WIDTH), BF16),
            pltpu.VMEM((tile, SSM_WIDTH), BF16),
            pltpu.VMEM((tile, width), BF16),
            pltpu.VMEM((tile, 4 * RET_WIDTH), F32),
            pltpu.VMEM((tile, SSM_WIDTH), F32),
            pltpu.VMEM((tile, DT_PAD), F32),
            pltpu.VMEM((tile, SSM_XBC), F32),
            pltpu.VMEM((tile, 4 * width), F32),
            pltpu.VMEM((tile, 3 * d), F32),
            pltpu.VMEM((tile, d), F32),
            pltpu.VMEM((tile // CHUNK, 3, CHUNK, DT_PAD), F32),
            pltpu.VMEM((tile // CHUNK, 2, DT_PAD, CHUNK), F32),
            pltpu.VMEM((tile // CHUNK * SSM_HEADS, CHUNK, CHUNK), F32),
            pltpu.VMEM((tile // CHUNK * SSM_HEADS, CHUNK, LANES), F32),
            pltpu.VMEM((tile // CHUNK, SSM_HEADS // 2, LANES), F32),
        ],
        compiler_params=pltpu.CompilerParams(
            dimension_semantics=("parallel", "arbitrary"), vmem_limit_bytes=VMEM_LIMIT_BYTES),
        name="layer",
    )(x, norm_g3, ada5, ada5, ada5, cos2, sin2, *([w_all] * (W_IN_BLOCKS + 1)),
      conv_w, conv_b, dt_bias, a_log, d_skip, ssm_ng, sc_w, wbr_ret, wbr_ssm, wbr_sc, w_out, final_g)


def kernel(x, c, positions, norm_g, w_ada, b_ada, w_in, ssm_conv_w, ssm_conv_b, ssm_dt_bias, ssm_a_log,
           ssm_d, ssm_norm_g, sc_conv_w, w_br_ret, w_br_ssm, w_br_sc, w_out, final_norm_g):
    bsz, seq, d = x.shape
    depth = w_in.shape[0]
    sc_width = sc_conv_w.shape[2]
    tile = 256
    assert seq % tile == 0 and tile % CHUNK == 0

    sizes = (RET_WIDTH, RET_WIDTH, RET_WIDTH, RET_WIDTH, SSM_WIDTH, SSM_XBC, SSM_HEADS,
             sc_width, sc_width, sc_width, sc_width, 3 * d)
    offs = [0]
    for s in sizes:
        offs.append(offs[-1] + s)
    assert offs[-1] == w_in.shape[2] and d == D_MODEL and sc_width == SC_WIDTH

    ada = _ada_call(c, w_ada, b_ada)
    ada5 = ada.reshape(depth, bsz, 3, 1, d)
    cos2, sin2 = _rope_call(positions)
    norm_g3 = norm_g.reshape(depth, 1, d)

    dt_pad = DT_PAD - SSM_HEADS
    assert offs[6] == SC_BLOCK * FILL_BLOCK
    weights = (_w_in_prep_call(w_in), w_br_ret.astype(BF16), w_br_ssm.astype(BF16), w_br_sc.astype(BF16), w_out.astype(BF16))
    pad_heads = lambda v: jnp.pad(v, ((0, 0), (0, dt_pad))).reshape(depth, 1, DT_PAD)
    smalls = (
        ssm_conv_w, ssm_conv_b.reshape(depth, 1, SSM_XBC),
        pad_heads(ssm_dt_bias), pad_heads(ssm_a_log),
        jnp.repeat(ssm_d, SSM_HEAD_DIM, axis=1).reshape(depth, 1, SSM_WIDTH),
        ssm_norm_g.reshape(depth, 1, SSM_WIDTH), sc_conv_w, final_norm_g.reshape(1, d),
    )
    for layer in range(depth):
        x = _layer_call(x, ada5, norm_g3, cos2, sin2, weights, smalls, layer, tile,
                        final=(layer == depth - 1))
    return x
```

```python
import functools
import math

import jax
import jax.numpy as jnp
from jax import lax
from jax.experimental import pallas as pl
from jax.experimental.pallas import tpu as pltpu

F32 = jnp.float32
BF16 = jnp.bfloat16

EPS = 1e-6
RET_HEADS = 4
RET_DIM = 128
RET_WIDTH = RET_HEADS * RET_DIM
ROPE_THETA = 10000.0
SSM_HEADS = 16
SSM_HEAD_DIM = 64
SSM_GROUPS = 2
SSM_STATE = 128
SSM_CONV = 4
SSM_WIDTH = SSM_HEADS * SSM_HEAD_DIM
SSM_XBC = SSM_WIDTH + 2 * SSM_GROUPS * SSM_STATE
SC_CONV = 3
CHUNK = 128
LANES = 128
SUBLANES = 8
DT_PAD = LANES
FILL_BLOCK = 512
D_MODEL = 1024
SC_WIDTH = D_MODEL // 2
RET_BLOCK = 0
Z_BLOCK = RET_BLOCK + 4 * RET_WIDTH // FILL_BLOCK
XBC_BLOCK = Z_BLOCK + SSM_WIDTH // FILL_BLOCK
SC_BLOCK = XBC_BLOCK + SSM_XBC // FILL_BLOCK
GATE_BLOCK = SC_BLOCK + 4 * SC_WIDTH // FILL_BLOCK
W_IN_BLOCKS = GATE_BLOCK + 3 * D_MODEL // FILL_BLOCK
VMEM_LIMIT_BYTES = 60 * 1024 * 1024

NEG_INF = float("-inf")


def _sigmoid(x):
    return 1.0 / (1.0 + jnp.exp(-x))


def _silu(x):
    return x * _sigmoid(x)


def _dot(a, b):
    return jnp.dot(a, b, preferred_element_type=F32)


def _modulated_norm(x, g, scale, shift):
    ms = jnp.mean(x * x, axis=-1, keepdims=True)
    y = x * lax.rsqrt(ms + EPS) * g
    return (y * (1.0 + scale) + shift).astype(BF16)


def _lane_bcast(a, col):
    return jnp.broadcast_to(a[:, col:col + 1], a.shape)


def _ada_kernel(c_ref, w_ref, b_ref, o_ref):
    act = _silu(c_ref[...])
    hi = act.astype(BF16)
    lo = (act - hi.astype(F32)).astype(BF16)
    w = w_ref[...].astype(BF16)
    o_ref[...] = _dot(hi, w) + _dot(lo, w) + b_ref[...]


def _ada_call(c, w_ada, b_ada):
    depth, d, d3 = w_ada.shape
    bsz = c.shape[0]
    nblk = d3 // d
    return pl.pallas_call(
        _ada_kernel,
        grid=(depth, nblk),
        in_specs=[
            pl.BlockSpec((bsz, d), lambda l, n: (0, 0)),
            pl.BlockSpec((None, d, d), lambda l, n: (l, 0, n)),
            pl.BlockSpec((None, 1, d), lambda l, n: (l, 0, n)),
        ],
        out_specs=pl.BlockSpec((None, bsz, d), lambda l, n: (l, 0, n)),
        out_shape=jax.ShapeDtypeStruct((depth, bsz, d3), F32),
        compiler_params=pltpu.CompilerParams(
            dimension_semantics=("parallel", "parallel"), vmem_limit_bytes=VMEM_LIMIT_BYTES),
        name="ada",
    )(c, w_ada, b_ada.reshape(depth, 1, d3))


def _rope_kernel(pos_ref, inv_ref, cos_ref, sin_ref):
    ang = pos_ref[...].astype(F32) * inv_ref[...]
    lane = lax.broadcasted_iota(jnp.int32, ang.shape, 1)
    cos_ref[...] = jnp.cos(ang)
    sin_ref[...] = jnp.where(lane < RET_DIM // 2, -jnp.sin(ang), jnp.sin(ang))


def _rope_call(positions):
    bsz, seq = positions.shape
    half = RET_DIM // 2
    inv_freq = ROPE_THETA ** (-jnp.arange(0, RET_DIM, 2, dtype=F32) / RET_DIM)
    inv2 = jnp.concatenate([inv_freq, inv_freq]).reshape(1, 2 * half)
    tbl = jax.ShapeDtypeStruct((bsz, seq, RET_DIM), F32)
    return pl.pallas_call(
        _rope_kernel,
        grid=(bsz,),
        in_specs=[
            pl.BlockSpec((None, seq, 1), lambda b: (b, 0, 0)),
            pl.BlockSpec((1, RET_DIM), lambda b: (0, 0)),
        ],
        out_specs=[pl.BlockSpec((None, seq, RET_DIM), lambda b: (b, 0, 0))] * 2,
        out_shape=[tbl, tbl],
        compiler_params=pltpu.CompilerParams(
            dimension_semantics=("parallel",), vmem_limit_bytes=VMEM_LIMIT_BYTES),
        name="rope_tables",
    )(positions.reshape(bsz, seq, 1), inv2)


def _run_fillers(fillers, slots_left):
    n = -(-len(fillers) // max(slots_left, 1))
    for _ in range(min(n, len(fillers))):
        fillers.pop(0)()


def _dot_t(a, b_t):
    return lax.dot_general(a, b_t, (((1,), (1,)), ((), ())), preferred_element_type=F32)


def _proj_block(dst_ref, h, wt_ref, dst_lo, act=None):
    def run():
        r = _dot_t(h, wt_ref[0])
        dst_ref[:, dst_lo:dst_lo + wt_ref.shape[1]] = r if act is None else act(r)
    return run


def _retention_unit(hd, c, qkvg_ref, cos_ref, sin_ref, state_ref, y_ref):
    def run():
        row = lax.broadcasted_iota(jnp.int32, (CHUNK, CHUNK), 0)
        col = lax.broadcasted_iota(jnp.int32, (CHUNK, CHUNK), 1)
        rowf = row.astype(F32)
        half = RET_DIM // 2
        log_g = math.log(1.0 - 2.0 ** (-5.0 - hd))
        decay_mask = jnp.exp(jnp.where(row >= col, (row - col).astype(F32) * log_g, NEG_INF))
        q_decay = jnp.exp((rowf + 1.0) * log_g)
        k_decay = jnp.exp((CHUNK - 1.0 - rowf) * log_g)
        chunk_decay = math.exp(CHUNK * log_g)
        rows = slice(c * CHUNK, (c + 1) * CHUNK)
        lo = hd * RET_DIM
        cos = cos_ref[rows, :]
        sin = sin_ref[rows, :]

        def rot(t):
            return t * cos + pltpu.roll(t, half, axis=1) * sin

        qc = (rot(qkvg_ref[rows, lo:lo + RET_DIM]) * (RET_DIM ** -0.5)).astype(BF16)
        kc = rot(qkvg_ref[rows, RET_WIDTH + lo:RET_WIDTH + lo + RET_DIM])
        vc = qkvg_ref[rows, 2 * RET_WIDTH + lo:2 * RET_WIDTH + lo + RET_DIM].astype(BF16)
        gate = qkvg_ref[rows, 3 * RET_WIDTH + lo:3 * RET_WIDTH + lo + RET_DIM]
        state = state_ref[hd]
        scores = lax.dot_general(qc, kc.astype(BF16), (((1,), (1,)), ((), ())), preferred_element_type=F32)
        inner = _dot((scores * decay_mask).astype(BF16), vc)
        cross = _dot(qc, state.astype(BF16)) * q_decay
        kd = (kc * k_decay).astype(BF16)
        state_ref[hd] = chunk_decay * state + lax.dot_general(kd, vc, (((0,), (0,)), ((), ())),
                                                              preferred_element_type=F32)
        o = inner + cross
        mu = jnp.mean(o, axis=-1, keepdims=True)
        oc = o - mu
        var = jnp.mean(oc * oc, axis=-1, keepdims=True)
        y = oc * lax.rsqrt(var + EPS) * _silu(gate)
        y_ref[rows, lo:lo + RET_DIM] = y.astype(y_ref.dtype)
    return run


def _block_diag_pair(slab):
    lane = lax.broadcasted_iota(jnp.int32, slab.shape, 1)
    first = lane < SSM_HEAD_DIM
    top = jnp.where(first, slab, 0.0)
    bot = jnp.where(first, 0.0, slab)
    return jnp.concatenate([top, bot], axis=0).astype(BF16)


def _ssd_conv(xbc_ref, xact_ref, convw_ref, convb_ref, tile, fillers):
    nblk = SSM_XBC // FILL_BLOCK
    for n in range(nblk):
        cols = slice(n * FILL_BLOCK, (n + 1) * FILL_BLOCK)
        conv = convb_ref[:, cols]
        for kk in range(SSM_CONV):
            off = SUBLANES - (SSM_CONV - 1) + kk
            conv = conv + convw_ref[kk:kk + 1, cols] * xbc_ref[off:off + tile, cols]
        xact_ref[:, cols] = _silu(conv)
        _run_fillers(fillers, nblk - n)
    xbc_ref[0:SUBLANES, :] = xbc_ref[tile:tile + SUBLANES, :]


T_ACS, T_WT, T_EACS = 0, 1, 2


def _ssd_terms_task(c, dt_ref, dtb_ref, alog_ref, term_ref, trans_ref):
    def run():
        rows = slice(c * CHUNK, (c + 1) * CHUNK)
        row = lax.broadcasted_iota(jnp.int32, (CHUNK, CHUNK), 0)
        col = lax.broadcasted_iota(jnp.int32, (CHUNK, CHUNK), 1)
        tri_f = (row >= col).astype(F32)
        dt_in = dt_ref[rows, :] + dtb_ref[...]
        dt_c = jnp.maximum(dt_in, 0.0) + jnp.log1p(jnp.exp(-jnp.abs(dt_in)))
        a_cs = jnp.dot(tri_f, dt_c * (-jnp.exp(alog_ref[...])), preferred_element_type=F32,
                       precision=lax.Precision.HIGHEST)
        a_last = a_cs[CHUNK - 1:CHUNK, :]
        term_ref[c, T_ACS] = a_cs
        term_ref[c, T_WT] = (jnp.exp(a_last - a_cs) * dt_c).T
        term_ref[c, T_EACS] = jnp.exp(a_cs)
        trans_ref[c, 0] = a_cs.T
        trans_ref[c, 1] = dt_c.T
    return run


def _ssd_head_task(c, heads, term_ref, trans_ref, ld_ref, eb_ref, cd_ref):
    def run():
        row = lax.broadcasted_iota(jnp.int32, (CHUNK, CHUNK), 0)
        col = lax.broadcasted_iota(jnp.int32, (CHUNK, CHUNK), 1)
        tril = row >= col
        lane1 = lax.broadcasted_iota(jnp.int32, (1, LANES), 1)
        a_cs = term_ref[c, T_ACS]
        e_acs = term_ref[c, T_EACS]
        cdec = e_acs[CHUNK - 1:CHUNK, :]
        for hh in heads:
            seg = _lane_bcast(a_cs, hh) - trans_ref[c, 0, hh:hh + 1, :]
            ld_ref[c * SSM_HEADS + hh] = jnp.exp(jnp.where(tril, seg, NEG_INF)) * trans_ref[c, 1, hh:hh + 1, :]
            eb_ref[c * SSM_HEADS + hh] = _lane_bcast(e_acs, hh)
        for h1 in heads[::2]:
            cd_ref[c, h1 // 2:h1 // 2 + 1, :] = jnp.where(
                lane1 < SSM_HEAD_DIM, _lane_bcast(cdec, h1), _lane_bcast(cdec, h1 + 1))
    return run


def _ssd_scan(xact_ref, z_ref, dskip_ref, ng_ref, term_ref, ld_ref, eb_ref, cd_ref,
              state_ref, ychunk_ref, y_ref, tile, fillers):
    pair = 2 * SSM_HEAD_DIM
    heads_per_group = SSM_HEADS // SSM_GROUPS
    n_chunks = tile // CHUNK
    slots = n_chunks * SSM_HEADS // 2

    for c in range(n_chunks):
        rows = slice(c * CHUNK, (c + 1) * CHUNK)
        for g in range(SSM_GROUPS):
            b_lo = SSM_WIDTH + g * SSM_STATE
            c_lo = SSM_WIDTH + SSM_GROUPS * SSM_STATE + g * SSM_STATE
            b_g = xact_ref[rows, b_lo:b_lo + SSM_STATE]
            c_g = xact_ref[rows, c_lo:c_lo + SSM_STATE]
            b_t = b_g.T
            cb = _dot(c_g.astype(BF16), b_t.astype(BF16))
            for p in range(heads_per_group // 2):
                h1 = g * heads_per_group + 2 * p
                lo = h1 * SSM_HEAD_DIM
                xs_pair = xact_ref[rows, lo:lo + pair]
                rhs = _block_diag_pair(xs_pair)
                m_parts, s_parts, o_parts = [], [], []
                for hh in (h1, h1 + 1):
                    m_parts.append(cb * ld_ref[c * SSM_HEADS + hh])
                    s_parts.append(b_t * term_ref[c, T_WT, hh:hh + 1, :])
                    o_parts.append(c_g * eb_ref[c * SSM_HEADS + hh])
                y_diag = _dot(jnp.concatenate(m_parts, axis=1).astype(BF16), rhs)
                s_new = _dot(jnp.concatenate(s_parts, axis=1).astype(BF16), rhs)
                s_prev = state_ref[:, lo:lo + pair]
                y_off = _dot(jnp.concatenate(o_parts, axis=1).astype(BF16), _block_diag_pair(s_prev))
                state_ref[:, lo:lo + pair] = cd_ref[c, h1 // 2:h1 // 2 + 1, :] * s_prev + s_new
                ychunk_ref[:, lo:lo + pair] = y_diag + y_off + dskip_ref[:, lo:lo + pair] * xs_pair
                _run_fillers(fillers, slots)
                slots -= 1
        y = ychunk_ref[...] * _silu(z_ref[rows, :])
        ms = jnp.mean(y * y, axis=-1, keepdims=True)
        y_ref[rows, :] = (y * lax.rsqrt(ms + EPS) * ng_ref[...]).astype(y_ref.dtype)


def _short_conv_task(cp_ref, scw_ref, u_ref, y_ref, tile):
    def run():
        width = u_ref.shape[1]
        u_ref[SUBLANES:SUBLANES + tile, :] = cp_ref[:, width:2 * width] * cp_ref[:, 2 * width:3 * width]
        conv = jnp.zeros((tile, width), F32)
        for kk in range(SC_CONV):
            off = SUBLANES - (SC_CONV - 1) + kk
            conv = conv + scw_ref[kk:kk + 1, :] * u_ref[off:off + tile, :]
        u_ref[0:SUBLANES, :] = u_ref[tile:tile + SUBLANES, :]
        y_ref[...] = (cp_ref[:, :width] * conv * _silu(cp_ref[:, 3 * width:])).astype(y_ref.dtype)
    return run


def _layer_kernel(x_ref, g_ref, scale_ref, shift_ref, rgate_ref, cos_ref, sin_ref, *refs, tile, final):
    win = refs[:W_IN_BLOCKS]
    (wdt_ref, convw_ref, convb_ref, dtb_ref, alog_ref, dskip_ref, sng_ref, scw_ref,
     wbr_ret_ref, wbr_ssm_ref, wbr_sc_ref, wout_ref, fg_ref,
     o_ref,
     rstate_ref, xbc_ref, sstate_ref, ychunk_ref, u_ref, yret_ref, yssm_ref, ysc_ref,
     qkvg_ref, z_ref, dt_ref, xact_ref, cp_ref, gates_ref, acc_ref,
     term_ref, trans_ref, ld_ref, eb_ref, cd_ref) = refs[W_IN_BLOCKS:]

    @pl.when(pl.program_id(1) == 0)
    def _():
        rstate_ref[...] = jnp.zeros_like(rstate_ref)
        sstate_ref[...] = jnp.zeros_like(sstate_ref)
        xbc_ref[0:SUBLANES, :] = jnp.zeros((SUBLANES, xbc_ref.shape[1]), F32)
        u_ref[0:SUBLANES, :] = jnp.zeros((SUBLANES, u_ref.shape[1]), F32)

    x = x_ref[...]
    d = x.shape[1]
    h = _modulated_norm(x, g_ref[...], scale_ref[...], shift_ref[...])
    blk = FILL_BLOCK

    def proj(dst_ref, first_block, n, act=None):
        return _proj_block(dst_ref, h, win[first_block + n], n * blk, act)

    for n in range(SSM_XBC // blk):
        xbc_ref[SUBLANES:SUBLANES + tile, n * blk:(n + 1) * blk] = _dot_t(h, win[XBC_BLOCK + n][0])
    n_chunks = tile // CHUNK
    fillers = [_proj_block(dt_ref, h, wdt_ref, 0)]
    fillers += [_ssd_terms_task(c, dt_ref, dtb_ref, alog_ref, term_ref, trans_ref) for c in range(n_chunks)]
    ret_proj = [proj(qkvg_ref, RET_BLOCK, n) for n in range(4 * RET_WIDTH // blk)]
    quad = 4
    head_tasks = [_ssd_head_task(c, list(range(q, q + quad)), term_ref, trans_ref, ld_ref, eb_ref, cd_ref)
                  for c in range(n_chunks) for q in range(0, SSM_HEADS, quad)]
    fillers += [t for duo in zip(ret_proj, head_tasks) for t in duo] + head_tasks[len(ret_proj):]
    _ssd_conv(xbc_ref, xact_ref, convw_ref, convb_ref, tile, fillers)

    def ret_merge():
        acc_ref[...] = gates_ref[:, :d] * _dot(yret_ref[...], wbr_ret_ref[...])

    def sc_merge():
        acc_ref[...] += gates_ref[:, 2 * d:] * _dot(ysc_ref[...], wbr_sc_ref[...])

    dense = [proj(z_ref, Z_BLOCK, n) for n in range(SSM_WIDTH // blk)]
    dense += [proj(cp_ref, SC_BLOCK, n) for n in range(cp_ref.shape[1] // blk)]
    dense += [proj(gates_ref, GATE_BLOCK, n, _sigmoid) for n in range(3 * d // blk)]
    ret_units = [_retention_unit(hd, c, qkvg_ref, cos_ref, sin_ref, rstate_ref, yret_ref)
                 for c in range(tile // CHUNK) for hd in range(RET_HEADS)]
    fillers = [t for duo in zip(dense, ret_units) for t in duo] + dense[len(ret_units):]
    fillers += [ret_merge, _short_conv_task(cp_ref, scw_ref, u_ref, ysc_ref, tile), sc_merge]
    _ssd_scan(xact_ref, z_ref, dskip_ref, sng_ref, term_ref, ld_ref, eb_ref, cd_ref,
              sstate_ref, ychunk_ref, yssm_ref, tile, fillers)
    _run_fillers(fillers, 1)

    merged = acc_ref[...] + gates_ref[:, d:2 * d] * _dot(yssm_ref[...], wbr_ssm_ref[...])
    out = _dot(merged.astype(BF16), wout_ref[...])
    xn = x + rgate_ref[...] * out
    if final:
        ms = jnp.mean(xn * xn, axis=-1, keepdims=True)
        xn = xn * lax.rsqrt(ms + EPS) * fg_ref[...]
    o_ref[...] = xn


def _layer_call(x, ada5, norm_g3, cos2, sin2, weights, smalls, layer, tile, final):
    bsz, seq, d = x.shape
    row_spec = lambda idx: pl.BlockSpec((None, None, None, 1, d), lambda b, j: (layer, b, idx, 0, 0))
    per_layer = lambda a: pl.BlockSpec((None,) + a.shape[1:], lambda b, j: (layer,) + (0,) * (a.ndim - 1),
                                       pipeline_mode=pl.Buffered(1))
    tok = lambda w: pl.BlockSpec((None, tile, w), lambda b, j: (b, j, 0))
    w_in_t, wbr_ret, wbr_ssm, wbr_sc, w_out = weights
    conv_w, conv_b, dt_bias, a_log, d_skip, ssm_ng, sc_w, final_g = smalls
    width = sc_w.shape[2]
    w_in_rows = lambda start, rows: pl.BlockSpec(
        (pl.Element(1), pl.Element(rows), pl.Element(d)), lambda b, j: (layer, start, 0),
        pipeline_mode=pl.Buffered(1))
    dt_start = SC_BLOCK * FILL_BLOCK
    block_start = lambda n: n * FILL_BLOCK + (SSM_HEADS if n >= SC_BLOCK else 0)
    return pl.pallas_call(
        functools.partial(_layer_kernel, tile=tile, final=final),
        grid=(bsz, seq // tile),
        in_specs=[
            tok(d),
            per_layer(norm_g3),
            row_spec(1),
            row_spec(0),
            row_spec(2),
            tok(RET_DIM), tok(RET_DIM),
        ] + [w_in_rows(block_start(n), FILL_BLOCK) for n in range(W_IN_BLOCKS)] + [
            w_in_rows(dt_start, DT_PAD),
            per_layer(conv_w), per_layer(conv_b), per_layer(dt_bias), per_layer(a_log), per_layer(d_skip),
            per_layer(ssm_ng), per_layer(sc_w),
            per_layer(wbr_ret), per_layer(wbr_ssm), per_layer(wbr_sc), per_layer(w_out),
            pl.BlockSpec(final_g.shape, lambda b, j: (0, 0), pipeline_mode=pl.Buffered(1)),
        ],
        out_specs=tok(d),
        out_shape=jax.ShapeDtypeStruct((bsz, seq, d), F32),
        scratch_shapes=[
            pltpu.VMEM((RET_HEADS, RET_DIM, RET_DIM), F32),
            pltpu.VMEM((tile + SUBLANES, SSM_XBC), F32),
            pltpu.VMEM((SSM_STATE, SSM_WIDTH), F32),
            pltpu.VMEM((CHUNK, SSM_WIDTH), F32),
            pltpu.VMEM((tile + SUBLANES, width), F32),
            pltpu.VMEM((tile, RET_WIDTH), BF16),
            pltpu.VMEM((tile, SSM_WIDTH), BF16),
            pltpu.VMEM((tile, width), BF16),
            pltpu.VMEM((tile, 4 * RET_WIDTH), F32),
            pltpu.VMEM((tile, SSM_WIDTH), F32),
            pltpu.VMEM((tile, DT_PAD), F32),
            pltpu.VMEM((tile, SSM_XBC), F32),
            pltpu.VMEM((tile, 4 * width), F32),
            pltpu.VMEM((tile, 3 * d), F32),
            pltpu.VMEM((tile, d), F32),
            pltpu.VMEM((tile // CHUNK, 3, CHUNK, DT_PAD), F32),
            pltpu.VMEM((tile // CHUNK, 2, DT_PAD, CHUNK), F32),
            pltpu.VMEM((tile // CHUNK * SSM_HEADS, CHUNK, CHUNK), F32),
            pltpu.VMEM((tile // CHUNK * SSM_HEADS, CHUNK, LANES), F32),
            pltpu.VMEM((tile // CHUNK, SSM_HEADS // 2, LANES), F32),
        ],
        compiler_params=pltpu.CompilerParams(
            dimension_semantics=("parallel", "arbitrary"), vmem_limit_bytes=VMEM_LIMIT_BYTES),
        name="layer",
    )(x, norm_g3, ada5, ada5, ada5, cos2, sin2, *([w_in_t] * (W_IN_BLOCKS + 1)),
      conv_w, conv_b, dt_bias, a_log, d_skip, ssm_ng, sc_w, wbr_ret, wbr_ssm, wbr_sc, w_out, final_g)


def kernel(x, c, positions, norm_g, w_ada, b_ada, w_in, ssm_conv_w, ssm_conv_b, ssm_dt_bias, ssm_a_log,
           ssm_d, ssm_norm_g, sc_conv_w, w_br_ret, w_br_ssm, w_br_sc, w_out, final_norm_g):
    bsz, seq, d = x.shape
    depth = w_in.shape[0]
    sc_width = sc_conv_w.shape[2]
    tile = 256
    assert seq % tile == 0 and tile % CHUNK == 0

    sizes = (RET_WIDTH, RET_WIDTH, RET_WIDTH, RET_WIDTH, SSM_WIDTH, SSM_XBC, SSM_HEADS,
             sc_width, sc_width, sc_width, sc_width, 3 * d)
    offs = [0]
    for s in sizes:
        offs.append(offs[-1] + s)
    assert offs[-1] == w_in.shape[2] and d == D_MODEL and sc_width == SC_WIDTH

    ada = _ada_call(c, w_ada, b_ada)
    ada5 = ada.reshape(depth, bsz, 3, 1, d)
    cos2, sin2 = _rope_call(positions)
    norm_g3 = norm_g.reshape(depth, 1, d)

    dt_pad = DT_PAD - SSM_HEADS
    assert offs[6] == SC_BLOCK * FILL_BLOCK
    weights = (jnp.swapaxes(w_in, 1, 2).astype(BF16), w_br_ret.astype(BF16), w_br_ssm.astype(BF16), w_br_sc.astype(BF16), w_out.astype(BF16))
    pad_heads = lambda v: jnp.pad(v, ((0, 0), (0, dt_pad))).reshape(depth, 1, DT_PAD)
    smalls = (
        ssm_conv_w, ssm_conv_b.reshape(depth, 1, SSM_XBC),
        pad_heads(ssm_dt_bias), pad_heads(ssm_a_log),
        jnp.repeat(ssm_d, SSM_HEAD_DIM, axis=1).reshape(depth, 1, SSM_WIDTH),
        ssm_norm_g.reshape(depth, 1, SSM_WIDTH), sc_conv_w, final_norm_g.reshape(1, d),
    )
    for layer in range(depth):
        x = _layer_call(x, ada5, norm_g3, cos2, sin2, weights, smalls, layer, tile,
                        final=(layer == depth - 1))
    return x
```

```python
import functools
import math

import jax
import jax.numpy as jnp
from jax import lax
from jax.experimental import pallas as pl
from jax.experimental.pallas import tpu as pltpu

F32 = jnp.float32
BF16 = jnp.bfloat16

EPS = 1e-6
RET_HEADS = 4
RET_DIM = 128
RET_WIDTH = RET_HEADS * RET_DIM
ROPE_THETA = 10000.0
SSM_HEADS = 16
SSM_HEAD_DIM = 64
SSM_GROUPS = 2
SSM_STATE = 128
SSM_CONV = 4
SSM_WIDTH = SSM_HEADS * SSM_HEAD_DIM
SSM_XBC = SSM_WIDTH + 2 * SSM_GROUPS * SSM_STATE
SC_CONV = 3
CHUNK = 128
LANES = 128
SUBLANES = 8
DT_PAD = LANES
FILL_BLOCK = 512
D_MODEL = 1024
SC_WIDTH = D_MODEL // 2
RET_BLOCK = 0
Z_BLOCK = RET_BLOCK + 4 * RET_WIDTH // FILL_BLOCK
XBC_BLOCK = Z_BLOCK + SSM_WIDTH // FILL_BLOCK
SC_BLOCK = XBC_BLOCK + SSM_XBC // FILL_BLOCK
GATE_BLOCK = SC_BLOCK + 4 * SC_WIDTH // FILL_BLOCK
W_IN_BLOCKS = GATE_BLOCK + 3 * D_MODEL // FILL_BLOCK
VMEM_LIMIT_BYTES = 60 * 1024 * 1024

NEG_INF = float("-inf")


def _sigmoid(x):
    return 1.0 / (1.0 + jnp.exp(-x))


def _silu(x):
    return x * _sigmoid(x)


def _dot(a, b):
    return jnp.dot(a, b, preferred_element_type=F32)


def _modulated_norm(x, g, scale, shift):
    ms = jnp.mean(x * x, axis=-1, keepdims=True)
    y = x * lax.rsqrt(ms + EPS) * g
    return (y * (1.0 + scale) + shift).astype(BF16)


def _lane_bcast(a, col):
    return jnp.broadcast_to(a[:, col:col + 1], a.shape)


def _ada_kernel(c_ref, w_ref, b_ref, o_ref):
    act = _silu(c_ref[...])
    hi = act.astype(BF16)
    lo = (act - hi.astype(F32)).astype(BF16)
    w = w_ref[...].astype(BF16)
    o_ref[...] = _dot(hi, w) + _dot(lo, w) + b_ref[...]


def _ada_call(c, w_ada, b_ada):
    depth, d, d3 = w_ada.shape
    bsz = c.shape[0]
    nblk = d3 // d
    return pl.pallas_call(
        _ada_kernel,
        grid=(depth, nblk),
        in_specs=[
            pl.BlockSpec((bsz, d), lambda l, n: (0, 0)),
            pl.BlockSpec((None, d, d), lambda l, n: (l, 0, n)),
            pl.BlockSpec((None, 1, d), lambda l, n: (l, 0, n)),
        ],
        out_specs=pl.BlockSpec((None, bsz, d), lambda l, n: (l, 0, n)),
        out_shape=jax.ShapeDtypeStruct((depth, bsz, d3), F32),
        compiler_params=pltpu.CompilerParams(
            dimension_semantics=("parallel", "parallel"), vmem_limit_bytes=VMEM_LIMIT_BYTES),
        name="ada",
    )(c, w_ada, b_ada.reshape(depth, 1, d3))


def _w_in_prep_kernel(wt_ref, o_ref):
    o_ref[...] = wt_ref[0].T.astype(o_ref.dtype)


def _w_in_prep_call(w_in):
    depth, d, width = w_in.shape
    assert width == W_IN_BLOCKS * FILL_BLOCK + SSM_HEADS
    w_t = jnp.swapaxes(w_in, 1, 2)
    dt_start = SC_BLOCK * FILL_BLOCK
    per_block = FILL_BLOCK // SSM_HEADS
    start = lambda n: SSM_HEADS * jnp.where(n == W_IN_BLOCKS, dt_start // SSM_HEADS,
                                            n * per_block + jnp.where(n >= SC_BLOCK, 1, 0))
    return pl.pallas_call(
        _w_in_prep_kernel,
        grid=(depth, W_IN_BLOCKS + 1),
        in_specs=[pl.BlockSpec((pl.Element(1), pl.Element(FILL_BLOCK), pl.Element(d)),
                               lambda l, n: (l, start(n), 0))],
        out_specs=pl.BlockSpec((None, d, FILL_BLOCK), lambda l, n: (l, 0, n)),
        out_shape=jax.ShapeDtypeStruct((depth, d, (W_IN_BLOCKS + 1) * FILL_BLOCK), BF16),
        compiler_params=pltpu.CompilerParams(
            dimension_semantics=("parallel", "parallel"), vmem_limit_bytes=VMEM_LIMIT_BYTES),
        name="w_in_prep",
    )(w_t)


def _rope_kernel(pos_ref, inv_ref, cos_ref, sin_ref):
    ang = pos_ref[...].astype(F32) * inv_ref[...]
    lane = lax.broadcasted_iota(jnp.int32, ang.shape, 1)
    cos_ref[...] = jnp.cos(ang)
    sin_ref[...] = jnp.where(lane < RET_DIM // 2, -jnp.sin(ang), jnp.sin(ang))


def _rope_call(positions):
    bsz, seq = positions.shape
    half = RET_DIM // 2
    inv_freq = ROPE_THETA ** (-jnp.arange(0, RET_DIM, 2, dtype=F32) / RET_DIM)
    inv2 = jnp.concatenate([inv_freq, inv_freq]).reshape(1, 2 * half)
    tbl = jax.ShapeDtypeStruct((bsz, seq, RET_DIM), F32)
    return pl.pallas_call(
        _rope_kernel,
        grid=(bsz,),
        in_specs=[
            pl.BlockSpec((None, seq, 1), lambda b: (b, 0, 0)),
            pl.BlockSpec((1, RET_DIM), lambda b: (0, 0)),
        ],
        out_specs=[pl.BlockSpec((None, seq, RET_DIM), lambda b: (b, 0, 0))] * 2,
        out_shape=[tbl, tbl],
        compiler_params=pltpu.CompilerParams(
            dimension_semantics=("parallel",), vmem_limit_bytes=VMEM_LIMIT_BYTES),
        name="rope_tables",
    )(positions.reshape(bsz, seq, 1), inv2)


def _run_fillers(fillers, slots_left):
    n = -(-len(fillers) // max(slots_left, 1))
    for _ in range(min(n, len(fillers))):
        fillers.pop(0)()


def _proj_block(dst_ref, h, w_ref, dst_lo, act=None):
    def run():
        r = _dot(h, w_ref[...])
        dst_ref[:, dst_lo:dst_lo + w_ref.shape[1]] = r if act is None else act(r)
    return run


def _retention_unit(hd, c, qkvg_ref, cos_ref, sin_ref, state_ref, y_ref):
    def run():
        row = lax.broadcasted_iota(jnp.int32, (CHUNK, CHUNK), 0)
        col = lax.broadcasted_iota(jnp.int32, (CHUNK, CHUNK), 1)
        rowf = row.astype(F32)
        half = RET_DIM // 2
        log_g = math.log(1.0 - 2.0 ** (-5.0 - hd))
        decay_mask = jnp.exp(jnp.where(row >= col, (row - col).astype(F32) * log_g, NEG_INF))
        q_decay = jnp.exp((rowf + 1.0) * log_g)
        k_decay = jnp.exp((CHUNK - 1.0 - rowf) * log_g)
        chunk_decay = math.exp(CHUNK * log_g)
        rows = slice(c * CHUNK, (c + 1) * CHUNK)
        lo = hd * RET_DIM
        cos = cos_ref[rows, :]
        sin = sin_ref[rows, :]

        def rot(t):
            return t * cos + pltpu.roll(t, half, axis=1) * sin

        qc = (rot(qkvg_ref[rows, lo:lo + RET_DIM]) * (RET_DIM ** -0.5)).astype(BF16)
        kc = rot(qkvg_ref[rows, RET_WIDTH + lo:RET_WIDTH + lo + RET_DIM])
        vc = qkvg_ref[rows, 2 * RET_WIDTH + lo:2 * RET_WIDTH + lo + RET_DIM].astype(BF16)
        gate = qkvg_ref[rows, 3 * RET_WIDTH + lo:3 * RET_WIDTH + lo + RET_DIM]
        state = state_ref[hd]
        scores = lax.dot_general(qc, kc.astype(BF16), (((1,), (1,)), ((), ())), preferred_element_type=F32)
        inner = _dot((scores * decay_mask).astype(BF16), vc)
        cross = _dot(qc, state.astype(BF16)) * q_decay
        kd = (kc * k_decay).astype(BF16)
        state_ref[hd] = chunk_decay * state + lax.dot_general(kd, vc, (((0,), (0,)), ((), ())),
                                                              preferred_element_type=F32)
        o = inner + cross
        mu = jnp.mean(o, axis=-1, keepdims=True)
        oc = o - mu
        var = jnp.mean(oc * oc, axis=-1, keepdims=True)
        y = oc * lax.rsqrt(var + EPS) * _silu(gate)
        y_ref[rows, lo:lo + RET_DIM] = y.astype(y_ref.dtype)
    return run


def _block_diag_pair(slab):
    lane = lax.broadcasted_iota(jnp.int32, slab.shape, 1)
    first = lane < SSM_HEAD_DIM
    top = jnp.where(first, slab, 0.0)
    bot = jnp.where(first, 0.0, slab)
    return jnp.concatenate([top, bot], axis=0).astype(BF16)


def _ssd_conv(xbc_ref, xact_ref, convw_ref, convb_ref, tile, fillers):
    nblk = SSM_XBC // FILL_BLOCK
    for n in range(nblk):
        cols = slice(n * FILL_BLOCK, (n + 1) * FILL_BLOCK)
        conv = convb_ref[:, cols]
        for kk in range(SSM_CONV):
            off = SUBLANES - (SSM_CONV - 1) + kk
            conv = conv + convw_ref[kk:kk + 1, cols] * xbc_ref[off:off + tile, cols]
        xact_ref[:, cols] = _silu(conv)
        _run_fillers(fillers, nblk - n)
    xbc_ref[0:SUBLANES, :] = xbc_ref[tile:tile + SUBLANES, :]


T_ACS, T_WT, T_EACS = 0, 1, 2


def _ssd_terms_task(c, dt_ref, dtb_ref, alog_ref, term_ref, trans_ref):
    def run():
        rows = slice(c * CHUNK, (c + 1) * CHUNK)
        row = lax.broadcasted_iota(jnp.int32, (CHUNK, CHUNK), 0)
        col = lax.broadcasted_iota(jnp.int32, (CHUNK, CHUNK), 1)
        tri_f = (row >= col).astype(F32)
        dt_in = dt_ref[rows, :] + dtb_ref[...]
        dt_c = jnp.maximum(dt_in, 0.0) + jnp.log1p(jnp.exp(-jnp.abs(dt_in)))
        a_cs = jnp.dot(tri_f, dt_c * (-jnp.exp(alog_ref[...])), preferred_element_type=F32,
                       precision=lax.Precision.HIGHEST)
        a_last = a_cs[CHUNK - 1:CHUNK, :]
        term_ref[c, T_ACS] = a_cs
        term_ref[c, T_WT] = (jnp.exp(a_last - a_cs) * dt_c).T
        term_ref[c, T_EACS] = jnp.exp(a_cs)
        trans_ref[c, 0] = a_cs.T
        trans_ref[c, 1] = dt_c.T
    return run


def _ssd_head_task(c, heads, term_ref, trans_ref, ld_ref, eb_ref, cd_ref):
    def run():
        row = lax.broadcasted_iota(jnp.int32, (CHUNK, CHUNK), 0)
        col = lax.broadcasted_iota(jnp.int32, (CHUNK, CHUNK), 1)
        tril = row >= col
        lane1 = lax.broadcasted_iota(jnp.int32, (1, LANES), 1)
        a_cs = term_ref[c, T_ACS]
        e_acs = term_ref[c, T_EACS]
        cdec = e_acs[CHUNK - 1:CHUNK, :]
        for hh in heads:
            seg = _lane_bcast(a_cs, hh) - trans_ref[c, 0, hh:hh + 1, :]
            ld_ref[c * SSM_HEADS + hh] = jnp.exp(jnp.where(tril, seg, NEG_INF)) * trans_ref[c, 1, hh:hh + 1, :]
            eb_ref[c * SSM_HEADS + hh] = _lane_bcast(e_acs, hh)
        for h1 in heads[::2]:
            cd_ref[c, h1 // 2:h1 // 2 + 1, :] = jnp.where(
                lane1 < SSM_HEAD_DIM, _lane_bcast(cdec, h1), _lane_bcast(cdec, h1 + 1))
    return run


def _ssd_scan(xact_ref, z_ref, dskip_ref, ng_ref, term_ref, ld_ref, eb_ref, cd_ref,
              state_ref, ychunk_ref, y_ref, tile, fillers):
    pair = 2 * SSM_HEAD_DIM
    heads_per_group = SSM_HEADS // SSM_GROUPS
    n_chunks = tile // CHUNK
    slots = n_chunks * SSM_HEADS // 2

    for c in range(n_chunks):
        rows = slice(c * CHUNK, (c + 1) * CHUNK)
        for g in range(SSM_GROUPS):
            b_lo = SSM_WIDTH + g * SSM_STATE
            c_lo = SSM_WIDTH + SSM_GROUPS * SSM_STATE + g * SSM_STATE
            b_g = xact_ref[rows, b_lo:b_lo + SSM_STATE]
            c_g = xact_ref[rows, c_lo:c_lo + SSM_STATE]
            b_t = b_g.T
            cb = _dot(c_g.astype(BF16), b_t.astype(BF16))
            for p in range(heads_per_group // 2):
                h1 = g * heads_per_group + 2 * p
                lo = h1 * SSM_HEAD_DIM
                xs_pair = xact_ref[rows, lo:lo + pair]
                rhs = _block_diag_pair(xs_pair)
                m_parts, s_parts, o_parts = [], [], []
                for hh in (h1, h1 + 1):
                    m_parts.append(cb * ld_ref[c * SSM_HEADS + hh])
                    s_parts.append(b_t * term_ref[c, T_WT, hh:hh + 1, :])
                    o_parts.append(c_g * eb_ref[c * SSM_HEADS + hh])
                y_diag = _dot(jnp.concatenate(m_parts, axis=1).astype(BF16), rhs)
                s_new = _dot(jnp.concatenate(s_parts, axis=1).astype(BF16), rhs)
                s_prev = state_ref[:, lo:lo + pair]
                y_off = _dot(jnp.concatenate(o_parts, axis=1).astype(BF16), _block_diag_pair(s_prev))
                state_ref[:, lo:lo + pair] = cd_ref[c, h1 // 2:h1 // 2 + 1, :] * s_prev + s_new
                ychunk_ref[:, lo:lo + pair] = y_diag + y_off + dskip_ref[:, lo:lo + pair] * xs_pair
                _run_fillers(fillers, slots)
                slots -= 1
        y = ychunk_ref[...] * _silu(z_ref[rows, :])
        ms = jnp.mean(y * y, axis=-1, keepdims=True)
        y_ref[rows, :] = (y * lax.rsqrt(ms + EPS) * ng_ref[...]).astype(y_ref.dtype)


def _short_conv_task(cp_ref, scw_ref, u_ref, y_ref, tile):
    def run():
        width = u_ref.shape[1]
        u_ref[SUBLANES:SUBLANES + tile, :] = cp_ref[:, width:2 * width] * cp_ref[:, 2 * width:3 * width]
        conv = jnp.zeros((tile, width), F32)
        for kk in range(SC_CONV):
            off = SUBLANES - (SC_CONV - 1) + kk
            conv = conv + scw_ref[kk:kk + 1, :] * u_ref[off:off + tile, :]
        u_ref[0:SUBLANES, :] = u_ref[tile:tile + SUBLANES, :]
        y_ref[...] = (cp_ref[:, :width] * conv * _silu(cp_ref[:, 3 * width:])).astype(y_ref.dtype)
    return run


def _layer_kernel(x_ref, g_ref, scale_ref, shift_ref, rgate_ref, cos_ref, sin_ref, *refs, tile, final):
    win = refs[:W_IN_BLOCKS]
    (wdt_ref, convw_ref, convb_ref, dtb_ref, alog_ref, dskip_ref, sng_ref, scw_ref,
     wbr_ret_ref, wbr_ssm_ref, wbr_sc_ref, wout_ref, fg_ref,
     o_ref,
     rstate_ref, xbc_ref, sstate_ref, ychunk_ref, u_ref, yret_ref, yssm_ref, ysc_ref,
     qkvg_ref, z_ref, dt_ref, xact_ref, cp_ref, gates_ref, acc_ref,
     term_ref, trans_ref, ld_ref, eb_ref, cd_ref) = refs[W_IN_BLOCKS:]

    @pl.when(pl.program_id(1) == 0)
    def _():
        rstate_ref[...] = jnp.zeros_like(rstate_ref)
        sstate_ref[...] = jnp.zeros_like(sstate_ref)
        xbc_ref[0:SUBLANES, :] = jnp.zeros((SUBLANES, xbc_ref.shape[1]), F32)
        u_ref[0:SUBLANES, :] = jnp.zeros((SUBLANES, u_ref.shape[1]), F32)

    x = x_ref[...]
    d = x.shape[1]
    h = _modulated_norm(x, g_ref[...], scale_ref[...], shift_ref[...])
    blk = FILL_BLOCK

    def proj(dst_ref, first_block, n, act=None):
        return _proj_block(dst_ref, h, win[first_block + n], n * blk, act)

    for n in range(SSM_XBC // blk):
        xbc_ref[SUBLANES:SUBLANES + tile, n * blk:(n + 1) * blk] = _dot(h, win[XBC_BLOCK + n][...])
    n_chunks = tile // CHUNK
    fillers = [_proj_block(dt_ref, h, wdt_ref, 0)]
    fillers += [_ssd_terms_task(c, dt_ref, dtb_ref, alog_ref, term_ref, trans_ref) for c in range(n_chunks)]
    ret_proj = [proj(qkvg_ref, RET_BLOCK, n) for n in range(4 * RET_WIDTH // blk)]
    quad = 4
    head_tasks = [_ssd_head_task(c, list(range(q, q + quad)), term_ref, trans_ref, ld_ref, eb_ref, cd_ref)
                  for c in range(n_chunks) for q in range(0, SSM_HEADS, quad)]
    fillers += [t for duo in zip(ret_proj, head_tasks) for t in duo] + head_tasks[len(ret_proj):]
    _ssd_conv(xbc_ref, xact_ref, convw_ref, convb_ref, tile, fillers)

    def ret_merge():
        acc_ref[...] = gates_ref[:, :d] * _dot(yret_ref[...], wbr_ret_ref[...])

    def sc_merge():
        acc_ref[...] += gates_ref[:, 2 * d:] * _dot(ysc_ref[...], wbr_sc_ref[...])

    dense = [proj(z_ref, Z_BLOCK, n) for n in range(SSM_WIDTH // blk)]
    dense += [proj(cp_ref, SC_BLOCK, n) for n in range(cp_ref.shape[1] // blk)]
    dense += [proj(gates_ref, GATE_BLOCK, n, _sigmoid) for n in range(3 * d // blk)]
    ret_units = [_retention_unit(hd, c, qkvg_ref, cos_ref, sin_ref, rstate_ref, yret_ref)
                 for c in range(tile // CHUNK) for hd in range(RET_HEADS)]
    fillers = [t for duo in zip(dense, ret_units) for t in duo] + dense[len(ret_units):]
    fillers += [ret_merge, _short_conv_task(cp_ref, scw_ref, u_ref, ysc_ref, tile), sc_merge]
    _ssd_scan(xact_ref, z_ref, dskip_ref, sng_ref, term_ref, ld_ref, eb_ref, cd_ref,
              sstate_ref, ychunk_ref, yssm_ref, tile, fillers)
    _run_fillers(fillers, 1)

    merged = acc_ref[...] + gates_ref[:, d:2 * d] * _dot(yssm_ref[...], wbr_ssm_ref[...])
    out = _dot(merged.astype(BF16), wout_ref[...])
    xn = x + rgate_ref[...] * out
    if final:
        ms = jnp.mean(xn * xn, axis=-1, keepdims=True)
        xn = xn * lax.rsqrt(ms + EPS) * fg_ref[...]
    o_ref[...] = xn


def _layer_call(x, ada5, norm_g3, cos2, sin2, weights, smalls, layer, tile, final):
    bsz, seq, d = x.shape
    row_spec = lambda idx: pl.BlockSpec((None, None, None, 1, d), lambda b, j: (layer, b, idx, 0, 0))
    per_layer = lambda a: pl.BlockSpec((None,) + a.shape[1:], lambda b, j: (layer,) + (0,) * (a.ndim - 1),
                                       pipeline_mode=pl.Buffered(1))
    tok = lambda w: pl.BlockSpec((None, tile, w), lambda b, j: (b, j, 0))
    w_blocks, wbr_ret, wbr_ssm, wbr_sc, w_out = weights
    conv_w, conv_b, dt_bias, a_log, d_skip, ssm_ng, sc_w, final_g = smalls
    width = sc_w.shape[2]
    w_in_block = lambda cols, n: pl.BlockSpec((None, d, cols), lambda b, j: (layer, 0, n),
                                              pipeline_mode=pl.Buffered(1))
    dt_block = W_IN_BLOCKS * FILL_BLOCK // DT_PAD
    return pl.pallas_call(
        functools.partial(_layer_kernel, tile=tile, final=final),
        grid=(bsz, seq // tile),
        in_specs=[
            tok(d),
            per_layer(norm_g3),
            row_spec(1),
            row_spec(0),
            row_spec(2),
            tok(RET_DIM), tok(RET_DIM),
        ] + [w_in_block(FILL_BLOCK, n) for n in range(W_IN_BLOCKS)] + [
            w_in_block(DT_PAD, dt_block),
            per_layer(conv_w), per_layer(conv_b), per_layer(dt_bias), per_layer(a_log), per_layer(d_skip),
            per_layer(ssm_ng), per_layer(sc_w),
            per_layer(wbr_ret), per_layer(wbr_ssm), per_layer(wbr_sc), per_layer(w_out),
            pl.BlockSpec(final_g.shape, lambda b, j: (0, 0), pipeline_mode=pl.Buffered(1)),
        ],
        out_specs=tok(d),
        out_shape=jax.ShapeDtypeStruct((bsz, seq, d), F32),
        scratch_shapes=[
            pltpu.VMEM((RET_HEADS, RET_DIM, RET_DIM), F32),
            pltpu.VMEM((tile + SUBLANES, SSM_XBC), F32),
            pltpu.VMEM((SSM_STATE, SSM_WIDTH), F32),
            pltpu.VMEM((CHUNK, SSM_WIDTH), F32),
            pltpu.VMEM((tile + SUBLANES, width), F32),
            pltpu.VMEM((tile, RET_WIDTH), BF16),
            pltpu.VMEM((tile, SSM_WIDTH), BF16),
            pltpu.VMEM((tile, width), BF16),
            pltpu.VMEM((tile, 4 * RET_WIDTH), F32),
            pltpu.VMEM((tile, SSM_WIDTH), F32),
            pltpu.VMEM((tile, DT_PAD), F32),
            pltpu.VMEM((tile, SSM_XBC), F32),
            pltpu.VMEM((tile, 4 * width), F32),
            pltpu.VMEM((tile, 3 * d), F32),
            pltpu.VMEM((tile, d), F32),
            pltpu.VMEM((tile // CHUNK, 3, CHUNK, DT_PAD), F32),
            pltpu.VMEM((tile // CHUNK, 2, DT_PAD, CHUNK), F32),
            pltpu.VMEM((tile // CHUNK * SSM_HEADS, CHUNK, CHUNK), F32),
            pltpu.VMEM((tile // CHUNK * SSM_HEADS, CHUNK, LANES), F32),
            pltpu.VMEM((tile // CHUNK, SSM_HEADS // 2, LANES), F32),
        ],
        compiler_params=pltpu.CompilerParams(
            dimension_semantics=("parallel", "arbitrary"), vmem_limit_bytes=VMEM_LIMIT_BYTES),
        name="layer",
    )(x, norm_g3, ada5, ada5, ada5, cos2, sin2, *([w_blocks] * (W_IN_BLOCKS + 1)),
      conv_w, conv_b, dt_bias, a_log, d_skip, ssm_ng, sc_w, wbr_ret, wbr_ssm, wbr_sc, w_out, final_g)


def kernel(x, c, positions, norm_g, w_ada, b_ada, w_in, ssm_conv_w, ssm_conv_b, ssm_dt_bias, ssm_a_log,
           ssm_d, ssm_norm_g, sc_conv_w, w_br_ret, w_br_ssm, w_br_sc, w_out, final_norm_g):
    bsz, seq, d = x.shape
    depth = w_in.shape[0]
    sc_width = sc_conv_w.shape[2]
    tile = 256
    assert seq % tile == 0 and tile % CHUNK == 0

    sizes = (RET_WIDTH, RET_WIDTH, RET_WIDTH, RET_WIDTH, SSM_WIDTH, SSM_XBC, SSM_HEADS,
             sc_width, sc_width, sc_width, sc_width, 3 * d)
    offs = [0]
    for s in sizes:
        offs.append(offs[-1] + s)
    assert offs[-1] == w_in.shape[2] and d == D_MODEL and sc_width == SC_WIDTH

    ada = _ada_call(c, w_ada, b_ada)
    ada5 = ada.reshape(depth, bsz, 3, 1, d)
    cos2, sin2 = _rope_call(positions)
    norm_g3 = norm_g.reshape(depth, 1, d)

    dt_pad = DT_PAD - SSM_HEADS
    assert offs[6] == SC_BLOCK * FILL_BLOCK
    weights = (_w_in_prep_call(w_in), w_br_ret.astype(BF16), w_br_ssm.astype(BF16), w_br_sc.astype(BF16), w_out.astype(BF16))
    pad_heads = lambda v: jnp.pad(v, ((0, 0), (0, dt_pad))).reshape(depth, 1, DT_PAD)
    smalls = (
        ssm_conv_w, ssm_conv_b.reshape(depth, 1, SSM_XBC),
        pad_heads(ssm_dt_bias), pad_heads(ssm_a_log),
        jnp.repeat(ssm_d, SSM_HEAD_DIM, axis=1).reshape(depth, 1, SSM_WIDTH),
        ssm_norm_g.reshape(depth, 1, SSM_WIDTH), sc_conv_w, final_norm_g.reshape(1, d),
    )
    for layer in range(depth):
        x = _layer_call(x, ada5, norm_g3, cos2, sin2, weights, smalls, layer, tile,
                        final=(layer == depth - 1))
    return x
```

```python
import functools
import math

import jax
import jax.numpy as jnp
from jax import lax
from jax.experimental import pallas as pl
from jax.experimental.pallas import tpu as pltpu

F32 = jnp.float32
BF16 = jnp.bfloat16

EPS = 1e-6
RET_HEADS = 4
RET_DIM = 128
RET_WIDTH = RET_HEADS * RET_DIM
ROPE_THETA = 10000.0
SSM_HEADS = 16
SSM_HEAD_DIM = 64
SSM_GROUPS = 2
SSM_STATE = 128
SSM_CONV = 4
SSM_WIDTH = SSM_HEADS * SSM_HEAD_DIM
SSM_XBC = SSM_WIDTH + 2 * SSM_GROUPS * SSM_STATE
SC_CONV = 3
CHUNK = 128
LANES = 128
SUBLANES = 8
DT_PAD = LANES
FILL_BLOCK = 512
D_MODEL = 1024
SC_WIDTH = D_MODEL // 2
RET_BLOCK = 0
Z_BLOCK = RET_BLOCK + 4 * RET_WIDTH // FILL_BLOCK
XBC_BLOCK = Z_BLOCK + SSM_WIDTH // FILL_BLOCK
SC_BLOCK = XBC_BLOCK + SSM_XBC // FILL_BLOCK
GATE_BLOCK = SC_BLOCK + 4 * SC_WIDTH // FILL_BLOCK
W_IN_BLOCKS = GATE_BLOCK + 3 * D_MODEL // FILL_BLOCK
VMEM_LIMIT_BYTES = 60 * 1024 * 1024

NEG_INF = float("-inf")


def _sigmoid(x):
    return 1.0 / (1.0 + jnp.exp(-x))


def _silu(x):
    return x * _sigmoid(x)


def _dot(a, b):
    return jnp.dot(a, b, preferred_element_type=F32)


def _modulated_norm(x, g, scale, shift):
    ms = jnp.mean(x * x, axis=-1, keepdims=True)
    y = x * lax.rsqrt(ms + EPS) * g
    return (y * (1.0 + scale) + shift).astype(BF16)


def _lane_bcast(a, col):
    return jnp.broadcast_to(a[:, col:col + 1], a.shape)


def _ada_kernel(c_ref, w_ref, b_ref, o_ref):
    act = _silu(c_ref[...])
    hi = act.astype(BF16)
    lo = (act - hi.astype(F32)).astype(BF16)
    w = w_ref[...].astype(BF16)
    o_ref[...] = _dot(hi, w) + _dot(lo, w) + b_ref[...]


def _ada_call(c, w_ada, b_ada):
    depth, d, d3 = w_ada.shape
    bsz = c.shape[0]
    nblk = d3 // d
    return pl.pallas_call(
        _ada_kernel,
        grid=(depth, nblk),
        in_specs=[
            pl.BlockSpec((bsz, d), lambda l, n: (0, 0)),
            pl.BlockSpec((None, d, d), lambda l, n: (l, 0, n)),
            pl.BlockSpec((None, 1, d), lambda l, n: (l, 0, n)),
        ],
        out_specs=pl.BlockSpec((None, bsz, d), lambda l, n: (l, 0, n)),
        out_shape=jax.ShapeDtypeStruct((depth, bsz, d3), F32),
        compiler_params=pltpu.CompilerParams(
            dimension_semantics=("parallel", "parallel"), vmem_limit_bytes=VMEM_LIMIT_BYTES),
        name="ada",
    )(c, w_ada, b_ada.reshape(depth, 1, d3))


def _w_in_prep_kernel(wt_ref, o_ref):
    o_ref[...] = wt_ref[0].T.astype(o_ref.dtype)


def _w_in_prep_call(w_in):
    depth, d, width = w_in.shape
    assert width == W_IN_BLOCKS * FILL_BLOCK + SSM_HEADS
    w_t = jnp.swapaxes(w_in, 1, 2)
    dt_start = SC_BLOCK * FILL_BLOCK
    per_block = FILL_BLOCK // SSM_HEADS
    start = lambda n: SSM_HEADS * jnp.where(n == W_IN_BLOCKS, dt_start // SSM_HEADS,
                                            n * per_block + jnp.where(n >= SC_BLOCK, 1, 0))
    return pl.pallas_call(
        _w_in_prep_kernel,
        grid=(depth, W_IN_BLOCKS + 1),
        in_specs=[pl.BlockSpec((pl.Element(1), pl.Element(FILL_BLOCK), pl.Element(d)),
                               lambda l, n: (l, start(n), 0))],
        out_specs=pl.BlockSpec((None, d, FILL_BLOCK), lambda l, n: (l, 0, n)),
        out_shape=jax.ShapeDtypeStruct((depth, d, (W_IN_BLOCKS + 1) * FILL_BLOCK), BF16),
        compiler_params=pltpu.CompilerParams(
            dimension_semantics=("parallel", "parallel"), vmem_limit_bytes=VMEM_LIMIT_BYTES),
        name="w_in_prep",
    )(w_t)


def _rope_kernel(pos_ref, inv_ref, cos_ref, sin_ref):
    ang = pos_ref[...].astype(F32) * inv_ref[...]
    lane = lax.broadcasted_iota(jnp.int32, ang.shape, 1)
    cos_ref[...] = jnp.cos(ang)
    sin_ref[...] = jnp.where(lane < RET_DIM // 2, -jnp.sin(ang), jnp.sin(ang))


def _rope_call(positions):
    bsz, seq = positions.shape
    half = RET_DIM // 2
    inv_freq = ROPE_THETA ** (-jnp.arange(0, RET_DIM, 2, dtype=F32) / RET_DIM)
    inv2 = jnp.concatenate([inv_freq, inv_freq]).reshape(1, 2 * half)
    tbl = jax.ShapeDtypeStruct((bsz, seq, RET_DIM), F32)
    return pl.pallas_call(
        _rope_kernel,
        grid=(bsz,),
        in_specs=[
            pl.BlockSpec((None, seq, 1), lambda b: (b, 0, 0)),
            pl.BlockSpec((1, RET_DIM), lambda b: (0, 0)),
        ],
        out_specs=[pl.BlockSpec((None, seq, RET_DIM), lambda b: (b, 0, 0))] * 2,
        out_shape=[tbl, tbl],
        compiler_params=pltpu.CompilerParams(
            dimension_semantics=("parallel",), vmem_limit_bytes=VMEM_LIMIT_BYTES),
        name="rope_tables",
    )(positions.reshape(bsz, seq, 1), inv2)


def _run_fillers(fillers, slots_left):
    n = -(-len(fillers) // max(slots_left, 1))
    for _ in range(min(n, len(fillers))):
        fillers.pop(0)()


def _proj_block(dst_ref, h, w_ref, dst_lo, act=None):
    def run():
        r = _dot(h, w_ref[...])
        dst_ref[:, dst_lo:dst_lo + w_ref.shape[1]] = r if act is None else act(r)
    return run


def _retention_unit(hd, c, qkvg_ref, cos_ref, sin_ref, state_ref, y_ref):
    def run():
        row = lax.broadcasted_iota(jnp.int32, (CHUNK, CHUNK), 0)
        col = lax.broadcasted_iota(jnp.int32, (CHUNK, CHUNK), 1)
        rowf = row.astype(F32)
        half = RET_DIM // 2
        log_g = math.log(1.0 - 2.0 ** (-5.0 - hd))
        decay_mask = jnp.exp(jnp.where(row >= col, (row - col).astype(F32) * log_g, NEG_INF))
        q_decay = jnp.exp((rowf + 1.0) * log_g)
        k_decay = jnp.exp((CHUNK - 1.0 - rowf) * log_g)
        chunk_decay = math.exp(CHUNK * log_g)
        rows = slice(c * CHUNK, (c + 1) * CHUNK)
        lo = hd * RET_DIM
        cos = cos_ref[rows, :]
        sin = sin_ref[rows, :]

        def rot(t):
            return t * cos + pltpu.roll(t, half, axis=1) * sin

        qc = (rot(qkvg_ref[rows, lo:lo + RET_DIM]) * (RET_DIM ** -0.5)).astype(BF16)
        kc = rot(qkvg_ref[rows, RET_WIDTH + lo:RET_WIDTH + lo + RET_DIM])
        vc = qkvg_ref[rows, 2 * RET_WIDTH + lo:2 * RET_WIDTH + lo + RET_DIM].astype(BF16)
        gate = qkvg_ref[rows, 3 * RET_WIDTH + lo:3 * RET_WIDTH + lo + RET_DIM]
        state = state_ref[hd]
        scores = lax.dot_general(qc, kc.astype(BF16), (((1,), (1,)), ((), ())), preferred_element_type=F32)
        inner = _dot((scores * decay_mask).astype(BF16), vc)
        cross = _dot(qc, state.astype(BF16)) * q_decay
        kd = (kc * k_decay).astype(BF16)
        state_ref[hd] = chunk_decay * state + lax.dot_general(kd, vc, (((0,), (0,)), ((), ())),
                                                              preferred_element_type=F32)
        o = inner + cross
        mu = jnp.mean(o, axis=-1, keepdims=True)
        oc = o - mu
        var = jnp.mean(oc * oc, axis=-1, keepdims=True)
        y = oc * lax.rsqrt(var + EPS) * _silu(gate)
        y_ref[rows, lo:lo + RET_DIM] = y.astype(y_ref.dtype)
    return run


def _block_diag_pair(slab):
    lane = lax.broadcasted_iota(jnp.int32, slab.shape, 1)
    first = lane < SSM_HEAD_DIM
    top = jnp.where(first, slab, 0.0)
    bot = jnp.where(first, 0.0, slab)
    return jnp.concatenate([top, bot], axis=0).astype(BF16)


def _ssd_conv(xbc_ref, xact_ref, convw_ref, convb_ref, tile, fillers):
    nblk = SSM_XBC // FILL_BLOCK
    for n in range(nblk):
        cols = slice(n * FILL_BLOCK, (n + 1) * FILL_BLOCK)
        conv = convb_ref[:, cols]
        for kk in range(SSM_CONV):
            off = SUBLANES - (SSM_CONV - 1) + kk
            conv = conv + convw_ref[kk:kk + 1, cols] * xbc_ref[off:off + tile, cols]
        xact_ref[:, cols] = _silu(conv)
        _run_fillers(fillers, nblk - n)
    xbc_ref[0:SUBLANES, :] = xbc_ref[tile:tile + SUBLANES, :]


T_ACS, T_W, T_EACS = 0, 1, 2


def _cumsum_rows(x):
    row = lax.broadcasted_iota(jnp.int32, x.shape, 0)
    k = 1
    while k < x.shape[0]:
        x = x + jnp.where(row >= k, pltpu.roll(x, k, axis=0), 0.0)
        k *= 2
    return x


def _ssd_terms_task(c, dt_ref, dtb_ref, alog_ref, term_ref, trans_ref):
    def run():
        rows = slice(c * CHUNK, (c + 1) * CHUNK)
        dt_in = dt_ref[rows, :] + dtb_ref[...]
        dt_c = jnp.maximum(dt_in, 0.0) + jnp.log1p(jnp.exp(-jnp.abs(dt_in)))
        a_cs = _cumsum_rows(dt_c * (-jnp.exp(alog_ref[...])))
        a_last = a_cs[CHUNK - 1:CHUNK, :]
        term_ref[c, T_ACS] = a_cs
        term_ref[c, T_W] = jnp.exp(a_last - a_cs) * dt_c
        term_ref[c, T_EACS] = jnp.exp(a_cs)
        trans_ref[c, 0] = a_cs.T
        trans_ref[c, 1] = dt_c.T
    return run


def _ssd_head_task(c, heads, term_ref, trans_ref, ld_ref, exe_ref, exw_ref):
    def run():
        row = lax.broadcasted_iota(jnp.int32, (CHUNK, CHUNK), 0)
        col = lax.broadcasted_iota(jnp.int32, (CHUNK, CHUNK), 1)
        tril = row >= col
        first = lax.broadcasted_iota(jnp.int32, (CHUNK, LANES), 1) < SSM_HEAD_DIM
        a_cs = term_ref[c, T_ACS]
        for hh in heads:
            seg = _lane_bcast(a_cs, hh) - trans_ref[c, 0, hh:hh + 1, :]
            ld_ref[c * SSM_HEADS + hh] = jnp.exp(jnp.where(tril, seg, NEG_INF)) * trans_ref[c, 1, hh:hh + 1, :]
        for h1 in heads[::2]:
            lo = h1 * SSM_HEAD_DIM
            for src, dst_ref in ((T_EACS, exe_ref), (T_W, exw_ref)):
                t = term_ref[c, src]
                dst_ref[c, :, lo:lo + 2 * SSM_HEAD_DIM] = jnp.where(first, _lane_bcast(t, h1), _lane_bcast(t, h1 + 1))
    return run


def _ssd_scan(xact_ref, z_ref, dskip_ref, ng_ref, ld_ref, exe_ref, exw_ref,
              state_ref, ychunk_ref, y_ref, tile, fillers):
    pair = 2 * SSM_HEAD_DIM
    heads_per_group = SSM_HEADS // SSM_GROUPS
    group_width = heads_per_group * SSM_HEAD_DIM
    n_chunks = tile // CHUNK
    slots = n_chunks * SSM_HEADS // 2

    for c in range(n_chunks):
        rows = slice(c * CHUNK, (c + 1) * CHUNK)
        for g in range(SSM_GROUPS):
            b_lo = SSM_WIDTH + g * SSM_STATE
            c_lo = SSM_WIDTH + SSM_GROUPS * SSM_STATE + g * SSM_STATE
            gcols = slice(g * group_width, (g + 1) * group_width)
            c_g = xact_ref[rows, c_lo:c_lo + SSM_STATE].astype(BF16)
            b_t = xact_ref[rows, b_lo:b_lo + SSM_STATE].T.astype(BF16)
            xs_g = xact_ref[rows, gcols]
            s_prev = state_ref[:, gcols]
            y_off = _dot(c_g, s_prev.astype(BF16)) * exe_ref[c, :, gcols]
            s_new = _dot(b_t, (xs_g * exw_ref[c, :, gcols]).astype(BF16))
            state_ref[:, gcols] = exe_ref[c, CHUNK - 1:CHUNK, gcols] * s_prev + s_new
            base = y_off + dskip_ref[:, gcols] * xs_g
            cb = _dot(c_g, b_t)
            for p in range(heads_per_group // 2):
                h1 = g * heads_per_group + 2 * p
                lo = h1 * SSM_HEAD_DIM
                rhs = _block_diag_pair(xs_g[:, p * pair:(p + 1) * pair])
                m = jnp.concatenate([cb * ld_ref[c * SSM_HEADS + hh] for hh in (h1, h1 + 1)], axis=1)
                ychunk_ref[:, lo:lo + pair] = base[:, p * pair:(p + 1) * pair] + _dot(m.astype(BF16), rhs)
                _run_fillers(fillers, slots)
                slots -= 1
        y = ychunk_ref[...] * _silu(z_ref[rows, :])
        ms = jnp.mean(y * y, axis=-1, keepdims=True)
        y_ref[rows, :] = (y * lax.rsqrt(ms + EPS) * ng_ref[...]).astype(y_ref.dtype)


def _short_conv_task(cp_ref, scw_ref, u_ref, y_ref, tile):
    def run():
        width = u_ref.shape[1]
        u_ref[SUBLANES:SUBLANES + tile, :] = cp_ref[:, width:2 * width] * cp_ref[:, 2 * width:3 * width]
        conv = jnp.zeros((tile, width), F32)
        for kk in range(SC_CONV):
            off = SUBLANES - (SC_CONV - 1) + kk
            conv = conv + scw_ref[kk:kk + 1, :] * u_ref[off:off + tile, :]
        u_ref[0:SUBLANES, :] = u_ref[tile:tile + SUBLANES, :]
        y_ref[...] = (cp_ref[:, :width] * conv * _silu(cp_ref[:, 3 * width:])).astype(y_ref.dtype)
    return run


def _layer_kernel(x_ref, g_ref, scale_ref, shift_ref, rgate_ref, cos_ref, sin_ref, *refs, tile, final):
    win = refs[:W_IN_BLOCKS]
    (wdt_ref, convw_ref, convb_ref, dtb_ref, alog_ref, dskip_ref, sng_ref, scw_ref,
     wbr_ret_ref, wbr_ssm_ref, wbr_sc_ref, wout_ref, fg_ref,
     o_ref,
     rstate_ref, xbc_ref, sstate_ref, ychunk_ref, u_ref, yret_ref, yssm_ref, ysc_ref,
     qkvg_ref, z_ref, dt_ref, xact_ref, cp_ref, gates_ref, acc_ref,
     term_ref, trans_ref, ld_ref, exe_ref, exw_ref) = refs[W_IN_BLOCKS:]

    @pl.when(pl.program_id(1) == 0)
    def _():
        rstate_ref[...] = jnp.zeros_like(rstate_ref)
        sstate_ref[...] = jnp.zeros_like(sstate_ref)
        xbc_ref[0:SUBLANES, :] = jnp.zeros((SUBLANES, xbc_ref.shape[1]), F32)
        u_ref[0:SUBLANES, :] = jnp.zeros((SUBLANES, u_ref.shape[1]), F32)

    x = x_ref[...]
    d = x.shape[1]
    h = _modulated_norm(x, g_ref[...], scale_ref[...], shift_ref[...])
    blk = FILL_BLOCK

    def proj(dst_ref, first_block, n, act=None):
        return _proj_block(dst_ref, h, win[first_block + n], n * blk, act)

    for n in range(SSM_XBC // blk):
        xbc_ref[SUBLANES:SUBLANES + tile, n * blk:(n + 1) * blk] = _dot(h, win[XBC_BLOCK + n][...])
    n_chunks = tile // CHUNK
    fillers = [_proj_block(dt_ref, h, wdt_ref, 0)]
    fillers += [_ssd_terms_task(c, dt_ref, dtb_ref, alog_ref, term_ref, trans_ref) for c in range(n_chunks)]
    ret_proj = [proj(qkvg_ref, RET_BLOCK, n) for n in range(4 * RET_WIDTH // blk)]
    quad = 4
    head_tasks = [_ssd_head_task(c, list(range(q, q + quad)), term_ref, trans_ref, ld_ref, exe_ref, exw_ref)
                  for c in range(n_chunks) for q in range(0, SSM_HEADS, quad)]
    fillers += [t for duo in zip(ret_proj, head_tasks) for t in duo] + head_tasks[len(ret_proj):]
    _ssd_conv(xbc_ref, xact_ref, convw_ref, convb_ref, tile, fillers)

    def ret_merge():
        acc_ref[...] = gates_ref[:, :d] * _dot(yret_ref[...], wbr_ret_ref[...])

    def sc_merge():
        acc_ref[...] += gates_ref[:, 2 * d:] * _dot(ysc_ref[...], wbr_sc_ref[...])

    dense = [proj(z_ref, Z_BLOCK, n) for n in range(SSM_WIDTH // blk)]
    dense += [proj(cp_ref, SC_BLOCK, n) for n in range(cp_ref.shape[1] // blk)]
    dense += [proj(gates_ref, GATE_BLOCK, n, _sigmoid) for n in range(3 * d // blk)]
    ret_units = [_retention_unit(hd, c, qkvg_ref, cos_ref, sin_ref, rstate_ref, yret_ref)
                 for c in range(tile // CHUNK) for hd in range(RET_HEADS)]
    fillers = [t for duo in zip(dense, ret_units) for t in duo] + dense[len(ret_units):]
    fillers += [ret_merge, _short_conv_task(cp_ref, scw_ref, u_ref, ysc_ref, tile), sc_merge]
    _ssd_scan(xact_ref, z_ref, dskip_ref, sng_ref, ld_ref, exe_ref, exw_ref,
              sstate_ref, ychunk_ref, yssm_ref, tile, fillers)
    _run_fillers(fillers, 1)

    merged = acc_ref[...] + gates_ref[:, d:2 * d] * _dot(yssm_ref[...], wbr_ssm_ref[...])
    out = _dot(merged.astype(BF16), wout_ref[...])
    xn = x + rgate_ref[...] * out
    if final:
        ms = jnp.mean(xn * xn, axis=-1, keepdims=True)
        xn = xn * lax.rsqrt(ms + EPS) * fg_ref[...]
    o_ref[...] = xn


def _layer_call(x, ada5, norm_g3, cos2, sin2, weights, smalls, layer, tile, final):
    bsz, seq, d = x.shape
    row_spec = lambda idx: pl.BlockSpec((None, None, None, 1, d), lambda b, j: (layer, b, idx, 0, 0))
    per_layer = lambda a: pl.BlockSpec((None,) + a.shape[1:], lambda b, j: (layer,) + (0,) * (a.ndim - 1),
                                       pipeline_mode=pl.Buffered(1))
    tok = lambda w: pl.BlockSpec((None, tile, w), lambda b, j: (b, j, 0))
    w_blocks, wbr_ret, wbr_ssm, wbr_sc, w_out = weights
    conv_w, conv_b, dt_bias, a_log, d_skip, ssm_ng, sc_w, final_g = smalls
    width = sc_w.shape[2]
    w_in_block = lambda cols, n: pl.BlockSpec((None, d, cols), lambda b, j: (layer, 0, n),
                                              pipeline_mode=pl.Buffered(1))
    dt_block = W_IN_BLOCKS * FILL_BLOCK // DT_PAD
    return pl.pallas_call(
        functools.partial(_layer_kernel, tile=tile, final=final),
        grid=(bsz, seq // tile),
        in_specs=[
            tok(d),
            per_layer(norm_g3),
            row_spec(1),
            row_spec(0),
            row_spec(2),
            tok(RET_DIM), tok(RET_DIM),
        ] + [w_in_block(FILL_BLOCK, n) for n in range(W_IN_BLOCKS)] + [
            w_in_block(DT_PAD, dt_block),
            per_layer(conv_w), per_layer(conv_b), per_layer(dt_bias), per_layer(a_log), per_layer(d_skip),
            per_layer(ssm_ng), per_layer(sc_w),
            per_layer(wbr_ret), per_layer(wbr_ssm), per_layer(wbr_sc), per_layer(w_out),
            pl.BlockSpec(final_g.shape, lambda b, j: (0, 0), pipeline_mode=pl.Buffered(1)),
        ],
        out_specs=tok(d),
        out_shape=jax.ShapeDtypeStruct((bsz, seq, d), F32),
        scratch_shapes=[
            pltpu.VMEM((RET_HEADS, RET_DIM, RET_DIM), F32),
            pltpu.VMEM((tile + SUBLANES, SSM_XBC), F32),
            pltpu.VMEM((SSM_STATE, SSM_WIDTH), F32),
            pltpu.VMEM((CHUNK, SSM_WIDTH), F32),
            pltpu.VMEM((tile + SUBLANES, width), F32),
            pltpu.VMEM((tile, RET_WIDTH), BF16),
            pltpu.VMEM((tile, SSM_WIDTH), BF16),
            pltpu.VMEM((tile, width), BF16),
            pltpu.VMEM((tile, 4 * RET_WIDTH), F32),
            pltpu.VMEM((tile, SSM_WIDTH), F32),
            pltpu.VMEM((tile, DT_PAD), F32),
            pltpu.VMEM((tile, SSM_XBC), F32),
            pltpu.VMEM((tile, 4 * width), F32),
            pltpu.VMEM((tile, 3 * d), F32),
            pltpu.VMEM((tile, d), F32),
            pltpu.VMEM((tile // CHUNK, 3, CHUNK, DT_PAD), F32),
            pltpu.VMEM((tile // CHUNK, 2, DT_PAD, CHUNK), F32),
            pltpu.VMEM((tile // CHUNK * SSM_HEADS, CHUNK, CHUNK), F32),
            pltpu.VMEM((tile // CHUNK, CHUNK, SSM_WIDTH), F32),
            pltpu.VMEM((tile // CHUNK, CHUNK, SSM_WIDTH), F32),
        ],
        compiler_params=pltpu.CompilerParams(
            dimension_semantics=("parallel", "arbitrary"), vmem_limit_bytes=VMEM_LIMIT_BYTES),
        name="layer",
    )(x, norm_g3, ada5, ada5, ada5, cos2, sin2, *([w_blocks] * (W_IN_BLOCKS + 1)),
      conv_w, conv_b, dt_bias, a_log, d_skip, ssm_ng, sc_w, wbr_ret, wbr_ssm, wbr_sc, w_out, final_g)


def kernel(x, c, positions, norm_g, w_ada, b_ada, w_in, ssm_conv_w, ssm_conv_b, ssm_dt_bias, ssm_a_log,
           ssm_d, ssm_norm_g, sc_conv_w, w_br_ret, w_br_ssm, w_br_sc, w_out, final_norm_g):
    bsz, seq, d = x.shape
    depth = w_in.shape[0]
    sc_width = sc_conv_w.shape[2]
    tile = 256
    assert seq % tile == 0 and tile % CHUNK == 0

    sizes = (RET_WIDTH, RET_WIDTH, RET_WIDTH, RET_WIDTH, SSM_WIDTH, SSM_XBC, SSM_HEADS,
             sc_width, sc_width, sc_width, sc_width, 3 * d)
    offs = [0]
    for s in sizes:
        offs.append(offs[-1] + s)
    assert offs[-1] == w_in.shape[2] and d == D_MODEL and sc_width == SC_WIDTH

    ada = _ada_call(c, w_ada, b_ada)
    ada5 = ada.reshape(depth, bsz, 3, 1, d)
    cos2, sin2 = _rope_call(positions)
    norm_g3 = norm_g.reshape(depth, 1, d)

    dt_pad = DT_PAD - SSM_HEADS
    assert offs[6] == SC_BLOCK * FILL_BLOCK
    weights = (_w_in_prep_call(w_in), w_br_ret.astype(BF16), w_br_ssm.astype(BF16), w_br_sc.astype(BF16), w_out.astype(BF16))
    pad_heads = lambda v: jnp.pad(v, ((0, 0), (0, dt_pad))).reshape(depth, 1, DT_PAD)
    smalls = (
        ssm_conv_w, ssm_conv_b.reshape(depth, 1, SSM_XBC),
        pad_heads(ssm_dt_bias), pad_heads(ssm_a_log),
        jnp.repeat(ssm_d, SSM_HEAD_DIM, axis=1).reshape(depth, 1, SSM_WIDTH),
        ssm_norm_g.reshape(depth, 1, SSM_WIDTH), sc_conv_w, final_norm_g.reshape(1, d),
    )
    for layer in range(depth):
        x = _layer_call(x, ada5, norm_g3, cos2, sin2, weights, smalls, layer, tile,
                        final=(layer == depth - 1))
    return x
```

```python
import functools
import math

import jax
import jax.numpy as jnp
from jax import lax
from jax.experimental import pallas as pl
from jax.experimental.pallas import tpu as pltpu

F32 = jnp.float32
BF16 = jnp.bfloat16

EPS = 1e-6
RET_HEADS = 4
RET_DIM = 128
RET_WIDTH = RET_HEADS * RET_DIM
ROPE_THETA = 10000.0
SSM_HEADS = 16
SSM_HEAD_DIM = 64
SSM_GROUPS = 2
SSM_STATE = 128
SSM_CONV = 4
SSM_WIDTH = SSM_HEADS * SSM_HEAD_DIM
SSM_XBC = SSM_WIDTH + 2 * SSM_GROUPS * SSM_STATE
SC_CONV = 3
CHUNK = 128
LANES = 128
SUBLANES = 8
DT_PAD = LANES
FILL_BLOCK = 512
D_MODEL = 1024
SC_WIDTH = D_MODEL // 2
RET_BLOCK = 0
Z_BLOCK = RET_BLOCK + 4 * RET_WIDTH // FILL_BLOCK
XBC_BLOCK = Z_BLOCK + SSM_WIDTH // FILL_BLOCK
SC_BLOCK = XBC_BLOCK + SSM_XBC // FILL_BLOCK
GATE_BLOCK = SC_BLOCK + 4 * SC_WIDTH // FILL_BLOCK
W_IN_BLOCKS = GATE_BLOCK + 3 * D_MODEL // FILL_BLOCK
VMEM_LIMIT_BYTES = 60 * 1024 * 1024

NEG_INF = float("-inf")


def _sigmoid(x):
    return 1.0 / (1.0 + jnp.exp(-x))


def _silu(x):
    return x * _sigmoid(x)


def _dot(a, b):
    return jnp.dot(a, b, preferred_element_type=F32)


def _modulated_norm(x, g, scale, shift):
    ms = jnp.mean(x * x, axis=-1, keepdims=True)
    y = x * lax.rsqrt(ms + EPS) * g
    return (y * (1.0 + scale) + shift).astype(BF16)


def _lane_bcast(a, col):
    return jnp.broadcast_to(a[:, col:col + 1], a.shape)


def _ada_kernel(c_ref, w_ref, b_ref, o_ref):
    act = _silu(c_ref[...])
    hi = act.astype(BF16)
    lo = (act - hi.astype(F32)).astype(BF16)
    w = w_ref[...].astype(BF16)
    o_ref[...] = _dot(hi, w) + _dot(lo, w) + b_ref[...]


def _ada_call(c, w_ada, b_ada):
    depth, d, d3 = w_ada.shape
    bsz = c.shape[0]
    nblk = d3 // d
    return pl.pallas_call(
        _ada_kernel,
        grid=(depth, nblk),
        in_specs=[
            pl.BlockSpec((bsz, d), lambda l, n: (0, 0)),
            pl.BlockSpec((None, d, d), lambda l, n: (l, 0, n)),
            pl.BlockSpec((None, 1, d), lambda l, n: (l, 0, n)),
        ],
        out_specs=pl.BlockSpec((None, bsz, d), lambda l, n: (l, 0, n)),
        out_shape=jax.ShapeDtypeStruct((depth, bsz, d3), F32),
        compiler_params=pltpu.CompilerParams(
            dimension_semantics=("parallel", "parallel"), vmem_limit_bytes=VMEM_LIMIT_BYTES),
        name="ada",
    )(c, w_ada, b_ada.reshape(depth, 1, d3))


def _w_in_prep_kernel(wt_ref, o_ref):
    o_ref[...] = wt_ref[0].T.astype(o_ref.dtype)


def _w_in_prep_call(w_in):
    depth, d, width = w_in.shape
    assert width == W_IN_BLOCKS * FILL_BLOCK + SSM_HEADS
    w_t = jnp.swapaxes(w_in, 1, 2)
    dt_start = SC_BLOCK * FILL_BLOCK
    per_block = FILL_BLOCK // SSM_HEADS
    start = lambda n: SSM_HEADS * jnp.where(n == W_IN_BLOCKS, dt_start // SSM_HEADS,
                                            n * per_block + jnp.where(n >= SC_BLOCK, 1, 0))
    return pl.pallas_call(
        _w_in_prep_kernel,
        grid=(depth, W_IN_BLOCKS + 1),
        in_specs=[pl.BlockSpec((pl.Element(1), pl.Element(FILL_BLOCK), pl.Element(d)),
                               lambda l, n: (l, start(n), 0))],
        out_specs=pl.BlockSpec((None, d, FILL_BLOCK), lambda l, n: (l, 0, n)),
        out_shape=jax.ShapeDtypeStruct((depth, d, (W_IN_BLOCKS + 1) * FILL_BLOCK), BF16),
        compiler_params=pltpu.CompilerParams(
            dimension_semantics=("parallel", "parallel"), vmem_limit_bytes=VMEM_LIMIT_BYTES),
        name="w_in_prep",
    )(w_t)


def _rope_kernel(pos_ref, inv_ref, cos_ref, sin_ref):
    ang = pos_ref[...].astype(F32) * inv_ref[...]
    lane = lax.broadcasted_iota(jnp.int32, ang.shape, 1)
    cos_ref[...] = jnp.cos(ang)
    sin_ref[...] = jnp.where(lane < RET_DIM // 2, -jnp.sin(ang), jnp.sin(ang))


def _rope_call(positions):
    bsz, seq = positions.shape
    half = RET_DIM // 2
    inv_freq = ROPE_THETA ** (-jnp.arange(0, RET_DIM, 2, dtype=F32) / RET_DIM)
    inv2 = jnp.concatenate([inv_freq, inv_freq]).reshape(1, 2 * half)
    tbl = jax.ShapeDtypeStruct((bsz, seq, RET_DIM), F32)
    return pl.pallas_call(
        _rope_kernel,
        grid=(bsz,),
        in_specs=[
            pl.BlockSpec((None, seq, 1), lambda b: (b, 0, 0)),
            pl.BlockSpec((1, RET_DIM), lambda b: (0, 0)),
        ],
        out_specs=[pl.BlockSpec((None, seq, RET_DIM), lambda b: (b, 0, 0))] * 2,
        out_shape=[tbl, tbl],
        compiler_params=pltpu.CompilerParams(
            dimension_semantics=("parallel",), vmem_limit_bytes=VMEM_LIMIT_BYTES),
        name="rope_tables",
    )(positions.reshape(bsz, seq, 1), inv2)


def _run_fillers(fillers, slots_left):
    n = -(-len(fillers) // max(slots_left, 1))
    for _ in range(min(n, len(fillers))):
        fillers.pop(0)()


def _proj_block(dst_ref, h, w_ref, dst_lo, act=None):
    def run():
        r = _dot(h, w_ref[...])
        dst_ref[:, dst_lo:dst_lo + w_ref.shape[1]] = r if act is None else act(r)
    return run


def _retention_unit(hd, c, qkvg_ref, cos_ref, sin_ref, state_ref, y_ref):
    def run():
        row = lax.broadcasted_iota(jnp.int32, (CHUNK, CHUNK), 0)
        col = lax.broadcasted_iota(jnp.int32, (CHUNK, CHUNK), 1)
        rowf = row.astype(F32)
        half = RET_DIM // 2
        log_g = math.log(1.0 - 2.0 ** (-5.0 - hd))
        decay_mask = jnp.exp(jnp.where(row >= col, (row - col).astype(F32) * log_g, NEG_INF))
        q_decay = jnp.exp((rowf + 1.0) * log_g)
        k_decay = jnp.exp((CHUNK - 1.0 - rowf) * log_g)
        chunk_decay = math.exp(CHUNK * log_g)
        rows = slice(c * CHUNK, (c + 1) * CHUNK)
        lo = hd * RET_DIM
        cos = cos_ref[rows, :]
        sin = sin_ref[rows, :]

        def rot(t):
            return t * cos + pltpu.roll(t, half, axis=1) * sin

        qc = (rot(qkvg_ref[rows, lo:lo + RET_DIM]) * (RET_DIM ** -0.5)).astype(BF16)
        kc = rot(qkvg_ref[rows, RET_WIDTH + lo:RET_WIDTH + lo + RET_DIM])
        vc = qkvg_ref[rows, 2 * RET_WIDTH + lo:2 * RET_WIDTH + lo + RET_DIM].astype(BF16)
        gate = qkvg_ref[rows, 3 * RET_WIDTH + lo:3 * RET_WIDTH + lo + RET_DIM]
        state = state_ref[hd]
        scores = lax.dot_general(qc, kc.astype(BF16), (((1,), (1,)), ((), ())), preferred_element_type=F32)
        inner = _dot((scores * decay_mask).astype(BF16), vc)
        cross = _dot(qc, state.astype(BF16)) * q_decay
        kd = (kc * k_decay).astype(BF16)
        state_ref[hd] = chunk_decay * state + lax.dot_general(kd, vc, (((0,), (0,)), ((), ())),
                                                              preferred_element_type=F32)
        o = inner + cross
        mu = jnp.mean(o, axis=-1, keepdims=True)
        oc = o - mu
        var = jnp.mean(oc * oc, axis=-1, keepdims=True)
        y = oc * lax.rsqrt(var + EPS) * _silu(gate)
        y_ref[rows, lo:lo + RET_DIM] = y.astype(y_ref.dtype)
    return run


def _block_diag_pair(slab):
    lane = lax.broadcasted_iota(jnp.int32, slab.shape, 1)
    first = lane < SSM_HEAD_DIM
    top = jnp.where(first, slab, 0.0)
    bot = jnp.where(first, 0.0, slab)
    return jnp.concatenate([top, bot], axis=0).astype(BF16)


def _ssd_conv(xbc_ref, xact_ref, convw_ref, convb_ref, tile, fillers):
    nblk = SSM_XBC // FILL_BLOCK
    for n in range(nblk):
        cols = slice(n * FILL_BLOCK, (n + 1) * FILL_BLOCK)
        conv = convb_ref[:, cols]
        for kk in range(SSM_CONV):
            off = SUBLANES - (SSM_CONV - 1) + kk
            conv = conv + convw_ref[kk:kk + 1, cols] * xbc_ref[off:off + tile, cols]
        xact_ref[:, cols] = _silu(conv)
        _run_fillers(fillers, nblk - n)
    xbc_ref[0:SUBLANES, :] = xbc_ref[tile:tile + SUBLANES, :]


T_ACS, T_WT, T_EACS = 0, 1, 2


def _cumsum_rows(x):
    row = lax.broadcasted_iota(jnp.int32, x.shape, 0)
    k = 1
    while k < x.shape[0]:
        x = x + jnp.where(row >= k, pltpu.roll(x, k, axis=0), 0.0)
        k *= 2
    return x


def _ssd_terms_task(c, dt_ref, dtb_ref, alog_ref, term_ref, trans_ref):
    def run():
        rows = slice(c * CHUNK, (c + 1) * CHUNK)
        dt_in = dt_ref[rows, :] + dtb_ref[...]
        dt_c = jnp.maximum(dt_in, 0.0) + jnp.log1p(jnp.exp(-jnp.abs(dt_in)))
        a_cs = _cumsum_rows(dt_c * (-jnp.exp(alog_ref[...])))
        a_last = a_cs[CHUNK - 1:CHUNK, :]
        term_ref[c, T_ACS] = a_cs
        term_ref[c, T_WT] = (jnp.exp(a_last - a_cs) * dt_c).T
        term_ref[c, T_EACS] = jnp.exp(a_cs)
        trans_ref[c, 0] = a_cs.T
        trans_ref[c, 1] = dt_c.T
    return run


def _ssd_head_task(c, heads, term_ref, trans_ref, ld_ref, eb_ref, cd_ref):
    def run():
        row = lax.broadcasted_iota(jnp.int32, (CHUNK, CHUNK), 0)
        col = lax.broadcasted_iota(jnp.int32, (CHUNK, CHUNK), 1)
        tril = row >= col
        lane1 = lax.broadcasted_iota(jnp.int32, (1, LANES), 1)
        a_cs = term_ref[c, T_ACS]
        e_acs = term_ref[c, T_EACS]
        cdec = e_acs[CHUNK - 1:CHUNK, :]
        for hh in heads:
            seg = _lane_bcast(a_cs, hh) - trans_ref[c, 0, hh:hh + 1, :]
            ld_ref[c * SSM_HEADS + hh] = jnp.exp(jnp.where(tril, seg, NEG_INF)) * trans_ref[c, 1, hh:hh + 1, :]
            eb_ref[c * SSM_HEADS + hh] = _lane_bcast(e_acs, hh)
        for h1 in heads[::2]:
            cd_ref[c, h1 // 2:h1 // 2 + 1, :] = jnp.where(
                lane1 < SSM_HEAD_DIM, _lane_bcast(cdec, h1), _lane_bcast(cdec, h1 + 1))
    return run


def _ssd_scan(xact_ref, z_ref, dskip_ref, ng_ref, term_ref, ld_ref, eb_ref, cd_ref,
              state_ref, ychunk_ref, y_ref, tile, fillers):
    pair = 2 * SSM_HEAD_DIM
    heads_per_group = SSM_HEADS // SSM_GROUPS
    n_chunks = tile // CHUNK
    slots = n_chunks * SSM_HEADS // 2

    for c in range(n_chunks):
        rows = slice(c * CHUNK, (c + 1) * CHUNK)
        for g in range(SSM_GROUPS):
            b_lo = SSM_WIDTH + g * SSM_STATE
            c_lo = SSM_WIDTH + SSM_GROUPS * SSM_STATE + g * SSM_STATE
            b_g = xact_ref[rows, b_lo:b_lo + SSM_STATE]
            c_g = xact_ref[rows, c_lo:c_lo + SSM_STATE]
            b_t = b_g.T
            cb = _dot(c_g.astype(BF16), b_t.astype(BF16))
            for p in range(heads_per_group // 2):
                h1 = g * heads_per_group + 2 * p
                lo = h1 * SSM_HEAD_DIM
                xs_pair = xact_ref[rows, lo:lo + pair]
                rhs = _block_diag_pair(xs_pair)
                m_parts, s_parts, o_parts = [], [], []
                for hh in (h1, h1 + 1):
                    m_parts.append(cb * ld_ref[c * SSM_HEADS + hh])
                    s_parts.append(b_t * term_ref[c, T_WT, hh:hh + 1, :])
                    o_parts.append(c_g * eb_ref[c * SSM_HEADS + hh])
                y_diag = _dot(jnp.concatenate(m_parts, axis=1).astype(BF16), rhs)
                s_new = _dot(jnp.concatenate(s_parts, axis=1).astype(BF16), rhs)
                s_prev = state_ref[:, lo:lo + pair]
                y_off = _dot(jnp.concatenate(o_parts, axis=1).astype(BF16), _block_diag_pair(s_prev))
                state_ref[:, lo:lo + pair] = cd_ref[c, h1 // 2:h1 // 2 + 1, :] * s_prev + s_new
                ychunk_ref[:, lo:lo + pair] = y_diag + y_off + dskip_ref[:, lo:lo + pair] * xs_pair
                _run_fillers(fillers, slots)
                slots -= 1
        y = ychunk_ref[...] * _silu(z_ref[rows, :])
        ms = jnp.mean(y * y, axis=-1, keepdims=True)
        y_ref[rows, :] = (y * lax.rsqrt(ms + EPS) * ng_ref[...]).astype(y_ref.dtype)


def _short_conv_task(cp_ref, scw_ref, u_ref, y_ref, tile):
    def run():
        width = u_ref.shape[1]
        u_ref[SUBLANES:SUBLANES + tile, :] = cp_ref[:, width:2 * width] * cp_ref[:, 2 * width:3 * width]
        conv = jnp.zeros((tile, width), F32)
        for kk in range(SC_CONV):
            off = SUBLANES - (SC_CONV - 1) + kk
            conv = conv + scw_ref[kk:kk + 1, :] * u_ref[off:off + tile, :]
        u_ref[0:SUBLANES, :] = u_ref[tile:tile + SUBLANES, :]
        y_ref[...] = (cp_ref[:, :width] * conv * _silu(cp_ref[:, 3 * width:])).astype(y_ref.dtype)
    return run


def _layer_kernel(x_ref, g_ref, scale_ref, shift_ref, rgate_ref, cos_ref, sin_ref, *refs, tile, final):
    win = refs[:W_IN_BLOCKS]
    (wdt_ref, convw_ref, convb_ref, dtb_ref, alog_ref, dskip_ref, sng_ref, scw_ref,
     wbr_ret_ref, wbr_ssm_ref, wbr_sc_ref, wout_ref, fg_ref,
     o_ref,
     rstate_ref, xbc_ref, sstate_ref, ychunk_ref, u_ref, yret_ref, yssm_ref, ysc_ref,
     qkvg_ref, z_ref, dt_ref, xact_ref, cp_ref, gates_ref, acc_ref,
     term_ref, trans_ref, ld_ref, eb_ref, cd_ref) = refs[W_IN_BLOCKS:]

    @pl.when(pl.program_id(1) == 0)
    def _():
        rstate_ref[...] = jnp.zeros_like(rstate_ref)
        sstate_ref[...] = jnp.zeros_like(sstate_ref)
        xbc_ref[0:SUBLANES, :] = jnp.zeros((SUBLANES, xbc_ref.shape[1]), F32)
        u_ref[0:SUBLANES, :] = jnp.zeros((SUBLANES, u_ref.shape[1]), F32)

    x = x_ref[...]
    d = x.shape[1]
    h = _modulated_norm(x, g_ref[...], scale_ref[...], shift_ref[...])
    blk = FILL_BLOCK

    def proj(dst_ref, first_block, n, act=None):
        return _proj_block(dst_ref, h, win[first_block + n], n * blk, act)

    for n in range(SSM_XBC // blk):
        xbc_ref[SUBLANES:SUBLANES + tile, n * blk:(n + 1) * blk] = _dot(h, win[XBC_BLOCK + n][...])
    n_chunks = tile // CHUNK
    fillers = [_proj_block(dt_ref, h, wdt_ref, 0)]
    fillers += [_ssd_terms_task(c, dt_ref, dtb_ref, alog_ref, term_ref, trans_ref) for c in range(n_chunks)]
    ret_proj = [proj(qkvg_ref, RET_BLOCK, n) for n in range(4 * RET_WIDTH // blk)]
    quad = 4
    head_tasks = [_ssd_head_task(c, list(range(q, q + quad)), term_ref, trans_ref, ld_ref, eb_ref, cd_ref)
                  for c in range(n_chunks) for q in range(0, SSM_HEADS, quad)]
    fillers += [t for duo in zip(ret_proj, head_tasks) for t in duo] + head_tasks[len(ret_proj):]
    _ssd_conv(xbc_ref, xact_ref, convw_ref, convb_ref, tile, fillers)

    def ret_merge():
        acc_ref[...] = gates_ref[:, :d] * _dot(yret_ref[...], wbr_ret_ref[...])

    def sc_merge():
        acc_ref[...] += gates_ref[:, 2 * d:] * _dot(ysc_ref[...], wbr_sc_ref[...])

    dense = [proj(z_ref, Z_BLOCK, n) for n in range(SSM_WIDTH // blk)]
    dense += [proj(cp_ref, SC_BLOCK, n) for n in range(cp_ref.shape[1] // blk)]
    dense += [proj(gates_ref, GATE_BLOCK, n, _sigmoid) for n in range(3 * d // blk)]
    ret_units = [_retention_unit(hd, c, qkvg_ref, cos_ref, sin_ref, rstate_ref, yret_ref)
                 for c in range(tile // CHUNK) for hd in range(RET_HEADS)]
    fillers = [t for duo in zip(dense, ret_units) for t in duo] + dense[len(ret_units):]
    fillers += [ret_merge, _short_conv_task(cp_ref, scw_ref, u_ref, ysc_ref, tile), sc_merge]
    _ssd_scan(xact_ref, z_ref, dskip_ref, sng_ref, term_ref, ld_ref, eb_ref, cd_ref,
              sstate_ref, ychunk_ref, yssm_ref, tile, fillers)
    _run_fillers(fillers, 1)

    merged = acc_ref[...] + gates_ref[:, d:2 * d] * _dot(yssm_ref[...], wbr_ssm_ref[...])
    out = _dot(merged.astype(BF16), wout_ref[...])
    xn = x + rgate_ref[...] * out
    if final:
        ms = jnp.mean(xn * xn, axis=-1, keepdims=True)
        xn = xn * lax.rsqrt(ms + EPS) * fg_ref[...]
    o_ref[...] = xn


def _layer_call(x, ada5, norm_g3, cos2, sin2, weights, smalls, layer, tile, final):
    bsz, seq, d = x.shape
    row_spec = lambda idx: pl.BlockSpec((None, None, None, 1, d), lambda b, j: (layer, b, idx, 0, 0))
    per_layer = lambda a: pl.BlockSpec((None,) + a.shape[1:], lambda b, j: (layer,) + (0,) * (a.ndim - 1),
                                       pipeline_mode=pl.Buffered(1))
    tok = lambda w: pl.BlockSpec((None, tile, w), lambda b, j: (b, j, 0))
    w_blocks, wbr_ret, wbr_ssm, wbr_sc, w_out = weights
    conv_w, conv_b, dt_bias, a_log, d_skip, ssm_ng, sc_w, final_g = smalls
    width = sc_w.shape[2]
    w_in_block = lambda cols, n: pl.BlockSpec((None, d, cols), lambda b, j: (layer, 0, n),
                                              pipeline_mode=pl.Buffered(1))
    dt_block = W_IN_BLOCKS * FILL_BLOCK // DT_PAD
    return pl.pallas_call(
        functools.partial(_layer_kernel, tile=tile, final=final),
        grid=(bsz, seq // tile),
        in_specs=[
            tok(d),
            per_layer(norm_g3),
            row_spec(1),
            row_spec(0),
            row_spec(2),
            tok(RET_DIM), tok(RET_DIM),
        ] + [w_in_block(FILL_BLOCK, n) for n in range(W_IN_BLOCKS)] + [
            w_in_block(DT_PAD, dt_block),
            per_layer(conv_w), per_layer(conv_b), per_layer(dt_bias), per_layer(a_log), per_layer(d_skip),
            per_layer(ssm_ng), per_layer(sc_w),
            per_layer(wbr_ret), per_layer(wbr_ssm), per_layer(wbr_sc), per_layer(w_out),
            pl.BlockSpec(final_g.shape, lambda b, j: (0, 0), pipeline_mode=pl.Buffered(1)),
        ],
        out_specs=tok(d),
        out_shape=jax.ShapeDtypeStruct((bsz, seq, d), F32),
        scratch_shapes=[
            pltpu.VMEM((RET_HEADS, RET_DIM, RET_DIM), F32),
            pltpu.VMEM((tile + SUBLANES, SSM_XBC), F32),
            pltpu.VMEM((SSM_STATE, SSM_WIDTH), F32),
            pltpu.VMEM((CHUNK, SSM_WIDTH), F32),
            pltpu.VMEM((tile + SUBLANES, width), F32),
            pltpu.VMEM((tile, RET_WIDTH), BF16),
            pltpu.VMEM((tile, SSM_WIDTH), BF16),
            pltpu.VMEM((tile, width), BF16),
            pltpu.VMEM((tile, 4 * RET_WIDTH), F32),
            pltpu.VMEM((tile, SSM_WIDTH), F32),
            pltpu.VMEM((tile, DT_PAD), F32),
            pltpu.VMEM((tile, SSM_XBC), F32),
            pltpu.VMEM((tile, 4 * width), F32),
            pltpu.VMEM((tile, 3 * d), F32),
            pltpu.VMEM((tile, d), F32),
            pltpu.VMEM((tile // CHUNK, 3, CHUNK, DT_PAD), F32),
            pltpu.VMEM((tile // CHUNK, 2, DT_PAD, CHUNK), F32),
            pltpu.VMEM((tile // CHUNK * SSM_HEADS, CHUNK, CHUNK), F32),
            pltpu.VMEM((tile // CHUNK * SSM_HEADS, CHUNK, LANES), F32),
            pltpu.VMEM((tile // CHUNK, SSM_HEADS // 2, LANES), F32),
        ],
        compiler_params=pltpu.CompilerParams(
            dimension_semantics=("parallel", "arbitrary"), vmem_limit_bytes=VMEM_LIMIT_BYTES),
        name="layer",
    )(x, norm_g3, ada5, ada5, ada5, cos2, sin2, *([w_blocks] * (W_IN_BLOCKS + 1)),
      conv_w, conv_b, dt_bias, a_log, d_skip, ssm_ng, sc_w, wbr_ret, wbr_ssm, wbr_sc, w_out, final_g)


def kernel(x, c, positions, norm_g, w_ada, b_ada, w_in, ssm_conv_w, ssm_conv_b, ssm_dt_bias, ssm_a_log,
           ssm_d, ssm_norm_g, sc_conv_w, w_br_ret, w_br_ssm, w_br_sc, w_out, final_norm_g):
    bsz, seq, d = x.shape
    depth = w_in.shape[0]
    sc_width = sc_conv_w.shape[2]
    tile = 256
    assert seq % tile == 0 and tile % CHUNK == 0

    sizes = (RET_WIDTH, RET_WIDTH, RET_WIDTH, RET_WIDTH, SSM_WIDTH, SSM_XBC, SSM_HEADS,
             sc_width, sc_width, sc_width, sc_width, 3 * d)
    offs = [0]
    for s in sizes:
        offs.append(offs[-1] + s)
    assert offs[-1] == w_in.shape[2] and d == D_MODEL and sc_width == SC_WIDTH

    ada = _ada_call(c, w_ada, b_ada)
    ada5 = ada.reshape(depth, bsz, 3, 1, d)
    cos2, sin2 = _rope_call(positions)
    norm_g3 = norm_g.reshape(depth, 1, d)

    dt_pad = DT_PAD - SSM_HEADS
    assert offs[6] == SC_BLOCK * FILL_BLOCK
    weights = (_w_in_prep_call(w_in), w_br_ret.astype(BF16), w_br_ssm.astype(BF16), w_br_sc.astype(BF16), w_out.astype(BF16))
    pad_heads = lambda v: jnp.pad(v, ((0, 0), (0, dt_pad))).reshape(depth, 1, DT_PAD)
    smalls = (
        ssm_conv_w, ssm_conv_b.reshape(depth, 1, SSM_XBC),
        pad_heads(ssm_dt_bias), pad_heads(ssm_a_log),
        jnp.repeat(ssm_d, SSM_HEAD_DIM, axis=1).reshape(depth, 1, SSM_WIDTH),
        ssm_norm_g.reshape(depth, 1, SSM_WIDTH), sc_conv_w, final_norm_g.reshape(1, d),
    )
    for layer in range(depth):
        x = _layer_call(x, ada5, norm_g3, cos2, sin2, weights, smalls, layer, tile,
                        final=(layer == depth - 1))
    return x
```

```python
import functools
import math

import jax
import jax.numpy as jnp
from jax import lax
from jax.experimental import pallas as pl
from jax.experimental.pallas import tpu as pltpu

F32 = jnp.float32
BF16 = jnp.bfloat16

EPS = 1e-6
RET_HEADS = 4
RET_DIM = 128
RET_WIDTH = RET_HEADS * RET_DIM
ROPE_THETA = 10000.0
SSM_HEADS = 16
SSM_HEAD_DIM = 64
SSM_GROUPS = 2
SSM_STATE = 128
SSM_CONV = 4
SSM_WIDTH = SSM_HEADS * SSM_HEAD_DIM
SSM_XBC = SSM_WIDTH + 2 * SSM_GROUPS * SSM_STATE
SC_CONV = 3
CHUNK = 128
LANES = 128
SUBLANES = 8
DT_PAD = LANES
FILL_BLOCK = 512
D_MODEL = 1024
SC_WIDTH = D_MODEL // 2
RET_BLOCK = 0
Z_BLOCK = RET_BLOCK + 4 * RET_WIDTH // FILL_BLOCK
XBC_BLOCK = Z_BLOCK + SSM_WIDTH // FILL_BLOCK
SC_BLOCK = XBC_BLOCK + SSM_XBC // FILL_BLOCK
GATE_BLOCK = SC_BLOCK + 4 * SC_WIDTH // FILL_BLOCK
W_IN_BLOCKS = GATE_BLOCK + 3 * D_MODEL // FILL_BLOCK
VMEM_LIMIT_BYTES = 60 * 1024 * 1024

NEG_INF = float("-inf")


def _sigmoid(x):
    return 1.0 / (1.0 + jnp.exp(-x))


def _silu(x):
    return x * _sigmoid(x)


def _dot(a, b):
    return jnp.dot(a, b, preferred_element_type=F32)


def _modulated_norm(x, g, scale, shift):
    ms = jnp.mean(x * x, axis=-1, keepdims=True)
    y = x * lax.rsqrt(ms + EPS) * g
    return (y * (1.0 + scale) + shift).astype(BF16)


def _lane_bcast(a, col):
    return jnp.broadcast_to(a[:, col:col + 1], a.shape)


def _ada_kernel(c_ref, w_ref, b_ref, o_ref):
    act = _silu(c_ref[...])
    hi = act.astype(BF16)
    lo = (act - hi.astype(F32)).astype(BF16)
    w = w_ref[...].astype(BF16)
    o_ref[...] = _dot(hi, w) + _dot(lo, w) + b_ref[...]


def _ada_call(c, w_ada, b_ada):
    depth, d, d3 = w_ada.shape
    bsz = c.shape[0]
    nblk = d3 // d
    return pl.pallas_call(
        _ada_kernel,
        grid=(depth, nblk),
        in_specs=[
            pl.BlockSpec((bsz, d), lambda l, n: (0, 0)),
            pl.BlockSpec((None, d, d), lambda l, n: (l, 0, n)),
            pl.BlockSpec((None, 1, d), lambda l, n: (l, 0, n)),
        ],
        out_specs=pl.BlockSpec((None, bsz, d), lambda l, n: (l, 0, n)),
        out_shape=jax.ShapeDtypeStruct((depth, bsz, d3), F32),
        compiler_params=pltpu.CompilerParams(
            dimension_semantics=("parallel", "parallel"), vmem_limit_bytes=VMEM_LIMIT_BYTES),
        name="ada",
    )(c, w_ada, b_ada.reshape(depth, 1, d3))


def _w_in_prep_kernel(wt_ref, o_ref):
    o_ref[...] = wt_ref[0].T.astype(o_ref.dtype)


def _w_in_prep_call(w_in):
    depth, d, width = w_in.shape
    assert width == W_IN_BLOCKS * FILL_BLOCK + SSM_HEADS
    w_t = jnp.swapaxes(w_in, 1, 2)
    dt_start = SC_BLOCK * FILL_BLOCK
    per_block = FILL_BLOCK // SSM_HEADS
    start = lambda n: SSM_HEADS * jnp.where(n == W_IN_BLOCKS, dt_start // SSM_HEADS,
                                            n * per_block + jnp.where(n >= SC_BLOCK, 1, 0))
    return pl.pallas_call(
        _w_in_prep_kernel,
        grid=(depth, W_IN_BLOCKS + 1),
        in_specs=[pl.BlockSpec((pl.Element(1), pl.Element(FILL_BLOCK), pl.Element(d)),
                               lambda l, n: (l, start(n), 0))],
        out_specs=pl.BlockSpec((None, d, FILL_BLOCK), lambda l, n: (l, 0, n)),
        out_shape=jax.ShapeDtypeStruct((depth, d, (W_IN_BLOCKS + 1) * FILL_BLOCK), BF16),
        compiler_params=pltpu.CompilerParams(
            dimension_semantics=("parallel", "parallel"), vmem_limit_bytes=VMEM_LIMIT_BYTES),
        name="w_in_prep",
    )(w_t)


def _rope_kernel(pos_ref, inv_ref, cos_ref, sin_ref):
    seq = pos_ref.shape[0]
    half_seq, half = seq // 2, RET_DIM // 2
    first = lax.broadcasted_iota(jnp.int32, (half_seq, RET_DIM), 1) < half
    pos = jnp.where(first, pos_ref[0:half_seq, :], pos_ref[half_seq:seq, :])
    ang = pos.astype(F32) * inv_ref[...]
    cs, sn = jnp.cos(ang), jnp.sin(ang)
    cs_sw, sn_sw = pltpu.roll(cs, half, axis=1), pltpu.roll(sn, half, axis=1)
    cos_ref[0:half_seq, :] = jnp.where(first, cs, cs_sw)
    cos_ref[half_seq:seq, :] = jnp.where(first, cs_sw, cs)
    sin_ref[0:half_seq, :] = jnp.where(first, -sn, sn_sw)
    sin_ref[half_seq:seq, :] = jnp.where(first, -sn_sw, sn)


def _rope_call(positions):
    bsz, seq = positions.shape
    half = RET_DIM // 2
    inv_freq = ROPE_THETA ** (-jnp.arange(0, RET_DIM, 2, dtype=F32) / RET_DIM)
    inv2 = jnp.concatenate([inv_freq, inv_freq]).reshape(1, 2 * half)
    tbl = jax.ShapeDtypeStruct((bsz, seq, RET_DIM), F32)
    return pl.pallas_call(
        _rope_kernel,
        grid=(bsz,),
        in_specs=[
            pl.BlockSpec((None, seq, 1), lambda b: (b, 0, 0)),
            pl.BlockSpec((1, RET_DIM), lambda b: (0, 0)),
        ],
        out_specs=[pl.BlockSpec((None, seq, RET_DIM), lambda b: (b, 0, 0))] * 2,
        out_shape=[tbl, tbl],
        compiler_params=pltpu.CompilerParams(
            dimension_semantics=("parallel",), vmem_limit_bytes=VMEM_LIMIT_BYTES),
        name="rope_tables",
    )(positions.reshape(bsz, seq, 1), inv2)


def _run_fillers(fillers, slots_left):
    n = -(-len(fillers) // max(slots_left, 1))
    for _ in range(min(n, len(fillers))):
        fillers.pop(0)()


def _proj_block(dst_ref, h, w_ref, dst_lo, act=None):
    def run():
        r = _dot(h, w_ref[...])
        dst_ref[:, dst_lo:dst_lo + w_ref.shape[1]] = r if act is None else act(r)
    return run


def _retention_unit(hd, c, qkvg_ref, cos_ref, sin_ref, state_ref, y_ref):
    def run():
        row = lax.broadcasted_iota(jnp.int32, (CHUNK, CHUNK), 0)
        col = lax.broadcasted_iota(jnp.int32, (CHUNK, CHUNK), 1)
        rowf = row.astype(F32)
        half = RET_DIM // 2
        log_g = math.log(1.0 - 2.0 ** (-5.0 - hd))
        decay_mask = jnp.exp(jnp.where(row >= col, (row - col).astype(F32) * log_g, NEG_INF))
        q_decay = jnp.exp((rowf + 1.0) * log_g)
        k_decay = jnp.exp((CHUNK - 1.0 - rowf) * log_g)
        chunk_decay = math.exp(CHUNK * log_g)
        rows = slice(c * CHUNK, (c + 1) * CHUNK)
        lo = hd * RET_DIM
        cos = cos_ref[rows, :]
        sin = sin_ref[rows, :]

        def rot(t):
            return t * cos + pltpu.roll(t, half, axis=1) * sin

        qc = (rot(qkvg_ref[rows, lo:lo + RET_DIM]) * (RET_DIM ** -0.5)).astype(BF16)
        kc = rot(qkvg_ref[rows, RET_WIDTH + lo:RET_WIDTH + lo + RET_DIM])
        vc = qkvg_ref[rows, 2 * RET_WIDTH + lo:2 * RET_WIDTH + lo + RET_DIM].astype(BF16)
        gate = qkvg_ref[rows, 3 * RET_WIDTH + lo:3 * RET_WIDTH + lo + RET_DIM]
        state = state_ref[hd]
        scores = lax.dot_general(qc, kc.astype(BF16), (((1,), (1,)), ((), ())), preferred_element_type=F32)
        inner = _dot((scores * decay_mask).astype(BF16), vc)
        cross = _dot(qc, state.astype(BF16)) * q_decay
        kd = (kc * k_decay).astype(BF16)
        state_ref[hd] = chunk_decay * state + lax.dot_general(kd, vc, (((0,), (0,)), ((), ())),
                                                              preferred_element_type=F32)
        o = inner + cross
        mu = jnp.mean(o, axis=-1, keepdims=True)
        oc = o - mu
        var = jnp.mean(oc * oc, axis=-1, keepdims=True)
        y = oc * lax.rsqrt(var + EPS) * _silu(gate)
        y_ref[rows, lo:lo + RET_DIM] = y.astype(y_ref.dtype)
    return run


def _block_diag_pair(slab):
    lane = lax.broadcasted_iota(jnp.int32, slab.shape, 1)
    first = lane < SSM_HEAD_DIM
    top = jnp.where(first, slab, 0.0)
    bot = jnp.where(first, 0.0, slab)
    return jnp.concatenate([top, bot], axis=0).astype(BF16)


def _ssd_conv(xbc_ref, xact_ref, convw_ref, convb_ref, tile, fillers):
    nblk = SSM_XBC // FILL_BLOCK
    for n in range(nblk):
        cols = slice(n * FILL_BLOCK, (n + 1) * FILL_BLOCK)
        conv = convb_ref[:, cols]
        for kk in range(SSM_CONV):
            off = SUBLANES - (SSM_CONV - 1) + kk
            conv = conv + convw_ref[kk:kk + 1, cols] * xbc_ref[off:off + tile, cols]
        xact_ref[:, cols] = _silu(conv)
        _run_fillers(fillers, nblk - n)
    xbc_ref[0:SUBLANES, :] = xbc_ref[tile:tile + SUBLANES, :]


T_ACS, T_WT, T_EACS = 0, 1, 2


def _cumsum_rows(x):
    row = lax.broadcasted_iota(jnp.int32, x.shape, 0)
    k = 1
    while k < x.shape[0]:
        x = x + jnp.where(row >= k, pltpu.roll(x, k, axis=0), 0.0)
        k *= 2
    return x


def _ssd_terms_task(c, dt_ref, dtb_ref, alog_ref, term_ref, trans_ref):
    def run():
        rows = slice(c * CHUNK, (c + 1) * CHUNK)
        dt_in = dt_ref[rows, :] + dtb_ref[...]
        dt_c = jnp.maximum(dt_in, 0.0) + jnp.log1p(jnp.exp(-jnp.abs(dt_in)))
        a_cs = _cumsum_rows(dt_c * (-jnp.exp(alog_ref[...])))
        a_last = a_cs[CHUNK - 1:CHUNK, :]
        term_ref[c, T_ACS] = a_cs
        term_ref[c, T_WT] = (jnp.exp(a_last - a_cs) * dt_c).T
        term_ref[c, T_EACS] = jnp.exp(a_cs)
        trans_ref[c, 0] = a_cs.T
        trans_ref[c, 1] = dt_c.T
    return run


def _ssd_head_task(c, heads, term_ref, trans_ref, ld_ref, eb_ref, cd_ref):
    def run():
        row = lax.broadcasted_iota(jnp.int32, (CHUNK, CHUNK), 0)
        col = lax.broadcasted_iota(jnp.int32, (CHUNK, CHUNK), 1)
        tril = row >= col
        lane1 = lax.broadcasted_iota(jnp.int32, (1, LANES), 1)
        a_cs = term_ref[c, T_ACS]
        e_acs = term_ref[c, T_EACS]
        cdec = e_acs[CHUNK - 1:CHUNK, :]
        for hh in heads:
            seg = _lane_bcast(a_cs, hh) - trans_ref[c, 0, hh:hh + 1, :]
            ld_ref[c * SSM_HEADS + hh] = jnp.exp(jnp.where(tril, seg, NEG_INF)) * trans_ref[c, 1, hh:hh + 1, :]
            eb_ref[c * SSM_HEADS + hh] = _lane_bcast(e_acs, hh)
        for h1 in heads[::2]:
            cd_ref[c, h1 // 2:h1 // 2 + 1, :] = jnp.where(
                lane1 < SSM_HEAD_DIM, _lane_bcast(cdec, h1), _lane_bcast(cdec, h1 + 1))
    return run


def _ssd_scan(xact_ref, z_ref, dskip_ref, ng_ref, term_ref, ld_ref, eb_ref, cd_ref,
              state_ref, ychunk_ref, y_ref, tile, fillers):
    pair = 2 * SSM_HEAD_DIM
    heads_per_group = SSM_HEADS // SSM_GROUPS
    n_chunks = tile // CHUNK
    slots = n_chunks * SSM_HEADS // 2

    for c in range(n_chunks):
        rows = slice(c * CHUNK, (c + 1) * CHUNK)
        for g in range(SSM_GROUPS):
            b_lo = SSM_WIDTH + g * SSM_STATE
            c_lo = SSM_WIDTH + SSM_GROUPS * SSM_STATE + g * SSM_STATE
            b_g = xact_ref[rows, b_lo:b_lo + SSM_STATE]
            c_g = xact_ref[rows, c_lo:c_lo + SSM_STATE]
            b_t = b_g.T
            cb = _dot(c_g.astype(BF16), b_t.astype(BF16))
            for p in range(heads_per_group // 2):
                h1 = g * heads_per_group + 2 * p
                lo = h1 * SSM_HEAD_DIM
                xs_pair = xact_ref[rows, lo:lo + pair]
                rhs = _block_diag_pair(xs_pair)
                m_parts, s_parts, o_parts = [], [], []
                for hh in (h1, h1 + 1):
                    m_parts.append(cb * ld_ref[c * SSM_HEADS + hh])
                    s_parts.append(b_t * term_ref[c, T_WT, hh:hh + 1, :])
                    o_parts.append(c_g * eb_ref[c * SSM_HEADS + hh])
                y_diag = _dot(jnp.concatenate(m_parts, axis=1).astype(BF16), rhs)
                s_new = _dot(jnp.concatenate(s_parts, axis=1).astype(BF16), rhs)
                s_prev = state_ref[:, lo:lo + pair]
                y_off = _dot(jnp.concatenate(o_parts, axis=1).astype(BF16), _block_diag_pair(s_prev))
                state_ref[:, lo:lo + pair] = cd_ref[c, h1 // 2:h1 // 2 + 1, :] * s_prev + s_new
                ychunk_ref[:, lo:lo + pair] = y_diag + y_off + dskip_ref[:, lo:lo + pair] * xs_pair
                _run_fillers(fillers, slots)
                slots -= 1
        y = ychunk_ref[...] * _silu(z_ref[rows, :])
        ms = jnp.mean(y * y, axis=-1, keepdims=True)
        y_ref[rows, :] = (y * lax.rsqrt(ms + EPS) * ng_ref[...]).astype(y_ref.dtype)


def _short_conv_task(cp_ref, scw_ref, u_ref, y_ref, tile):
    def run():
        width = u_ref.shape[1]
        u_ref[SUBLANES:SUBLANES + tile, :] = cp_ref[:, width:2 * width] * cp_ref[:, 2 * width:3 * width]
        conv = jnp.zeros((tile, width), F32)
        for kk in range(SC_CONV):
            off = SUBLANES - (SC_CONV - 1) + kk
            conv = conv + scw_ref[kk:kk + 1, :] * u_ref[off:off + tile, :]
        u_ref[0:SUBLANES, :] = u_ref[tile:tile + SUBLANES, :]
        y_ref[...] = (cp_ref[:, :width] * conv * _silu(cp_ref[:, 3 * width:])).astype(y_ref.dtype)
    return run


def _layer_kernel(x_ref, g_ref, scale_ref, shift_ref, rgate_ref, cos_ref, sin_ref, *refs, tile, final):
    win = refs[:W_IN_BLOCKS]
    (wdt_ref, convw_ref, convb_ref, dtb_ref, alog_ref, dskip_ref, sng_ref, scw_ref,
     wbr_ret_ref, wbr_ssm_ref, wbr_sc_ref, wout_ref, fg_ref,
     o_ref,
     rstate_ref, xbc_ref, sstate_ref, ychunk_ref, u_ref, yret_ref, yssm_ref, ysc_ref,
     qkvg_ref, z_ref, dt_ref, xact_ref, cp_ref, gates_ref, acc_ref,
     term_ref, trans_ref, ld_ref, eb_ref, cd_ref) = refs[W_IN_BLOCKS:]

    @pl.when(pl.program_id(1) == 0)
    def _():
        rstate_ref[...] = jnp.zeros_like(rstate_ref)
        sstate_ref[...] = jnp.zeros_like(sstate_ref)
        xbc_ref[0:SUBLANES, :] = jnp.zeros((SUBLANES, xbc_ref.shape[1]), F32)
        u_ref[0:SUBLANES, :] = jnp.zeros((SUBLANES, u_ref.shape[1]), F32)

    x = x_ref[...]
    d = x.shape[1]
    h = _modulated_norm(x, g_ref[...], scale_ref[...], shift_ref[...])
    blk = FILL_BLOCK

    def proj(dst_ref, first_block, n, act=None):
        return _proj_block(dst_ref, h, win[first_block + n], n * blk, act)

    for n in range(SSM_XBC // blk):
        xbc_ref[SUBLANES:SUBLANES + tile, n * blk:(n + 1) * blk] = _dot(h, win[XBC_BLOCK + n][...])
    n_chunks = tile // CHUNK
    fillers = [_proj_block(dt_ref, h, wdt_ref, 0)]
    fillers += [_ssd_terms_task(c, dt_ref, dtb_ref, alog_ref, term_ref, trans_ref) for c in range(n_chunks)]
    ret_proj = [proj(qkvg_ref, RET_BLOCK, n) for n in range(4 * RET_WIDTH // blk)]
    quad = 4
    head_tasks = [_ssd_head_task(c, list(range(q, q + quad)), term_ref, trans_ref, ld_ref, eb_ref, cd_ref)
                  for c in range(n_chunks) for q in range(0, SSM_HEADS, quad)]
    fillers += [t for duo in zip(ret_proj, head_tasks) for t in duo] + head_tasks[len(ret_proj):]
    _ssd_conv(xbc_ref, xact_ref, convw_ref, convb_ref, tile, fillers)

    def ret_merge():
        acc_ref[...] = gates_ref[:, :d] * _dot(yret_ref[...], wbr_ret_ref[...])

    def sc_merge():
        acc_ref[...] += gates_ref[:, 2 * d:] * _dot(ysc_ref[...], wbr_sc_ref[...])

    dense = [proj(z_ref, Z_BLOCK, n) for n in range(SSM_WIDTH // blk)]
    dense += [proj(cp_ref, SC_BLOCK, n) for n in range(cp_ref.shape[1] // blk)]
    dense += [proj(gates_ref, GATE_BLOCK, n, _sigmoid) for n in range(3 * d // blk)]
    ret_units = [_retention_unit(hd, c, qkvg_ref, cos_ref, sin_ref, rstate_ref, yret_ref)
                 for c in range(tile // CHUNK) for hd in range(RET_HEADS)]
    fillers = [t for duo in zip(dense, ret_units) for t in duo] + dense[len(ret_units):]
    fillers += [ret_merge, _short_conv_task(cp_ref, scw_ref, u_ref, ysc_ref, tile), sc_merge]
    _ssd_scan(xact_ref, z_ref, dskip_ref, sng_ref, term_ref, ld_ref, eb_ref, cd_ref,
              sstate_ref, ychunk_ref, yssm_ref, tile, fillers)
    _run_fillers(fillers, 1)

    merged = acc_ref[...] + gates_ref[:, d:2 * d] * _dot(yssm_ref[...], wbr_ssm_ref[...])
    out = _dot(merged.astype(BF16), wout_ref[...])
    xn = x + rgate_ref[...] * out
    if final:
        ms = jnp.mean(xn * xn, axis=-1, keepdims=True)
        xn = xn * lax.rsqrt(ms + EPS) * fg_ref[...]
    o_ref[...] = xn


def _layer_call(x, ada5, norm_g3, cos2, sin2, weights, smalls, layer, tile, final):
    bsz, seq, d = x.shape
    row_spec = lambda idx: pl.BlockSpec((None, None, None, 1, d), lambda b, j: (layer, b, idx, 0, 0))
    per_layer = lambda a: pl.BlockSpec((None,) + a.shape[1:], lambda b, j: (layer,) + (0,) * (a.ndim - 1),
                                       pipeline_mode=pl.Buffered(1))
    tok = lambda w: pl.BlockSpec((None, tile, w), lambda b, j: (b, j, 0))
    w_blocks, wbr_ret, wbr_ssm, wbr_sc, w_out = weights
    conv_w, conv_b, dt_bias, a_log, d_skip, ssm_ng, sc_w, final_g = smalls
    width = sc_w.shape[2]
    w_in_block = lambda cols, n: pl.BlockSpec((None, d, cols), lambda b, j: (layer, 0, n),
                                              pipeline_mode=pl.Buffered(1))
    dt_block = W_IN_BLOCKS * FILL_BLOCK // DT_PAD
    return pl.pallas_call(
        functools.partial(_layer_kernel, tile=tile, final=final),
        grid=(bsz, seq // tile),
        in_specs=[
            tok(d),
            per_layer(norm_g3),
            row_spec(1),
            row_spec(0),
            row_spec(2),
            tok(RET_DIM), tok(RET_DIM),
        ] + [w_in_block(FILL_BLOCK, n) for n in range(W_IN_BLOCKS)] + [
            w_in_block(DT_PAD, dt_block),
            per_layer(conv_w), per_layer(conv_b), per_layer(dt_bias), per_layer(a_log), per_layer(d_skip),
            per_layer(ssm_ng), per_layer(sc_w),
            per_layer(wbr_ret), per_layer(wbr_ssm), per_layer(wbr_sc), per_layer(w_out),
            pl.BlockSpec(final_g.shape, lambda b, j: (0, 0), pipeline_mode=pl.Buffered(1)),
        ],
        out_specs=tok(d),
        out_shape=jax.ShapeDtypeStruct((bsz, seq, d), F32),
        scratch_shapes=[
            pltpu.VMEM((RET_HEADS, RET_DIM, RET_DIM), F32),
            pltpu.VMEM((tile + SUBLANES, SSM_XBC), F32),
            pltpu.VMEM((SSM_STATE, SSM_WIDTH), F32),
            pltpu.VMEM((CHUNK, SSM_WIDTH), F32),
            pltpu.VMEM((tile + SUBLANES, width), F32),
            pltpu.VMEM((tile, RET_WIDTH), BF16),
            pltpu.VMEM((tile, SSM_WIDTH), BF16),
            pltpu.VMEM((tile, width), BF16),
            pltpu.VMEM((tile, 4 * RET_WIDTH), F32),
            pltpu.VMEM((tile, SSM_WIDTH), F32),
            pltpu.VMEM((tile, DT_PAD), F32),
            pltpu.VMEM((tile, SSM_XBC), F32),
            pltpu.VMEM((tile, 4 * width), F32),
            pltpu.VMEM((tile, 3 * d), F32),
            pltpu.VMEM((tile, d), F32),
            pltpu.VMEM((tile // CHUNK, 3, CHUNK, DT_PAD), F32),
            pltpu.VMEM((tile // CHUNK, 2, DT_PAD, CHUNK), F32),
            pltpu.VMEM((tile // CHUNK * SSM_HEADS, CHUNK, CHUNK), F32),
            pltpu.VMEM((tile // CHUNK * SSM_HEADS, CHUNK, LANES), F32),
            pltpu.VMEM((tile // CHUNK, SSM_HEADS // 2, LANES), F32),
        ],
        compiler_params=pltpu.CompilerParams(
            dimension_semantics=("parallel", "arbitrary"), vmem_limit_bytes=VMEM_LIMIT_BYTES),
        name="layer",
    )(x, norm_g3, ada5, ada5, ada5, cos2, sin2, *([w_blocks] * (W_IN_BLOCKS + 1)),
      conv_w, conv_b, dt_bias, a_log, d_skip, ssm_ng, sc_w, wbr_ret, wbr_ssm, wbr_sc, w_out, final_g)


def kernel(x, c, positions, norm_g, w_ada, b_ada, w_in, ssm_conv_w, ssm_conv_b, ssm_dt_bias, ssm_a_log,
           ssm_d, ssm_norm_g, sc_conv_w, w_br_ret, w_br_ssm, w_br_sc, w_out, final_norm_g):
    bsz, seq, d = x.shape
    depth = w_in.shape[0]
    sc_width = sc_conv_w.shape[2]
    tile = 256
    assert seq % tile == 0 and tile % CHUNK == 0

    sizes = (RET_WIDTH, RET_WIDTH, RET_WIDTH, RET_WIDTH, SSM_WIDTH, SSM_XBC, SSM_HEADS,
             sc_width, sc_width, sc_width, sc_width, 3 * d)
    offs = [0]
    for s in sizes:
        offs.append(offs[-1] + s)
    assert offs[-1] == w_in.shape[2] and d == D_MODEL and sc_width == SC_WIDTH

    ada = _ada_call(c, w_ada, b_ada)
    ada5 = ada.reshape(depth, bsz, 3, 1, d)
    cos2, sin2 = _rope_call(positions)
    norm_g3 = norm_g.reshape(depth, 1, d)

    dt_pad = DT_PAD - SSM_HEADS
    assert offs[6] == SC_BLOCK * FILL_BLOCK
    weights = (_w_in_prep_call(w_in), w_br_ret.astype(BF16), w_br_ssm.astype(BF16), w_br_sc.astype(BF16), w_out.astype(BF16))
    pad_heads = lambda v: jnp.pad(v, ((0, 0), (0, dt_pad))).reshape(depth, 1, DT_PAD)
    smalls = (
        ssm_conv_w, ssm_conv_b.reshape(depth, 1, SSM_XBC),
        pad_heads(ssm_dt_bias), pad_heads(ssm_a_log),
        jnp.repeat(ssm_d, SSM_HEAD_DIM, axis=1).reshape(depth, 1, SSM_WIDTH),
        ssm_norm_g.reshape(depth, 1, SSM_WIDTH), sc_conv_w, final_norm_g.reshape(1, d),
    )
    for layer in range(depth):
        x = _layer_call(x, ada5, norm_g3, cos2, sin2, weights, smalls, layer, tile,
                        final=(layer == depth - 1))
    return x
```

```python
import functools
import math

import jax
import jax.numpy as jnp
from jax import lax
from jax.experimental import pallas as pl
from jax.experimental.pallas import tpu as pltpu

F32 = jnp.float32
BF16 = jnp.bfloat16

EPS = 1e-6
RET_HEADS = 4
RET_DIM = 128
RET_WIDTH = RET_HEADS * RET_DIM
ROPE_THETA = 10000.0
SSM_HEADS = 16
SSM_HEAD_DIM = 64
SSM_GROUPS = 2
SSM_STATE = 128
SSM_CONV = 4
SSM_WIDTH = SSM_HEADS * SSM_HEAD_DIM
SSM_XBC = SSM_WIDTH + 2 * SSM_GROUPS * SSM_STATE
SC_CONV = 3
CHUNK = 128
LANES = 128
SUBLANES = 8
DT_PAD = LANES
FILL_BLOCK = 512
D_MODEL = 1024
SC_WIDTH = D_MODEL // 2
RET_BLOCK = 0
Z_BLOCK = RET_BLOCK + 4 * RET_WIDTH // FILL_BLOCK
XBC_BLOCK = Z_BLOCK + SSM_WIDTH // FILL_BLOCK
SC_BLOCK = XBC_BLOCK + SSM_XBC // FILL_BLOCK
GATE_BLOCK = SC_BLOCK + 4 * SC_WIDTH // FILL_BLOCK
W_IN_BLOCKS = GATE_BLOCK + 3 * D_MODEL // FILL_BLOCK
VMEM_LIMIT_BYTES = 60 * 1024 * 1024

NEG_INF = float("-inf")


def _sigmoid(x):
    return 1.0 / (1.0 + jnp.exp(-x))


def _silu(x):
    return x * _sigmoid(x)


def _dot(a, b):
    return jnp.dot(a, b, preferred_element_type=F32)


def _modulated_norm(x, g, scale, shift):
    ms = jnp.mean(x * x, axis=-1, keepdims=True)
    y = x * lax.rsqrt(ms + EPS) * g
    return (y * (1.0 + scale) + shift).astype(BF16)


def _lane_bcast(a, col):
    return jnp.broadcast_to(a[:, col:col + 1], a.shape)


def _ada_kernel(c_ref, w_ref, b_ref, o_ref):
    act = _silu(c_ref[...])
    hi = act.astype(BF16)
    lo = (act - hi.astype(F32)).astype(BF16)
    w = w_ref[...].astype(BF16)
    o_ref[...] = _dot(hi, w) + _dot(lo, w) + b_ref[...]


def _ada_call(c, w_ada, b_ada):
    depth, d, d3 = w_ada.shape
    bsz = c.shape[0]
    nblk = d3 // d
    return pl.pallas_call(
        _ada_kernel,
        grid=(depth, nblk),
        in_specs=[
            pl.BlockSpec((bsz, d), lambda l, n: (0, 0)),
            pl.BlockSpec((None, d, d), lambda l, n: (l, 0, n)),
            pl.BlockSpec((None, 1, d), lambda l, n: (l, 0, n)),
        ],
        out_specs=pl.BlockSpec((None, bsz, d), lambda l, n: (l, 0, n)),
        out_shape=jax.ShapeDtypeStruct((depth, bsz, d3), F32),
        compiler_params=pltpu.CompilerParams(
            dimension_semantics=("parallel", "parallel"), vmem_limit_bytes=VMEM_LIMIT_BYTES),
        name="ada",
    )(c, w_ada, b_ada.reshape(depth, 1, d3))


def _w_in_prep_kernel(wa_ref, wb_ref, o_ref):
    o_ref[:, :FILL_BLOCK] = wa_ref[0].T.astype(o_ref.dtype)
    o_ref[:, FILL_BLOCK:] = wb_ref[0].T.astype(o_ref.dtype)


def _w_in_prep_call(w_in):
    depth, d, width = w_in.shape
    assert width == W_IN_BLOCKS * FILL_BLOCK + SSM_HEADS and (W_IN_BLOCKS + 1) % 2 == 0
    w_t = jnp.swapaxes(w_in, 1, 2)
    dt_start = SC_BLOCK * FILL_BLOCK
    per_block = FILL_BLOCK // SSM_HEADS
    start = lambda n: SSM_HEADS * jnp.where(n == W_IN_BLOCKS, dt_start // SSM_HEADS,
                                            n * per_block + jnp.where(n >= SC_BLOCK, 1, 0))
    rows = lambda which: pl.BlockSpec((pl.Element(1), pl.Element(FILL_BLOCK), pl.Element(d)),
                                      lambda l, n: (l, start(2 * n + which), 0))
    return pl.pallas_call(
        _w_in_prep_kernel,
        grid=(depth, (W_IN_BLOCKS + 1) // 2),
        in_specs=[rows(0), rows(1)],
        out_specs=pl.BlockSpec((None, d, 2 * FILL_BLOCK), lambda l, n: (l, 0, n)),
        out_shape=jax.ShapeDtypeStruct((depth, d, (W_IN_BLOCKS + 1) * FILL_BLOCK), BF16),
        compiler_params=pltpu.CompilerParams(
            dimension_semantics=("parallel", "parallel"), vmem_limit_bytes=VMEM_LIMIT_BYTES),
        name="w_in_prep",
    )(w_t, w_t)


def _rope_kernel(pos_ref, inv_ref, cos_ref, sin_ref):
    seq = pos_ref.shape[0]
    half_seq, half = seq // 2, RET_DIM // 2
    first = lax.broadcasted_iota(jnp.int32, (half_seq, RET_DIM), 1) < half
    pos = jnp.where(first, pos_ref[0:half_seq, :], pos_ref[half_seq:seq, :])
    ang = pos.astype(F32) * inv_ref[...]
    cs, sn = jnp.cos(ang), jnp.sin(ang)
    cs_sw, sn_sw = pltpu.roll(cs, half, axis=1), pltpu.roll(sn, half, axis=1)
    cos_ref[0:half_seq, :] = jnp.where(first, cs, cs_sw)
    cos_ref[half_seq:seq, :] = jnp.where(first, cs_sw, cs)
    sin_ref[0:half_seq, :] = jnp.where(first, -sn, sn_sw)
    sin_ref[half_seq:seq, :] = jnp.where(first, -sn_sw, sn)


def _rope_call(positions):
    bsz, seq = positions.shape
    half = RET_DIM // 2
    inv_freq = ROPE_THETA ** (-jnp.arange(0, RET_DIM, 2, dtype=F32) / RET_DIM)
    inv2 = jnp.concatenate([inv_freq, inv_freq]).reshape(1, 2 * half)
    tbl = jax.ShapeDtypeStruct((bsz, seq, RET_DIM), F32)
    return pl.pallas_call(
        _rope_kernel,
        grid=(bsz,),
        in_specs=[
            pl.BlockSpec((None, seq, 1), lambda b: (b, 0, 0)),
            pl.BlockSpec((1, RET_DIM), lambda b: (0, 0)),
        ],
        out_specs=[pl.BlockSpec((None, seq, RET_DIM), lambda b: (b, 0, 0))] * 2,
        out_shape=[tbl, tbl],
        compiler_params=pltpu.CompilerParams(
            dimension_semantics=("parallel",), vmem_limit_bytes=VMEM_LIMIT_BYTES),
        name="rope_tables",
    )(positions.reshape(bsz, seq, 1), inv2)


def _run_fillers(fillers, slots_left):
    n = -(-len(fillers) // max(slots_left, 1))
    for _ in range(min(n, len(fillers))):
        fillers.pop(0)()


def _proj_block(dst_ref, h, w_ref, dst_lo, act=None):
    def run():
        r = _dot(h, w_ref[...])
        dst_ref[:, dst_lo:dst_lo + w_ref.shape[1]] = r if act is None else act(r)
    return run


def _retention_unit(hd, c, qkvg_ref, cos_ref, sin_ref, state_ref, y_ref):
    def run():
        row = lax.broadcasted_iota(jnp.int32, (CHUNK, CHUNK), 0)
        col = lax.broadcasted_iota(jnp.int32, (CHUNK, CHUNK), 1)
        rowf = row.astype(F32)
        half = RET_DIM // 2
        log_g = math.log(1.0 - 2.0 ** (-5.0 - hd))
        decay_mask = jnp.exp(jnp.where(row >= col, (row - col).astype(F32) * log_g, NEG_INF))
        q_decay = jnp.exp((rowf + 1.0) * log_g)
        k_decay = jnp.exp((CHUNK - 1.0 - rowf) * log_g)
        chunk_decay = math.exp(CHUNK * log_g)
        rows = slice(c * CHUNK, (c + 1) * CHUNK)
        lo = hd * RET_DIM
        cos = cos_ref[rows, :]
        sin = sin_ref[rows, :]

        def rot(t):
            return t * cos + pltpu.roll(t, half, axis=1) * sin

        qc = (rot(qkvg_ref[rows, lo:lo + RET_DIM]) * (RET_DIM ** -0.5)).astype(BF16)
        kc = rot(qkvg_ref[rows, RET_WIDTH + lo:RET_WIDTH + lo + RET_DIM])
        vc = qkvg_ref[rows, 2 * RET_WIDTH + lo:2 * RET_WIDTH + lo + RET_DIM].astype(BF16)
        gate = qkvg_ref[rows, 3 * RET_WIDTH + lo:3 * RET_WIDTH + lo + RET_DIM]
        state = state_ref[hd]
        scores = lax.dot_general(qc, kc.astype(BF16), (((1,), (1,)), ((), ())), preferred_element_type=F32)
        inner = _dot((scores * decay_mask).astype(BF16), vc)
        cross = _dot(qc, state.astype(BF16)) * q_decay
        kd = (kc * k_decay).astype(BF16)
        state_ref[hd] = chunk_decay * state + lax.dot_general(kd, vc, (((0,), (0,)), ((), ())),
                                                              preferred_element_type=F32)
        o = inner + cross
        mu = jnp.mean(o, axis=-1, keepdims=True)
        oc = o - mu
        var = jnp.mean(oc * oc, axis=-1, keepdims=True)
        y = oc * lax.rsqrt(var + EPS) * _silu(gate)
        y_ref[rows, lo:lo + RET_DIM] = y.astype(y_ref.dtype)
    return run


def _block_diag_pair(slab):
    lane = lax.broadcasted_iota(jnp.int32, slab.shape, 1)
    first = lane < SSM_HEAD_DIM
    top = jnp.where(first, slab, 0.0)
    bot = jnp.where(first, 0.0, slab)
    return jnp.concatenate([top, bot], axis=0).astype(BF16)


def _ssd_conv(xbc_ref, xact_ref, convw_ref, convb_ref, tile, fillers):
    nblk = SSM_XBC // FILL_BLOCK
    for n in range(nblk):
        cols = slice(n * FILL_BLOCK, (n + 1) * FILL_BLOCK)
        conv = convb_ref[:, cols]
        for kk in range(SSM_CONV):
            off = SUBLANES - (SSM_CONV - 1) + kk
            conv = conv + convw_ref[kk:kk + 1, cols] * xbc_ref[off:off + tile, cols]
        xact_ref[:, cols] = _silu(conv)
        _run_fillers(fillers, nblk - n)
    xbc_ref[0:SUBLANES, :] = xbc_ref[tile:tile + SUBLANES, :]


T_ACS, T_WT, T_EACS = 0, 1, 2


def _cumsum_rows(x):
    row = lax.broadcasted_iota(jnp.int32, x.shape, 0)
    k = 1
    while k < x.shape[0]:
        x = x + jnp.where(row >= k, pltpu.roll(x, k, axis=0), 0.0)
        k *= 2
    return x


def _ssd_terms_task(c, dt_ref, dtb_ref, alog_ref, term_ref, trans_ref):
    def run():
        rows = slice(c * CHUNK, (c + 1) * CHUNK)
        dt_in = dt_ref[rows, :] + dtb_ref[...]
        dt_c = jnp.maximum(dt_in, 0.0) + jnp.log1p(jnp.exp(-jnp.abs(dt_in)))
        a_cs = _cumsum_rows(dt_c * (-jnp.exp(alog_ref[...])))
        a_last = a_cs[CHUNK - 1:CHUNK, :]
        term_ref[c, T_ACS] = a_cs
        term_ref[c, T_WT] = (jnp.exp(a_last - a_cs) * dt_c).T
        term_ref[c, T_EACS] = jnp.exp(a_cs)
        trans_ref[c, 0] = a_cs.T
        trans_ref[c, 1] = dt_c.T
    return run


def _ssd_head_task(c, heads, term_ref, trans_ref, ld_ref, eb_ref, cd_ref):
    def run():
        row = lax.broadcasted_iota(jnp.int32, (CHUNK, CHUNK), 0)
        col = lax.broadcasted_iota(jnp.int32, (CHUNK, CHUNK), 1)
        tril = row >= col
        lane1 = lax.broadcasted_iota(jnp.int32, (1, LANES), 1)
        a_cs = term_ref[c, T_ACS]
        e_acs = term_ref[c, T_EACS]
        cdec = e_acs[CHUNK - 1:CHUNK, :]
        for hh in heads:
            seg = _lane_bcast(a_cs, hh) - trans_ref[c, 0, hh:hh + 1, :]
            ld_ref[c * SSM_HEADS + hh] = jnp.exp(jnp.where(tril, seg, NEG_INF)) * trans_ref[c, 1, hh:hh + 1, :]
            eb_ref[c * SSM_HEADS + hh] = _lane_bcast(e_acs, hh)
        for h1 in heads[::2]:
            cd_ref[c, h1 // 2:h1 // 2 + 1, :] = jnp.where(
                lane1 < SSM_HEAD_DIM, _lane_bcast(cdec, h1), _lane_bcast(cdec, h1 + 1))
    return run


def _ssd_scan(xact_ref, z_ref, dskip_ref, ng_ref, term_ref, ld_ref, eb_ref, cd_ref,
              state_ref, ychunk_ref, y_ref, tile, fillers):
    pair = 2 * SSM_HEAD_DIM
    heads_per_group = SSM_HEADS // SSM_GROUPS
    n_chunks = tile // CHUNK
    slots = n_chunks * SSM_HEADS // 2

    for c in range(n_chunks):
        rows = slice(c * CHUNK, (c + 1) * CHUNK)
        for g in range(SSM_GROUPS):
            b_lo = SSM_WIDTH + g * SSM_STATE
            c_lo = SSM_WIDTH + SSM_GROUPS * SSM_STATE + g * SSM_STATE
            b_g = xact_ref[rows, b_lo:b_lo + SSM_STATE]
            c_g = xact_ref[rows, c_lo:c_lo + SSM_STATE]
            b_t = b_g.T
            cb = _dot(c_g.astype(BF16), b_t.astype(BF16))
            for p in range(heads_per_group // 2):
                h1 = g * heads_per_group + 2 * p
                lo = h1 * SSM_HEAD_DIM
                xs_pair = xact_ref[rows, lo:lo + pair]
                rhs = _block_diag_pair(xs_pair)
                m_parts, s_parts, o_parts = [], [], []
                for hh in (h1, h1 + 1):
                    m_parts.append(cb * ld_ref[c * SSM_HEADS + hh])
                    s_parts.append(b_t * term_ref[c, T_WT, hh:hh + 1, :])
                    o_parts.append(c_g * eb_ref[c * SSM_HEADS + hh])
                y_diag = _dot(jnp.concatenate(m_parts, axis=1).astype(BF16), rhs)
                s_new = _dot(jnp.concatenate(s_parts, axis=1).astype(BF16), rhs)
                s_prev = state_ref[:, lo:lo + pair]
                y_off = _dot(jnp.concatenate(o_parts, axis=1).astype(BF16), _block_diag_pair(s_prev))
                state_ref[:, lo:lo + pair] = cd_ref[c, h1 // 2:h1 // 2 + 1, :] * s_prev + s_new
                ychunk_ref[:, lo:lo + pair] = y_diag + y_off + dskip_ref[:, lo:lo + pair] * xs_pair
                _run_fillers(fillers, slots)
                slots -= 1
        y = ychunk_ref[...] * _silu(z_ref[rows, :])
        ms = jnp.mean(y * y, axis=-1, keepdims=True)
        y_ref[rows, :] = (y * lax.rsqrt(ms + EPS) * ng_ref[...]).astype(y_ref.dtype)


def _short_conv_task(cp_ref, scw_ref, u_ref, y_ref, tile):
    def run():
        width = u_ref.shape[1]
        u_ref[SUBLANES:SUBLANES + tile, :] = cp_ref[:, width:2 * width] * cp_ref[:, 2 * width:3 * width]
        conv = jnp.zeros((tile, width), F32)
        for kk in range(SC_CONV):
            off = SUBLANES - (SC_CONV - 1) + kk
            conv = conv + scw_ref[kk:kk + 1, :] * u_ref[off:off + tile, :]
        u_ref[0:SUBLANES, :] = u_ref[tile:tile + SUBLANES, :]
        y_ref[...] = (cp_ref[:, :width] * conv * _silu(cp_ref[:, 3 * width:])).astype(y_ref.dtype)
    return run


def _layer_kernel(x_ref, g_ref, scale_ref, shift_ref, rgate_ref, cos_ref, sin_ref, *refs, tile, final):
    win = refs[:W_IN_BLOCKS]
    (wdt_ref, convw_ref, convb_ref, dtb_ref, alog_ref, dskip_ref, sng_ref, scw_ref,
     wbr_ret_ref, wbr_ssm_ref, wbr_sc_ref, wout_ref, fg_ref,
     o_ref,
     rstate_ref, xbc_ref, sstate_ref, ychunk_ref, u_ref, yret_ref, yssm_ref, ysc_ref,
     qkvg_ref, z_ref, dt_ref, xact_ref, cp_ref, gates_ref, acc_ref,
     term_ref, trans_ref, ld_ref, eb_ref, cd_ref) = refs[W_IN_BLOCKS:]

    @pl.when(pl.program_id(1) == 0)
    def _():
        rstate_ref[...] = jnp.zeros_like(rstate_ref)
        sstate_ref[...] = jnp.zeros_like(sstate_ref)
        xbc_ref[0:SUBLANES, :] = jnp.zeros((SUBLANES, xbc_ref.shape[1]), F32)
        u_ref[0:SUBLANES, :] = jnp.zeros((SUBLANES, u_ref.shape[1]), F32)

    x = x_ref[...]
    d = x.shape[1]
    h = _modulated_norm(x, g_ref[...], scale_ref[...], shift_ref[...])
    blk = FILL_BLOCK

    def proj(dst_ref, first_block, n, act=None):
        return _proj_block(dst_ref, h, win[first_block + n], n * blk, act)

    for n in range(SSM_XBC // blk):
        xbc_ref[SUBLANES:SUBLANES + tile, n * blk:(n + 1) * blk] = _dot(h, win[XBC_BLOCK + n][...])
    n_chunks = tile // CHUNK
    fillers = [_proj_block(dt_ref, h, wdt_ref, 0)]
    fillers += [_ssd_terms_task(c, dt_ref, dtb_ref, alog_ref, term_ref, trans_ref) for c in range(n_chunks)]
    ret_proj = [proj(qkvg_ref, RET_BLOCK, n) for n in range(4 * RET_WIDTH // blk)]
    quad = 4
    head_tasks = [_ssd_head_task(c, list(range(q, q + quad)), term_ref, trans_ref, ld_ref, eb_ref, cd_ref)
                  for c in range(n_chunks) for q in range(0, SSM_HEADS, quad)]
    fillers += [t for duo in zip(ret_proj, head_tasks) for t in duo] + head_tasks[len(ret_proj):]
    _ssd_conv(xbc_ref, xact_ref, convw_ref, convb_ref, tile, fillers)

    def ret_merge():
        acc_ref[...] = gates_ref[:, :d] * _dot(yret_ref[...], wbr_ret_ref[...])

    def sc_merge():
        acc_ref[...] += gates_ref[:, 2 * d:] * _dot(ysc_ref[...], wbr_sc_ref[...])

    dense = [proj(z_ref, Z_BLOCK, n) for n in range(SSM_WIDTH // blk)]
    dense += [proj(cp_ref, SC_BLOCK, n) for n in range(cp_ref.shape[1] // blk)]
    dense += [proj(gates_ref, GATE_BLOCK, n, _sigmoid) for n in range(3 * d // blk)]
    ret_units = [_retention_unit(hd, c, qkvg_ref, cos_ref, sin_ref, rstate_ref, yret_ref)
                 for c in range(tile // CHUNK) for hd in range(RET_HEADS)]
    fillers = [t for duo in zip(dense, ret_units) for t in duo] + dense[len(ret_units):]
    fillers += [ret_merge, _short_conv_task(cp_ref, scw_ref, u_ref, ysc_ref, tile), sc_merge]
    _ssd_scan(xact_ref, z_ref, dskip_ref, sng_ref, term_ref, ld_ref, eb_ref, cd_ref,
              sstate_ref, ychunk_ref, yssm_ref, tile, fillers)
    _run_fillers(fillers, 1)

    merged = acc_ref[...] + gates_ref[:, d:2 * d] * _dot(yssm_ref[...], wbr_ssm_ref[...])
    out = _dot(merged.astype(BF16), wout_ref[...])
    xn = x + rgate_ref[...] * out
    if final:
        ms = jnp.mean(xn * xn, axis=-1, keepdims=True)
        xn = xn * lax.rsqrt(ms + EPS) * fg_ref[...]
    o_ref[...] = xn


def _layer_call(x, ada5, norm_g3, cos2, sin2, weights, smalls, layer, tile, final):
    bsz, seq, d = x.shape
    row_spec = lambda idx: pl.BlockSpec((None, None, None, 1, d), lambda b, j: (layer, b, idx, 0, 0))
    per_layer = lambda a: pl.BlockSpec((None,) + a.shape[1:], lambda b, j: (layer,) + (0,) * (a.ndim - 1),
                                       pipeline_mode=pl.Buffered(1))
    tok = lambda w: pl.BlockSpec((None, tile, w), lambda b, j: (b, j, 0))
    w_blocks, wbr_ret, wbr_ssm, wbr_sc, w_out = weights
    conv_w, conv_b, dt_bias, a_log, d_skip, ssm_ng, sc_w, final_g = smalls
    width = sc_w.shape[2]
    w_in_block = lambda cols, n: pl.BlockSpec((None, d, cols), lambda b, j: (layer, 0, n),
                                              pipeline_mode=pl.Buffered(1))
    dt_block = W_IN_BLOCKS * FILL_BLOCK // DT_PAD
    return pl.pallas_call(
        functools.partial(_layer_kernel, tile=tile, final=final),
        grid=(bsz, seq // tile),
        in_specs=[
            tok(d),
            per_layer(norm_g3),
            row_spec(1),
            row_spec(0),
            row_spec(2),
            tok(RET_DIM), tok(RET_DIM),
        ] + [w_in_block(FILL_BLOCK, n) for n in range(W_IN_BLOCKS)] + [
            w_in_block(DT_PAD, dt_block),
            per_layer(conv_w), per_layer(conv_b), per_layer(dt_bias), per_layer(a_log), per_layer(d_skip),
            per_layer(ssm_ng), per_layer(sc_w),
            per_layer(wbr_ret), per_layer(wbr_ssm), per_layer(wbr_sc), per_layer(w_out),
            pl.BlockSpec(final_g.shape, lambda b, j: (0, 0), pipeline_mode=pl.Buffered(1)),
        ],
        out_specs=tok(d),
        out_shape=jax.ShapeDtypeStruct((bsz, seq, d), F32),
        scratch_shapes=[
            pltpu.VMEM((RET_HEADS, RET_DIM, RET_DIM), F32),
            pltpu.VMEM((tile + SUBLANES, SSM_XBC), F32),
            pltpu.VMEM((SSM_STATE, SSM_WIDTH), F32),
            pltpu.VMEM((CHUNK, SSM_WIDTH), F32),
            pltpu.VMEM((tile + SUBLANES, width), F32),
            pltpu.VMEM((tile, RET_WIDTH), BF16),
            pltpu.VMEM((tile, SSM_WIDTH), BF16),
            pltpu.VMEM((tile, width), BF16),
            pltpu.VMEM((tile, 4 * RET_WIDTH), F32),
            pltpu.VMEM((tile, SSM_WIDTH), F32),
            pltpu.VMEM((tile, DT_PAD), F32),
            pltpu.VMEM((tile, SSM_XBC), F32),
            pltpu.VMEM((tile, 4 * width), F32),
            pltpu.VMEM((tile, 3 * d), F32),
            pltpu.VMEM((tile, d), F32),
            pltpu.VMEM((tile // CHUNK, 3, CHUNK, DT_PAD), F32),
            pltpu.VMEM((tile // CHUNK, 2, DT_PAD, CHUNK), F32),
            pltpu.VMEM((tile // CHUNK * SSM_HEADS, CHUNK, CHUNK), F32),
            pltpu.VMEM((tile // CHUNK * SSM_HEADS, CHUNK, LANES), F32),
            pltpu.VMEM((tile // CHUNK, SSM_HEADS // 2, LANES), F32),
        ],
        compiler_params=pltpu.CompilerParams(
            dimension_semantics=("parallel", "arbitrary"), vmem_limit_bytes=VMEM_LIMIT_BYTES),
        name="layer",
    )(x, norm_g3, ada5, ada5, ada5, cos2, sin2, *([w_blocks] * (W_IN_BLOCKS + 1)),
      conv_w, conv_b, dt_bias, a_log, d_skip, ssm_ng, sc_w, wbr_ret, wbr_ssm, wbr_sc, w_out, final_g)


def kernel(x, c, positions, norm_g, w_ada, b_ada, w_in, ssm_conv_w, ssm_conv_b, ssm_dt_bias, ssm_a_log,
           ssm_d, ssm_norm_g, sc_conv_w, w_br_ret, w_br_ssm, w_br_sc, w_out, final_norm_g):
    bsz, seq, d = x.shape
    depth = w_in.shape[0]
    sc_width = sc_conv_w.shape[2]
    tile = 256
    assert seq % tile == 0 and tile % CHUNK == 0

    sizes = (RET_WIDTH, RET_WIDTH, RET_WIDTH, RET_WIDTH, SSM_WIDTH, SSM_XBC, SSM_HEADS,
             sc_width, sc_width, sc_width, sc_width, 3 * d)
    offs = [0]
    for s in sizes:
        offs.append(offs[-1] + s)
    assert offs[-1] == w_in.shape[2] and d == D_MODEL and sc_width == SC_WIDTH

    ada = _ada_call(c, w_ada, b_ada)
    ada5 = ada.reshape(depth, bsz, 3, 1, d)
    cos2, sin2 = _rope_call(positions)
    norm_g3 = norm_g.reshape(depth, 1, d)

    dt_pad = DT_PAD - SSM_HEADS
    assert offs[6] == SC_BLOCK * FILL_BLOCK
    weights = (_w_in_prep_call(w_in), w_br_ret.astype(BF16), w_br_ssm.astype(BF16), w_br_sc.astype(BF16), w_out.astype(BF16))
    pad_heads = lambda v: jnp.pad(v, ((0, 0), (0, dt_pad))).reshape(depth, 1, DT_PAD)
    smalls = (
        ssm_conv_w, ssm_conv_b.reshape(depth, 1, SSM_XBC),
        pad_heads(ssm_dt_bias), pad_heads(ssm_a_log),
        jnp.repeat(ssm_d, SSM_HEAD_DIM, axis=1).reshape(depth, 1, SSM_WIDTH),
        ssm_norm_g.reshape(depth, 1, SSM_WIDTH), sc_conv_w, final_norm_g.reshape(1, d),
    )
    for layer in range(depth):
        x = _layer_call(x, ada5, norm_g3, cos2, sin2, weights, smalls, layer, tile,
                        final=(layer == depth - 1))
    return x
```

```python
import functools
import math

import jax
import jax.numpy as jnp
from jax import lax
from jax.experimental import pallas as pl
from jax.experimental.pallas import tpu as pltpu

F32 = jnp.float32
BF16 = jnp.bfloat16

EPS = 1e-6
RET_HEADS = 4
RET_DIM = 128
RET_WIDTH = RET_HEADS * RET_DIM
ROPE_THETA = 10000.0
SSM_HEADS = 16
SSM_HEAD_DIM = 64
SSM_GROUPS = 2
SSM_STATE = 128
SSM_CONV = 4
SSM_WIDTH = SSM_HEADS * SSM_HEAD_DIM
SSM_XBC = SSM_WIDTH + 2 * SSM_GROUPS * SSM_STATE
SC_CONV = 3
CHUNK = 128
LANES = 128
SUBLANES = 8
DT_PAD = LANES
FILL_BLOCK = 512
D_MODEL = 1024
SC_WIDTH = D_MODEL // 2
RET_BLOCK = 0
Z_BLOCK = RET_BLOCK + 4 * RET_WIDTH // FILL_BLOCK
XBC_BLOCK = Z_BLOCK + SSM_WIDTH // FILL_BLOCK
SC_BLOCK = XBC_BLOCK + SSM_XBC // FILL_BLOCK
GATE_BLOCK = SC_BLOCK + 4 * SC_WIDTH // FILL_BLOCK
W_IN_BLOCKS = GATE_BLOCK + 3 * D_MODEL // FILL_BLOCK
VMEM_LIMIT_BYTES = 60 * 1024 * 1024

NEG_INF = float("-inf")


def _sigmoid(x):
    return 1.0 / (1.0 + jnp.exp(-x))


def _silu(x):
    return x * _sigmoid(x)


def _dot(a, b):
    return jnp.dot(a, b, preferred_element_type=F32)


def _modulated_norm(x, g, scale, shift):
    ms = jnp.mean(x * x, axis=-1, keepdims=True)
    y = x * lax.rsqrt(ms + EPS) * g
    return (y * (1.0 + scale) + shift).astype(BF16)


def _lane_bcast(a, col):
    return jnp.broadcast_to(a[:, col:col + 1], a.shape)


def _ada_kernel(c_ref, w_ref, b_ref, o_ref):
    act = _silu(c_ref[...])
    hi = act.astype(BF16)
    lo = (act - hi.astype(F32)).astype(BF16)
    w = w_ref[...].astype(BF16)
    o_ref[...] = _dot(hi, w) + _dot(lo, w) + b_ref[...]


def _ada_call(c, w_ada, b_ada):
    depth, d, d3 = w_ada.shape
    bsz = c.shape[0]
    nblk = d3 // d
    return pl.pallas_call(
        _ada_kernel,
        grid=(depth, nblk),
        in_specs=[
            pl.BlockSpec((bsz, d), lambda l, n: (0, 0)),
            pl.BlockSpec((None, d, d), lambda l, n: (l, 0, n)),
            pl.BlockSpec((None, 1, d), lambda l, n: (l, 0, n)),
        ],
        out_specs=pl.BlockSpec((None, bsz, d), lambda l, n: (l, 0, n)),
        out_shape=jax.ShapeDtypeStruct((depth, bsz, d3), F32),
        compiler_params=pltpu.CompilerParams(
            dimension_semantics=("parallel", "parallel"), vmem_limit_bytes=VMEM_LIMIT_BYTES),
        name="ada",
    )(c, w_ada, b_ada.reshape(depth, 1, d3))


def _w_in_prep_kernel(wa_ref, wb_ref, o_ref):
    o_ref[:, :FILL_BLOCK] = wa_ref[0].T.astype(o_ref.dtype)
    o_ref[:, FILL_BLOCK:] = wb_ref[0].T.astype(o_ref.dtype)


def _w_in_prep_call(w_in):
    depth, d, width = w_in.shape
    assert width == W_IN_BLOCKS * FILL_BLOCK + SSM_HEADS and (W_IN_BLOCKS + 1) % 2 == 0
    w_t = jnp.swapaxes(w_in, 1, 2)
    dt_start = SC_BLOCK * FILL_BLOCK
    per_block = FILL_BLOCK // SSM_HEADS
    start = lambda n: SSM_HEADS * jnp.where(n == W_IN_BLOCKS, dt_start // SSM_HEADS,
                                            n * per_block + jnp.where(n >= SC_BLOCK, 1, 0))
    rows = lambda which: pl.BlockSpec((pl.Element(1), pl.Element(FILL_BLOCK), pl.Element(d)),
                                      lambda l, n: (l, start(2 * n + which), 0))
    return pl.pallas_call(
        _w_in_prep_kernel,
        grid=(depth, (W_IN_BLOCKS + 1) // 2),
        in_specs=[rows(0), rows(1)],
        out_specs=pl.BlockSpec((None, d, 2 * FILL_BLOCK), lambda l, n: (l, 0, n)),
        out_shape=jax.ShapeDtypeStruct((depth, d, (W_IN_BLOCKS + 1) * FILL_BLOCK), BF16),
        compiler_params=pltpu.CompilerParams(
            dimension_semantics=("parallel", "parallel"), vmem_limit_bytes=VMEM_LIMIT_BYTES),
        name="w_in_prep",
    )(w_t, w_t)


def _rope_kernel(pos_ref, inv_ref, cos_ref, sin_ref):
    seq = pos_ref.shape[0]
    half_seq, half = seq // 2, RET_DIM // 2
    first = lax.broadcasted_iota(jnp.int32, (half_seq, RET_DIM), 1) < half
    pos = jnp.where(first, pos_ref[0:half_seq, :], pos_ref[half_seq:seq, :])
    ang = pos.astype(F32) * inv_ref[...]
    cs, sn = jnp.cos(ang), jnp.sin(ang)
    cs_sw, sn_sw = pltpu.roll(cs, half, axis=1), pltpu.roll(sn, half, axis=1)
    cos_ref[0:half_seq, :] = jnp.where(first, cs, cs_sw)
    cos_ref[half_seq:seq, :] = jnp.where(first, cs_sw, cs)
    sin_ref[0:half_seq, :] = jnp.where(first, -sn, sn_sw)
    sin_ref[half_seq:seq, :] = jnp.where(first, -sn_sw, sn)


def _rope_call(positions):
    bsz, seq = positions.shape
    half = RET_DIM // 2
    inv_freq = ROPE_THETA ** (-jnp.arange(0, RET_DIM, 2, dtype=F32) / RET_DIM)
    inv2 = jnp.concatenate([inv_freq, inv_freq]).reshape(1, 2 * half)
    tbl = jax.ShapeDtypeStruct((bsz, seq, RET_DIM), F32)
    return pl.pallas_call(
        _rope_kernel,
        grid=(bsz,),
        in_specs=[
            pl.BlockSpec((None, seq, 1), lambda b: (b, 0, 0)),
            pl.BlockSpec((1, RET_DIM), lambda b: (0, 0)),
        ],
        out_specs=[pl.BlockSpec((None, seq, RET_DIM), lambda b: (b, 0, 0))] * 2,
        out_shape=[tbl, tbl],
        compiler_params=pltpu.CompilerParams(
            dimension_semantics=("parallel",), vmem_limit_bytes=VMEM_LIMIT_BYTES),
        name="rope_tables",
    )(positions.reshape(bsz, seq, 1), inv2)


def _run_fillers(fillers, slots_left):
    n = -(-len(fillers) // max(slots_left, 1))
    for _ in range(min(n, len(fillers))):
        fillers.pop(0)()


def _proj_block(dst_ref, h, w_ref, dst_lo, act=None):
    def run():
        r = _dot(h, w_ref[...])
        dst_ref[:, dst_lo:dst_lo + w_ref.shape[1]] = r if act is None else act(r)
    return run


def _retention_unit(hd, c, qkvg_ref, cos_ref, sin_ref, state_ref, y_ref):
    def run():
        row = lax.broadcasted_iota(jnp.int32, (CHUNK, CHUNK), 0)
        col = lax.broadcasted_iota(jnp.int32, (CHUNK, CHUNK), 1)
        rowf = row.astype(F32)
        half = RET_DIM // 2
        log_g = math.log(1.0 - 2.0 ** (-5.0 - hd))
        decay_mask = jnp.exp(jnp.where(row >= col, (row - col).astype(F32) * log_g, NEG_INF))
        q_decay = jnp.exp((rowf + 1.0) * log_g)
        k_decay = jnp.exp((CHUNK - 1.0 - rowf) * log_g)
        chunk_decay = math.exp(CHUNK * log_g)
        rows = slice(c * CHUNK, (c + 1) * CHUNK)
        lo = hd * RET_DIM
        cos = cos_ref[rows, :]
        sin = sin_ref[rows, :]

        def rot(t):
            return t * cos + pltpu.roll(t, half, axis=1) * sin

        qc = (rot(qkvg_ref[rows, lo:lo + RET_DIM]) * (RET_DIM ** -0.5)).astype(BF16)
        kc = rot(qkvg_ref[rows, RET_WIDTH + lo:RET_WIDTH + lo + RET_DIM])
        vc = qkvg_ref[rows, 2 * RET_WIDTH + lo:2 * RET_WIDTH + lo + RET_DIM].astype(BF16)
        gate = qkvg_ref[rows, 3 * RET_WIDTH + lo:3 * RET_WIDTH + lo + RET_DIM]
        state = state_ref[hd]
        scores = lax.dot_general(qc, kc.astype(BF16), (((1,), (1,)), ((), ())), preferred_element_type=F32)
        inner = _dot((scores * decay_mask).astype(BF16), vc)
        cross = _dot(qc, state.astype(BF16)) * q_decay
        kd = (kc * k_decay).astype(BF16)
        state_ref[hd] = chunk_decay * state + lax.dot_general(kd, vc, (((0,), (0,)), ((), ())),
                                                              preferred_element_type=F32)
        o = inner + cross
        mu = jnp.mean(o, axis=-1, keepdims=True)
        oc = o - mu
        var = jnp.mean(oc * oc, axis=-1, keepdims=True)
        y = oc * lax.rsqrt(var + EPS) * _silu(gate)
        y_ref[rows, lo:lo + RET_DIM] = y.astype(y_ref.dtype)
    return run


def _block_diag_pair(slab):
    lane = lax.broadcasted_iota(jnp.int32, slab.shape, 1)
    first = lane < SSM_HEAD_DIM
    top = jnp.where(first, slab, 0.0)
    bot = jnp.where(first, 0.0, slab)
    return jnp.concatenate([top, bot], axis=0).astype(BF16)


def _ssd_conv(xbc_ref, xact_ref, convw_ref, convb_ref, tile, fillers):
    nblk = SSM_XBC // FILL_BLOCK
    for n in range(nblk):
        cols = slice(n * FILL_BLOCK, (n + 1) * FILL_BLOCK)
        conv = convb_ref[:, cols]
        for kk in range(SSM_CONV):
            off = SUBLANES - (SSM_CONV - 1) + kk
            conv = conv + convw_ref[kk:kk + 1, cols] * xbc_ref[off:off + tile, cols]
        xact_ref[:, cols] = _silu(conv)
        _run_fillers(fillers, nblk - n)
    xbc_ref[0:SUBLANES, :] = xbc_ref[tile:tile + SUBLANES, :]


T_ACS, T_WT, T_EACS = 0, 1, 2


def _cumsum_rows(x):
    row = lax.broadcasted_iota(jnp.int32, x.shape, 0)
    k = 1
    while k < x.shape[0]:
        x = x + jnp.where(row >= k, pltpu.roll(x, k, axis=0), 0.0)
        k *= 2
    return x


def _ssd_terms_task(c, dt_ref, dtb_ref, alog_ref, term_ref, trans_ref):
    def run():
        rows = slice(c * CHUNK, (c + 1) * CHUNK)
        dt_in = dt_ref[rows, :] + dtb_ref[...]
        dt_c = jnp.maximum(dt_in, 0.0) + jnp.log1p(jnp.exp(-jnp.abs(dt_in)))
        a_cs = _cumsum_rows(dt_c * (-jnp.exp(alog_ref[...])))
        a_last = a_cs[CHUNK - 1:CHUNK, :]
        term_ref[c, T_ACS] = a_cs
        term_ref[c, T_WT] = (jnp.exp(a_last - a_cs) * dt_c).T
        term_ref[c, T_EACS] = jnp.exp(a_cs)
        trans_ref[c, 0] = a_cs.T
        trans_ref[c, 1] = dt_c.T
    return run


def _ssd_head_task(c, heads, term_ref, trans_ref, ld_ref, eb_ref, cd_ref):
    def run():
        row = lax.broadcasted_iota(jnp.int32, (CHUNK, CHUNK), 0)
        col = lax.broadcasted_iota(jnp.int32, (CHUNK, CHUNK), 1)
        tril = row >= col
        lane1 = lax.broadcasted_iota(jnp.int32, (1, LANES), 1)
        a_cs = term_ref[c, T_ACS]
        e_acs = term_ref[c, T_EACS]
        cdec = e_acs[CHUNK - 1:CHUNK, :]
        for hh in heads:
            seg = _lane_bcast(a_cs, hh) - trans_ref[c, 0, hh:hh + 1, :]
            ld_ref[c * SSM_HEADS + hh] = jnp.exp(jnp.where(tril, seg, NEG_INF)) * trans_ref[c, 1, hh:hh + 1, :]
            eb_ref[c * SSM_HEADS + hh] = _lane_bcast(e_acs, hh)
        for h1 in heads[::2]:
            cd_ref[c, h1 // 2:h1 // 2 + 1, :] = jnp.where(
                lane1 < SSM_HEAD_DIM, _lane_bcast(cdec, h1), _lane_bcast(cdec, h1 + 1))
    return run


def _ssd_scan(xact_ref, z_ref, dskip_ref, ng_ref, term_ref, ld_ref, eb_ref, cd_ref,
              state_ref, ychunk_ref, y_ref, tile, fillers):
    pair = 2 * SSM_HEAD_DIM
    heads_per_group = SSM_HEADS // SSM_GROUPS
    n_chunks = tile // CHUNK
    slots = n_chunks * SSM_HEADS // 2

    for c in range(n_chunks):
        rows = slice(c * CHUNK, (c + 1) * CHUNK)
        for g in range(SSM_GROUPS):
            b_lo = SSM_WIDTH + g * SSM_STATE
            c_lo = SSM_WIDTH + SSM_GROUPS * SSM_STATE + g * SSM_STATE
            b_g = xact_ref[rows, b_lo:b_lo + SSM_STATE]
            c_g = xact_ref[rows, c_lo:c_lo + SSM_STATE]
            b_t = b_g.T
            cb = _dot(c_g.astype(BF16), b_t.astype(BF16))
            for p in range(heads_per_group // 2):
                h1 = g * heads_per_group + 2 * p
                lo = h1 * SSM_HEAD_DIM
                xs_pair = xact_ref[rows, lo:lo + pair]
                rhs = _block_diag_pair(xs_pair)
                m_parts, s_parts, o_parts = [], [], []
                for hh in (h1, h1 + 1):
                    m_parts.append(cb * ld_ref[c * SSM_HEADS + hh])
                    s_parts.append(b_t * term_ref[c, T_WT, hh:hh + 1, :])
                    o_parts.append(c_g * eb_ref[c * SSM_HEADS + hh])
                both = _dot(jnp.concatenate([jnp.concatenate(m_parts, axis=1).astype(BF16),
                                             jnp.concatenate(s_parts, axis=1).astype(BF16)], axis=0), rhs)
                y_diag, s_new = both[:CHUNK], both[CHUNK:]
                s_prev = state_ref[:, lo:lo + pair]
                y_off = _dot(jnp.concatenate(o_parts, axis=1).astype(BF16), _block_diag_pair(s_prev))
                state_ref[:, lo:lo + pair] = cd_ref[c, h1 // 2:h1 // 2 + 1, :] * s_prev + s_new
                ychunk_ref[:, lo:lo + pair] = y_diag + y_off + dskip_ref[:, lo:lo + pair] * xs_pair
                _run_fillers(fillers, slots)
                slots -= 1
        y = ychunk_ref[...] * _silu(z_ref[rows, :])
        ms = jnp.mean(y * y, axis=-1, keepdims=True)
        y_ref[rows, :] = (y * lax.rsqrt(ms + EPS) * ng_ref[...]).astype(y_ref.dtype)


def _short_conv_task(cp_ref, scw_ref, u_ref, y_ref, tile):
    def run():
        width = u_ref.shape[1]
        u_ref[SUBLANES:SUBLANES + tile, :] = cp_ref[:, width:2 * width] * cp_ref[:, 2 * width:3 * width]
        conv = jnp.zeros((tile, width), F32)
        for kk in range(SC_CONV):
            off = SUBLANES - (SC_CONV - 1) + kk
            conv = conv + scw_ref[kk:kk + 1, :] * u_ref[off:off + tile, :]
        u_ref[0:SUBLANES, :] = u_ref[tile:tile + SUBLANES, :]
        y_ref[...] = (cp_ref[:, :width] * conv * _silu(cp_ref[:, 3 * width:])).astype(y_ref.dtype)
    return run


def _layer_kernel(x_ref, g_ref, scale_ref, shift_ref, rgate_ref, cos_ref, sin_ref, *refs, tile, final):
    win = refs[:W_IN_BLOCKS]
    (wdt_ref, convw_ref, convb_ref, dtb_ref, alog_ref, dskip_ref, sng_ref, scw_ref,
     wbr_ret_ref, wbr_ssm_ref, wbr_sc_ref, wout_ref, fg_ref,
     o_ref,
     rstate_ref, xbc_ref, sstate_ref, ychunk_ref, u_ref, yret_ref, yssm_ref, ysc_ref,
     qkvg_ref, z_ref, dt_ref, xact_ref, cp_ref, gates_ref, acc_ref,
     term_ref, trans_ref, ld_ref, eb_ref, cd_ref) = refs[W_IN_BLOCKS:]

    @pl.when(pl.program_id(1) == 0)
    def _():
        rstate_ref[...] = jnp.zeros_like(rstate_ref)
        sstate_ref[...] = jnp.zeros_like(sstate_ref)
        xbc_ref[0:SUBLANES, :] = jnp.zeros((SUBLANES, xbc_ref.shape[1]), F32)
        u_ref[0:SUBLANES, :] = jnp.zeros((SUBLANES, u_ref.shape[1]), F32)

    d = x_ref.shape[1]
    h = _modulated_norm(x_ref[...], g_ref[...], scale_ref[...], shift_ref[...])
    blk = FILL_BLOCK

    def proj(dst_ref, first_block, n, act=None):
        return _proj_block(dst_ref, h, win[first_block + n], n * blk, act)

    for n in range(SSM_XBC // blk):
        xbc_ref[SUBLANES:SUBLANES + tile, n * blk:(n + 1) * blk] = _dot(h, win[XBC_BLOCK + n][...])
    n_chunks = tile // CHUNK
    fillers = [_proj_block(dt_ref, h, wdt_ref, 0)]
    fillers += [_ssd_terms_task(c, dt_ref, dtb_ref, alog_ref, term_ref, trans_ref) for c in range(n_chunks)]
    ret_proj = [proj(qkvg_ref, RET_BLOCK, n) for n in range(4 * RET_WIDTH // blk)]
    quad = 4
    head_tasks = [_ssd_head_task(c, list(range(q, q + quad)), term_ref, trans_ref, ld_ref, eb_ref, cd_ref)
                  for c in range(n_chunks) for q in range(0, SSM_HEADS, quad)]
    fillers += [t for duo in zip(ret_proj, head_tasks) for t in duo] + head_tasks[len(ret_proj):]
    _ssd_conv(xbc_ref, xact_ref, convw_ref, convb_ref, tile, fillers)

    def ret_merge():
        acc_ref[...] = gates_ref[:, :d] * _dot(yret_ref[...], wbr_ret_ref[...])

    def sc_merge():
        acc_ref[...] += gates_ref[:, 2 * d:] * _dot(ysc_ref[...], wbr_sc_ref[...])

    dense = [proj(z_ref, Z_BLOCK, n) for n in range(SSM_WIDTH // blk)]
    dense += [proj(cp_ref, SC_BLOCK, n) for n in range(cp_ref.shape[1] // blk)]
    dense += [proj(gates_ref, GATE_BLOCK, n, _sigmoid) for n in range(3 * d // blk)]
    ret_units = [_retention_unit(hd, c, qkvg_ref, cos_ref, sin_ref, rstate_ref, yret_ref)
                 for c in range(tile // CHUNK) for hd in range(RET_HEADS)]
    fillers = [t for duo in zip(dense, ret_units) for t in duo] + dense[len(ret_units):]
    fillers += [ret_merge, _short_conv_task(cp_ref, scw_ref, u_ref, ysc_ref, tile), sc_merge]
    _ssd_scan(xact_ref, z_ref, dskip_ref, sng_ref, term_ref, ld_ref, eb_ref, cd_ref,
              sstate_ref, ychunk_ref, yssm_ref, tile, fillers)
    _run_fillers(fillers, 1)

    merged = acc_ref[...] + gates_ref[:, d:2 * d] * _dot(yssm_ref[...], wbr_ssm_ref[...])
    out = _dot(merged.astype(BF16), wout_ref[...])
    xn = x_ref[...] + rgate_ref[...] * out
    if final:
        ms = jnp.mean(xn * xn, axis=-1, keepdims=True)
        xn = xn * lax.rsqrt(ms + EPS) * fg_ref[...]
    o_ref[...] = xn


def _layer_call(x, ada5, norm_g3, cos2, sin2, weights, smalls, layer, tile, final):
    bsz, seq, d = x.shape
    row_spec = lambda idx: pl.BlockSpec((None, None, None, 1, d), lambda b, j: (layer, b, idx, 0, 0))
    per_layer = lambda a: pl.BlockSpec((None,) + a.shape[1:], lambda b, j: (layer,) + (0,) * (a.ndim - 1),
                                       pipeline_mode=pl.Buffered(1))
    tok = lambda w: pl.BlockSpec((None, tile, w), lambda b, j: (b, j, 0))
    w_blocks, wbr_ret, wbr_ssm, wbr_sc, w_out = weights
    conv_w, conv_b, dt_bias, a_log, d_skip, ssm_ng, sc_w, final_g = smalls
    width = sc_w.shape[2]
    w_in_block = lambda cols, n: pl.BlockSpec((None, d, cols), lambda b, j: (layer, 0, n),
                                              pipeline_mode=pl.Buffered(1))
    dt_block = W_IN_BLOCKS * FILL_BLOCK // DT_PAD
    return pl.pallas_call(
        functools.partial(_layer_kernel, tile=tile, final=final),
        grid=(bsz, seq // tile),
        in_specs=[
            tok(d),
            per_layer(norm_g3),
            row_spec(1),
            row_spec(0),
            row_spec(2),
            tok(RET_DIM), tok(RET_DIM),
        ] + [w_in_block(FILL_BLOCK, n) for n in range(W_IN_BLOCKS)] + [
            w_in_block(DT_PAD, dt_block),
            per_layer(conv_w), per_layer(conv_b), per_layer(dt_bias), per_layer(a_log), per_layer(d_skip),
            per_layer(ssm_ng), per_layer(sc_w),
            per_layer(wbr_ret), per_layer(wbr_ssm), per_layer(wbr_sc), per_layer(w_out),
            pl.BlockSpec(final_g.shape, lambda b, j: (0, 0), pipeline_mode=pl.Buffered(1)),
        ],
        out_specs=tok(d),
        out_shape=jax.ShapeDtypeStruct((bsz, seq, d), F32),
        scratch_shapes=[
            pltpu.VMEM((RET_HEADS, RET_DIM, RET_DIM), F32),
            pltpu.VMEM((tile + SUBLANES, SSM_XBC), F32),
            pltpu.VMEM((SSM_STATE, SSM_WIDTH), F32),
            pltpu.VMEM((CHUNK, SSM_WIDTH), F32),
            pltpu.VMEM((tile + SUBLANES, width), F32),
            pltpu.VMEM((tile, RET_WIDTH), BF16),
            pltpu.VMEM((tile, SSM_WIDTH), BF16),
            pltpu.VMEM((tile, width), BF16),
            pltpu.VMEM((tile, 4 * RET_WIDTH), F32),
            pltpu.VMEM((tile, SSM_WIDTH), F32),
            pltpu.VMEM((tile, DT_PAD), F32),
            pltpu.VMEM((tile, SSM_XBC), F32),
            pltpu.VMEM((tile, 4 * width), F32),
            pltpu.VMEM((tile, 3 * d), F32),
            pltpu.VMEM((tile, d), F32),
            pltpu.VMEM((tile // CHUNK, 3, CHUNK, DT_PAD), F32),
            pltpu.VMEM((tile // CHUNK, 2, DT_PAD, CHUNK), F32),
            pltpu.VMEM((tile // CHUNK * SSM_HEADS, CHUNK, CHUNK), F32),
            pltpu.VMEM((tile // CHUNK * SSM_HEADS, CHUNK, LANES), F32),
            pltpu.VMEM((tile // CHUNK, SSM_HEADS // 2, LANES), F32),
        ],
        compiler_params=pltpu.CompilerParams(
            dimension_semantics=("parallel", "arbitrary"), vmem_limit_bytes=VMEM_LIMIT_BYTES),
        name="layer",
    )(x, norm_g3, ada5, ada5, ada5, cos2, sin2, *([w_blocks] * (W_IN_BLOCKS + 1)),
      conv_w, conv_b, dt_bias, a_log, d_skip, ssm_ng, sc_w, wbr_ret, wbr_ssm, wbr_sc, w_out, final_g)


def kernel(x, c, positions, norm_g, w_ada, b_ada, w_in, ssm_conv_w, ssm_conv_b, ssm_dt_bias, ssm_a_log,
           ssm_d, ssm_norm_g, sc_conv_w, w_br_ret, w_br_ssm, w_br_sc, w_out, final_norm_g):
    bsz, seq, d = x.shape
    depth = w_in.shape[0]
    sc_width = sc_conv_w.shape[2]
    tile = 256
    assert seq % tile == 0 and tile % CHUNK == 0

    sizes = (RET_WIDTH, RET_WIDTH, RET_WIDTH, RET_WIDTH, SSM_WIDTH, SSM_XBC, SSM_HEADS,
             sc_width, sc_width, sc_width, sc_width, 3 * d)
    offs = [0]
    for s in sizes:
        offs.append(offs[-1] + s)
    assert offs[-1] == w_in.shape[2] and d == D_MODEL and sc_width == SC_WIDTH

    ada = _ada_call(c, w_ada, b_ada)
    ada5 = ada.reshape(depth, bsz, 3, 1, d)
    cos2, sin2 = _rope_call(positions)
    norm_g3 = norm_g.reshape(depth, 1, d)

    dt_pad = DT_PAD - SSM_HEADS
    assert offs[6] == SC_BLOCK * FILL_BLOCK
    weights = (_w_in_prep_call(w_in), w_br_ret.astype(BF16), w_br_ssm.astype(BF16), w_br_sc.astype(BF16), w_out.astype(BF16))
    pad_heads = lambda v: jnp.pad(v, ((0, 0), (0, dt_pad))).reshape(depth, 1, DT_PAD)
    smalls = (
        ssm_conv_w, ssm_conv_b.reshape(depth, 1, SSM_XBC),
        pad_heads(ssm_dt_bias), pad_heads(ssm_a_log),
        jnp.repeat(ssm_d, SSM_HEAD_DIM, axis=1).reshape(depth, 1, SSM_WIDTH),
        ssm_norm_g.reshape(depth, 1, SSM_WIDTH), sc_conv_w, final_norm_g.reshape(1, d),
    )
    for layer in range(depth):
        x = _layer_call(x, ada5, norm_g3, cos2, sin2, weights, smalls, layer, tile,
                        final=(layer == depth - 1))
    return x
```

```python
import functools
import math

import jax
import jax.numpy as jnp
from jax import lax
from jax.experimental import pallas as pl
from jax.experimental.pallas import tpu as pltpu

F32 = jnp.float32
BF16 = jnp.bfloat16

EPS = 1e-6
RET_HEADS = 4
RET_DIM = 128
RET_WIDTH = RET_HEADS * RET_DIM
ROPE_THETA = 10000.0
SSM_HEADS = 16
SSM_HEAD_DIM = 64
SSM_GROUPS = 2
SSM_STATE = 128
SSM_CONV = 4
SSM_WIDTH = SSM_HEADS * SSM_HEAD_DIM
SSM_XBC = SSM_WIDTH + 2 * SSM_GROUPS * SSM_STATE
SC_CONV = 3
CHUNK = 128
LANES = 128
SUBLANES = 8
DT_PAD = LANES
FILL_BLOCK = 512
D_MODEL = 1024
SC_WIDTH = D_MODEL // 2
RET_BLOCK = 0
Z_BLOCK = RET_BLOCK + 4 * RET_WIDTH // FILL_BLOCK
XBC_BLOCK = Z_BLOCK + SSM_WIDTH // FILL_BLOCK
SC_BLOCK = XBC_BLOCK + SSM_XBC // FILL_BLOCK
GATE_BLOCK = SC_BLOCK + 4 * SC_WIDTH // FILL_BLOCK
W_IN_BLOCKS = GATE_BLOCK + 3 * D_MODEL // FILL_BLOCK
VMEM_LIMIT_BYTES = 60 * 1024 * 1024

NEG_INF = float("-inf")


def _sigmoid(x):
    return 1.0 / (1.0 + jnp.exp(-x))


def _silu(x):
    return x * _sigmoid(x)


def _dot(a, b):
    return jnp.dot(a, b, preferred_element_type=F32)


def _modulated_norm(x, g, scale, shift):
    ms = jnp.mean(x * x, axis=-1, keepdims=True)
    y = x * lax.rsqrt(ms + EPS) * g
    return (y * (1.0 + scale) + shift).astype(BF16)


def _lane_bcast(a, col):
    return jnp.broadcast_to(a[:, col:col + 1], a.shape)


def _ada_kernel(c_ref, w_ref, b_ref, o_ref):
    act = _silu(c_ref[...])
    hi = act.astype(BF16)
    lo = (act - hi.astype(F32)).astype(BF16)
    w = w_ref[...].astype(BF16)
    o_ref[...] = _dot(hi, w) + _dot(lo, w) + b_ref[...]


def _ada_call(c, w_ada, b_ada):
    depth, d, d3 = w_ada.shape
    bsz = c.shape[0]
    nblk = d3 // d
    return pl.pallas_call(
        _ada_kernel,
        grid=(depth, nblk),
        in_specs=[
            pl.BlockSpec((bsz, d), lambda l, n: (0, 0)),
            pl.BlockSpec((None, d, d), lambda l, n: (l, 0, n)),
            pl.BlockSpec((None, 1, d), lambda l, n: (l, 0, n)),
        ],
        out_specs=pl.BlockSpec((None, bsz, d), lambda l, n: (l, 0, n)),
        out_shape=jax.ShapeDtypeStruct((depth, bsz, d3), F32),
        compiler_params=pltpu.CompilerParams(
            dimension_semantics=("parallel", "parallel"), vmem_limit_bytes=VMEM_LIMIT_BYTES),
        name="ada",
    )(c, w_ada, b_ada.reshape(depth, 1, d3))


def _w_in_prep_kernel(wa_ref, wb_ref, o_ref):
    o_ref[:, :FILL_BLOCK] = wa_ref[0].T.astype(o_ref.dtype)
    o_ref[:, FILL_BLOCK:] = wb_ref[0].T.astype(o_ref.dtype)


def _w_in_prep_call(w_in):
    depth, d, width = w_in.shape
    assert width == W_IN_BLOCKS * FILL_BLOCK + SSM_HEADS and (W_IN_BLOCKS + 1) % 2 == 0
    w_t = jnp.swapaxes(w_in, 1, 2)
    dt_start = SC_BLOCK * FILL_BLOCK
    per_block = FILL_BLOCK // SSM_HEADS
    start = lambda n: SSM_HEADS * jnp.where(n == W_IN_BLOCKS, dt_start // SSM_HEADS,
                                            n * per_block + jnp.where(n >= SC_BLOCK, 1, 0))
    rows = lambda which: pl.BlockSpec((pl.Element(1), pl.Element(FILL_BLOCK), pl.Element(d)),
                                      lambda l, n: (l, start(2 * n + which), 0))
    return pl.pallas_call(
        _w_in_prep_kernel,
        grid=(depth, (W_IN_BLOCKS + 1) // 2),
        in_specs=[rows(0), rows(1)],
        out_specs=pl.BlockSpec((None, d, 2 * FILL_BLOCK), lambda l, n: (l, 0, n)),
        out_shape=jax.ShapeDtypeStruct((depth, d, (W_IN_BLOCKS + 1) * FILL_BLOCK), BF16),
        compiler_params=pltpu.CompilerParams(
            dimension_semantics=("parallel", "parallel"), vmem_limit_bytes=VMEM_LIMIT_BYTES),
        name="w_in_prep",
    )(w_t, w_t)


def _rope_kernel(pos_ref, inv_ref, cos_ref, sin_ref):
    seq = pos_ref.shape[0]
    half_seq, half = seq // 2, RET_DIM // 2
    first = lax.broadcasted_iota(jnp.int32, (half_seq, RET_DIM), 1) < half
    pos = jnp.where(first, pos_ref[0:half_seq, :], pos_ref[half_seq:seq, :])
    ang = pos.astype(F32) * inv_ref[...]
    cs, sn = jnp.cos(ang), jnp.sin(ang)
    cs_sw, sn_sw = pltpu.roll(cs, half, axis=1), pltpu.roll(sn, half, axis=1)
    cos_ref[0:half_seq, :] = jnp.where(first, cs, cs_sw)
    cos_ref[half_seq:seq, :] = jnp.where(first, cs_sw, cs)
    sin_ref[0:half_seq, :] = jnp.where(first, -sn, sn_sw)
    sin_ref[half_seq:seq, :] = jnp.where(first, -sn_sw, sn)


def _rope_call(positions):
    bsz, seq = positions.shape
    half = RET_DIM // 2
    inv_freq = ROPE_THETA ** (-jnp.arange(0, RET_DIM, 2, dtype=F32) / RET_DIM)
    inv2 = jnp.concatenate([inv_freq, inv_freq]).reshape(1, 2 * half)
    tbl = jax.ShapeDtypeStruct((bsz, seq, RET_DIM), F32)
    return pl.pallas_call(
        _rope_kernel,
        grid=(bsz,),
        in_specs=[
            pl.BlockSpec((None, seq, 1), lambda b: (b, 0, 0)),
            pl.BlockSpec((1, RET_DIM), lambda b: (0, 0)),
        ],
        out_specs=[pl.BlockSpec((None, seq, RET_DIM), lambda b: (b, 0, 0))] * 2,
        out_shape=[tbl, tbl],
        compiler_params=pltpu.CompilerParams(
            dimension_semantics=("parallel",), vmem_limit_bytes=VMEM_LIMIT_BYTES),
        name="rope_tables",
    )(positions.reshape(bsz, seq, 1), inv2)


def _run_fillers(fillers, slots_left):
    n = -(-len(fillers) // max(slots_left, 1))
    for _ in range(min(n, len(fillers))):
        fillers.pop(0)()


def _interleave(a, b):
    paired = [t for duo in zip(a, b) for t in duo]
    return paired + a[len(b):] + b[len(a):]


def _proj_block(dst_ref, h, w_ref, dst_lo, act=None):
    def run():
        r = _dot(h, w_ref[...])
        dst_ref[:, dst_lo:dst_lo + w_ref.shape[1]] = r if act is None else act(r)
    return run


def _retention_unit(hd, c, qkvg_ref, cos_ref, sin_ref, state_ref, y_ref):
    def run():
        row = lax.broadcasted_iota(jnp.int32, (CHUNK, CHUNK), 0)
        col = lax.broadcasted_iota(jnp.int32, (CHUNK, CHUNK), 1)
        rowf = row.astype(F32)
        half = RET_DIM // 2
        log_g = math.log(1.0 - 2.0 ** (-5.0 - hd))
        decay_mask = jnp.exp(jnp.where(row >= col, (row - col).astype(F32) * log_g, NEG_INF))
        q_decay = jnp.exp((rowf + 1.0) * log_g)
        k_decay = jnp.exp((CHUNK - 1.0 - rowf) * log_g)
        chunk_decay = math.exp(CHUNK * log_g)
        rows = slice(c * CHUNK, (c + 1) * CHUNK)
        lo = hd * RET_DIM
        cos = cos_ref[rows, :]
        sin = sin_ref[rows, :]

        def rot(t):
            return t * cos + pltpu.roll(t, half, axis=1) * sin

        qc = (rot(qkvg_ref[rows, lo:lo + RET_DIM]) * (RET_DIM ** -0.5)).astype(BF16)
        kc = rot(qkvg_ref[rows, RET_WIDTH + lo:RET_WIDTH + lo + RET_DIM])
        vc = qkvg_ref[rows, 2 * RET_WIDTH + lo:2 * RET_WIDTH + lo + RET_DIM].astype(BF16)
        gate = qkvg_ref[rows, 3 * RET_WIDTH + lo:3 * RET_WIDTH + lo + RET_DIM]
        state = state_ref[hd]
        scores = lax.dot_general(qc, kc.astype(BF16), (((1,), (1,)), ((), ())), preferred_element_type=F32)
        inner = _dot((scores * decay_mask).astype(BF16), vc)
        cross = _dot(qc, state.astype(BF16)) * q_decay
        kd = (kc * k_decay).astype(BF16)
        state_ref[hd] = chunk_decay * state + lax.dot_general(kd, vc, (((0,), (0,)), ((), ())),
                                                              preferred_element_type=F32)
        o = inner + cross
        mu = jnp.mean(o, axis=-1, keepdims=True)
        oc = o - mu
        var = jnp.mean(oc * oc, axis=-1, keepdims=True)
        y = oc * lax.rsqrt(var + EPS) * _silu(gate)
        y_ref[rows, lo:lo + RET_DIM] = y.astype(y_ref.dtype)
    return run


def _block_diag_pair(slab):
    lane = lax.broadcasted_iota(jnp.int32, slab.shape, 1)
    first = lane < SSM_HEAD_DIM
    top = jnp.where(first, slab, 0.0)
    bot = jnp.where(first, 0.0, slab)
    return jnp.concatenate([top, bot], axis=0).astype(BF16)


def _ssd_conv(xbc_ref, xact_ref, convw_ref, convb_ref, tile, fillers):
    nblk = SSM_XBC // FILL_BLOCK
    for n in range(nblk):
        cols = slice(n * FILL_BLOCK, (n + 1) * FILL_BLOCK)
        conv = convb_ref[:, cols]
        for kk in range(SSM_CONV):
            off = SUBLANES - (SSM_CONV - 1) + kk
            conv = conv + convw_ref[kk:kk + 1, cols] * xbc_ref[off:off + tile, cols]
        xact_ref[:, cols] = _silu(conv)
        _run_fillers(fillers, nblk - n)
    xbc_ref[0:SUBLANES, :] = xbc_ref[tile:tile + SUBLANES, :]


T_ACS, T_WT, T_EACS = 0, 1, 2


def _cumsum_rows(x):
    row = lax.broadcasted_iota(jnp.int32, x.shape, 0)
    k = 1
    while k < x.shape[0]:
        x = x + jnp.where(row >= k, pltpu.roll(x, k, axis=0), 0.0)
        k *= 2
    return x


def _ssd_terms_task(c, dt_ref, dtb_ref, alog_ref, term_ref, trans_ref):
    def run():
        rows = slice(c * CHUNK, (c + 1) * CHUNK)
        dt_in = dt_ref[rows, :] + dtb_ref[...]
        dt_c = jnp.maximum(dt_in, 0.0) + jnp.log1p(jnp.exp(-jnp.abs(dt_in)))
        a_cs = _cumsum_rows(dt_c * (-jnp.exp(alog_ref[...])))
        a_last = a_cs[CHUNK - 1:CHUNK, :]
        term_ref[c, T_ACS] = a_cs
        term_ref[c, T_WT] = (jnp.exp(a_last - a_cs) * dt_c).T
        term_ref[c, T_EACS] = jnp.exp(a_cs)
        trans_ref[c, 0] = a_cs.T
        trans_ref[c, 1] = dt_c.T
    return run


def _ssd_head_task(c, heads, term_ref, trans_ref, ld_ref, eb_ref, cd_ref):
    def run():
        row = lax.broadcasted_iota(jnp.int32, (CHUNK, CHUNK), 0)
        col = lax.broadcasted_iota(jnp.int32, (CHUNK, CHUNK), 1)
        tril = row >= col
        lane1 = lax.broadcasted_iota(jnp.int32, (1, LANES), 1)
        a_cs = term_ref[c, T_ACS]
        e_acs = term_ref[c, T_EACS]
        cdec = e_acs[CHUNK - 1:CHUNK, :]
        for hh in heads:
            seg = _lane_bcast(a_cs, hh) - trans_ref[c, 0, hh:hh + 1, :]
            ld_ref[c * SSM_HEADS + hh] = jnp.exp(jnp.where(tril, seg, NEG_INF)) * trans_ref[c, 1, hh:hh + 1, :]
            eb_ref[c * SSM_HEADS + hh] = _lane_bcast(e_acs, hh)
        for h1 in heads[::2]:
            cd_ref[c, h1 // 2:h1 // 2 + 1, :] = jnp.where(
                lane1 < SSM_HEAD_DIM, _lane_bcast(cdec, h1), _lane_bcast(cdec, h1 + 1))
    return run


def _ssd_scan(xact_ref, z_ref, dskip_ref, ng_ref, term_ref, ld_ref, eb_ref, cd_ref,
              state_ref, ychunk_ref, y_ref, tile, fillers):
    pair = 2 * SSM_HEAD_DIM
    heads_per_group = SSM_HEADS // SSM_GROUPS
    n_chunks = tile // CHUNK
    slots = n_chunks * SSM_HEADS // 2

    for c in range(n_chunks):
        rows = slice(c * CHUNK, (c + 1) * CHUNK)
        for g in range(SSM_GROUPS):
            b_lo = SSM_WIDTH + g * SSM_STATE
            c_lo = SSM_WIDTH + SSM_GROUPS * SSM_STATE + g * SSM_STATE
            b_g = xact_ref[rows, b_lo:b_lo + SSM_STATE]
            c_g = xact_ref[rows, c_lo:c_lo + SSM_STATE]
            b_t = b_g.T
            cb = _dot(c_g.astype(BF16), b_t.astype(BF16))
            for p in range(heads_per_group // 2):
                h1 = g * heads_per_group + 2 * p
                lo = h1 * SSM_HEAD_DIM
                xs_pair = xact_ref[rows, lo:lo + pair]
                rhs = _block_diag_pair(xs_pair)
                m_parts, s_parts, o_parts = [], [], []
                for hh in (h1, h1 + 1):
                    m_parts.append(cb * ld_ref[c * SSM_HEADS + hh])
                    s_parts.append(b_t * term_ref[c, T_WT, hh:hh + 1, :])
                    o_parts.append(c_g * eb_ref[c * SSM_HEADS + hh])
                y_diag = _dot(jnp.concatenate(m_parts, axis=1).astype(BF16), rhs)
                s_new = _dot(jnp.concatenate(s_parts, axis=1).astype(BF16), rhs)
                s_prev = state_ref[:, lo:lo + pair]
                y_off = _dot(jnp.concatenate(o_parts, axis=1).astype(BF16), _block_diag_pair(s_prev))
                state_ref[:, lo:lo + pair] = cd_ref[c, h1 // 2:h1 // 2 + 1, :] * s_prev + s_new
                ychunk_ref[:, lo:lo + pair] = y_diag + y_off + dskip_ref[:, lo:lo + pair] * xs_pair
                _run_fillers(fillers, slots)
                slots -= 1
        y = ychunk_ref[...] * _silu(z_ref[rows, :])
        ms = jnp.mean(y * y, axis=-1, keepdims=True)
        y_ref[rows, :] = (y * lax.rsqrt(ms + EPS) * ng_ref[...]).astype(y_ref.dtype)


def _short_conv_task(cp_ref, scw_ref, u_ref, y_ref, tile):
    def run():
        width = u_ref.shape[1]
        u_ref[SUBLANES:SUBLANES + tile, :] = cp_ref[:, width:2 * width] * cp_ref[:, 2 * width:3 * width]
        conv = jnp.zeros((tile, width), F32)
        for kk in range(SC_CONV):
            off = SUBLANES - (SC_CONV - 1) + kk
            conv = conv + scw_ref[kk:kk + 1, :] * u_ref[off:off + tile, :]
        u_ref[0:SUBLANES, :] = u_ref[tile:tile + SUBLANES, :]
        y_ref[...] = (cp_ref[:, :width] * conv * _silu(cp_ref[:, 3 * width:])).astype(y_ref.dtype)
    return run


def _layer_kernel(x_ref, g_ref, scale_ref, shift_ref, rgate_ref, cos_ref, sin_ref, *refs, tile, final):
    win = refs[:W_IN_BLOCKS]
    (wdt_ref, convw_ref, convb_ref, dtb_ref, alog_ref, dskip_ref, sng_ref, scw_ref,
     wbr_ret_ref, wbr_ssm_ref, wbr_sc_ref, wout_ref, fg_ref,
     o_ref,
     rstate_ref, xbc_ref, sstate_ref, ychunk_ref, u_ref, yret_ref, yssm_ref, ysc_ref,
     qkvg_ref, z_ref, dt_ref, xact_ref, cp_ref, gates_ref, acc_ref,
     term_ref, trans_ref, ld_ref, eb_ref, cd_ref) = refs[W_IN_BLOCKS:]

    @pl.when(pl.program_id(1) == 0)
    def _():
        rstate_ref[...] = jnp.zeros_like(rstate_ref)
        sstate_ref[...] = jnp.zeros_like(sstate_ref)
        xbc_ref[0:SUBLANES, :] = jnp.zeros((SUBLANES, xbc_ref.shape[1]), F32)
        u_ref[0:SUBLANES, :] = jnp.zeros((SUBLANES, u_ref.shape[1]), F32)

    x = x_ref[...]
    d = x.shape[1]
    h = _modulated_norm(x, g_ref[...], scale_ref[...], shift_ref[...])
    blk = FILL_BLOCK

    def proj(dst_ref, first_block, n, act=None):
        return _proj_block(dst_ref, h, win[first_block + n], n * blk, act)

    for n in range(SSM_XBC // blk):
        xbc_ref[SUBLANES:SUBLANES + tile, n * blk:(n + 1) * blk] = _dot(h, win[XBC_BLOCK + n][...])
    n_chunks = tile // CHUNK
    fillers = [_proj_block(dt_ref, h, wdt_ref, 0)]
    fillers += [_ssd_terms_task(c, dt_ref, dtb_ref, alog_ref, term_ref, trans_ref) for c in range(n_chunks)]
    ret_proj = [proj(qkvg_ref, RET_BLOCK, n) for n in range(4 * RET_WIDTH // blk)]
    quad = 4
    head_tasks = [_ssd_head_task(c, list(range(q, q + quad)), term_ref, trans_ref, ld_ref, eb_ref, cd_ref)
                  for c in range(n_chunks) for q in range(0, SSM_HEADS, quad)]
    fillers += _interleave(ret_proj, head_tasks)
    _ssd_conv(xbc_ref, xact_ref, convw_ref, convb_ref, tile, fillers)

    def ret_merge():
        acc_ref[...] = gates_ref[:, :d] * _dot(yret_ref[...], wbr_ret_ref[...])

    def sc_merge():
        acc_ref[...] += gates_ref[:, 2 * d:] * _dot(ysc_ref[...], wbr_sc_ref[...])

    dense = [proj(z_ref, Z_BLOCK, n) for n in range(SSM_WIDTH // blk)]
    dense += [proj(cp_ref, SC_BLOCK, n) for n in range(cp_ref.shape[1] // blk)]
    dense += [proj(gates_ref, GATE_BLOCK, n, _sigmoid) for n in range(3 * d // blk)]
    ret_units = [_retention_unit(hd, c, qkvg_ref, cos_ref, sin_ref, rstate_ref, yret_ref)
                 for c in range(tile // CHUNK) for hd in range(RET_HEADS)]
    fillers = _interleave(dense, ret_units)
    fillers += [ret_merge, _short_conv_task(cp_ref, scw_ref, u_ref, ysc_ref, tile), sc_merge]
    _ssd_scan(xact_ref, z_ref, dskip_ref, sng_ref, term_ref, ld_ref, eb_ref, cd_ref,
              sstate_ref, ychunk_ref, yssm_ref, tile, fillers)
    _run_fillers(fillers, 1)

    merged = acc_ref[...] + gates_ref[:, d:2 * d] * _dot(yssm_ref[...], wbr_ssm_ref[...])
    out = _dot(merged.astype(BF16), wout_ref[...])
    xn = x + rgate_ref[...] * out
    if final:
        ms = jnp.mean(xn * xn, axis=-1, keepdims=True)
        xn = xn * lax.rsqrt(ms + EPS) * fg_ref[...]
    o_ref[...] = xn


def _layer_call(x, ada5, norm_g3, cos2, sin2, weights, smalls, layer, tile, final):
    bsz, seq, d = x.shape
    row_spec = lambda idx: pl.BlockSpec((None, None, None, 1, d), lambda b, j: (layer, b, idx, 0, 0))
    per_layer = lambda a: pl.BlockSpec((None,) + a.shape[1:], lambda b, j: (layer,) + (0,) * (a.ndim - 1),
                                       pipeline_mode=pl.Buffered(1))
    tok = lambda w: pl.BlockSpec((None, tile, w), lambda b, j: (b, j, 0))
    w_blocks, wbr_ret, wbr_ssm, wbr_sc, w_out = weights
    conv_w, conv_b, dt_bias, a_log, d_skip, ssm_ng, sc_w, final_g = smalls
    width = sc_w.shape[2]
    w_in_block = lambda cols, n: pl.BlockSpec((None, d, cols), lambda b, j: (layer, 0, n),
                                              pipeline_mode=pl.Buffered(1))
    dt_block = W_IN_BLOCKS * FILL_BLOCK // DT_PAD
    return pl.pallas_call(
        functools.partial(_layer_kernel, tile=tile, final=final),
        grid=(bsz, seq // tile),
        in_specs=[
            tok(d),
            per_layer(norm_g3),
            row_spec(1),
            row_spec(0),
            row_spec(2),
            tok(RET_DIM), tok(RET_DIM),
        ] + [w_in_block(FILL_BLOCK, n) for n in range(W_IN_BLOCKS)] + [
            w_in_block(DT_PAD, dt_block),
            per_layer(conv_w), per_layer(conv_b), per_layer(dt_bias), per_layer(a_log), per_layer(d_skip),
            per_layer(ssm_ng), per_layer(sc_w),
            per_layer(wbr_ret), per_layer(wbr_ssm), per_layer(wbr_sc), per_layer(w_out),
            pl.BlockSpec(final_g.shape, lambda b, j: (0, 0), pipeline_mode=pl.Buffered(1)),
        ],
        out_specs=tok(d),
        out_shape=jax.ShapeDtypeStruct((bsz, seq, d), F32),
        scratch_shapes=[
            pltpu.VMEM((RET_HEADS, RET_DIM, RET_DIM), F32),
            pltpu.VMEM((tile + SUBLANES, SSM_XBC), F32),
            pltpu.VMEM((SSM_STATE, SSM_WIDTH), F32),
            pltpu.VMEM((CHUNK, SSM_WIDTH), F32),
            pltpu.VMEM((tile + SUBLANES, width), F32),
            pltpu.VMEM((tile, RET_WIDTH), BF16),
            pltpu.VMEM((tile, SSM_WIDTH), BF16),
            pltpu.VMEM((tile, width), BF16),
            pltpu.VMEM((tile, 4 * RET_WIDTH), F32),
            pltpu.VMEM((tile, SSM_WIDTH), F32),
            pltpu.VMEM((tile, DT_PAD), F32),
            pltpu.VMEM((tile, SSM_XBC), F32),
            pltpu.VMEM((tile, 4 * width), F32),
            pltpu.VMEM((tile, 3 * d), F32),
            pltpu.VMEM((tile, d), F32),
            pltpu.VMEM((tile // CHUNK, 3, CHUNK, DT_PAD), F32),
            pltpu.VMEM((tile // CHUNK, 2, DT_PAD, CHUNK), F32),
            pltpu.VMEM((tile // CHUNK * SSM_HEADS, CHUNK, CHUNK), F32),
            pltpu.VMEM((tile // CHUNK * SSM_HEADS, CHUNK, LANES), F32),
            pltpu.VMEM((tile // CHUNK, SSM_HEADS // 2, LANES), F32),
        ],
        compiler_params=pltpu.CompilerParams(
            dimension_semantics=("parallel", "arbitrary"), vmem_limit_bytes=VMEM_LIMIT_BYTES),
        name="layer",
    )(x, norm_g3, ada5, ada5, ada5, cos2, sin2, *([w_blocks] * (W_IN_BLOCKS + 1)),
      conv_w, conv_b, dt_bias, a_log, d_skip, ssm_ng, sc_w, wbr_ret, wbr_ssm, wbr_sc, w_out, final_g)


def kernel(x, c, positions, norm_g, w_ada, b_ada, w_in, ssm_conv_w, ssm_conv_b, ssm_dt_bias, ssm_a_log,
           ssm_d, ssm_norm_g, sc_conv_w, w_br_ret, w_br_ssm, w_br_sc, w_out, final_norm_g):
    bsz, seq, d = x.shape
    depth = w_in.shape[0]
    sc_width = sc_conv_w.shape[2]
    tile = 512
    assert seq % tile == 0 and tile % CHUNK == 0

    sizes = (RET_WIDTH, RET_WIDTH, RET_WIDTH, RET_WIDTH, SSM_WIDTH, SSM_XBC, SSM_HEADS,
             sc_width, sc_width, sc_width, sc_width, 3 * d)
    offs = [0]
    for s in sizes:
        offs.append(offs[-1] + s)
    assert offs[-1] == w_in.shape[2] and d == D_MODEL and sc_width == SC_WIDTH

    ada = _ada_call(c, w_ada, b_ada)
    ada5 = ada.reshape(depth, bsz, 3, 1, d)
    cos2, sin2 = _rope_call(positions)
    norm_g3 = norm_g.reshape(depth, 1, d)

    dt_pad = DT_PAD - SSM_HEADS
    assert offs[6] == SC_BLOCK * FILL_BLOCK
    weights = (_w_in_prep_call(w_in), w_br_ret.astype(BF16), w_br_ssm.astype(BF16), w_br_sc.astype(BF16), w_out.astype(BF16))
    pad_heads = lambda v: jnp.pad(v, ((0, 0), (0, dt_pad))).reshape(depth, 1, DT_PAD)
    smalls = (
        ssm_conv_w, ssm_conv_b.reshape(depth, 1, SSM_XBC),
        pad_heads(ssm_dt_bias), pad_heads(ssm_a_log),
        jnp.repeat(ssm_d, SSM_HEAD_DIM, axis=1).reshape(depth, 1, SSM_WIDTH),
        ssm_norm_g.reshape(depth, 1, SSM_WIDTH), sc_conv_w, final_norm_g.reshape(1, d),
    )
    for layer in range(depth):
        x = _layer_call(x, ada5, norm_g3, cos2, sin2, weights, smalls, layer, tile,
                        final=(layer == depth - 1))
    return x
```

```python
import functools
import math

import jax
import jax.numpy as jnp
from jax import lax
from jax.experimental import pallas as pl
from jax.experimental.pallas import tpu as pltpu

F32 = jnp.float32
BF16 = jnp.bfloat16

EPS = 1e-6
RET_HEADS = 4
RET_DIM = 128
RET_WIDTH = RET_HEADS * RET_DIM
ROPE_THETA = 10000.0
SSM_HEADS = 16
SSM_HEAD_DIM = 64
SSM_GROUPS = 2
SSM_STATE = 128
SSM_CONV = 4
SSM_WIDTH = SSM_HEADS * SSM_HEAD_DIM
SSM_XBC = SSM_WIDTH + 2 * SSM_GROUPS * SSM_STATE
SC_CONV = 3
CHUNK = 128
LANES = 128
SUBLANES = 8
DT_PAD = LANES
FILL_BLOCK = 512
D_MODEL = 1024
SC_WIDTH = D_MODEL // 2
RET_BLOCK = 0
Z_BLOCK = RET_BLOCK + 4 * RET_WIDTH // FILL_BLOCK
XBC_BLOCK = Z_BLOCK + SSM_WIDTH // FILL_BLOCK
SC_BLOCK = XBC_BLOCK + SSM_XBC // FILL_BLOCK
GATE_BLOCK = SC_BLOCK + 4 * SC_WIDTH // FILL_BLOCK
W_IN_BLOCKS = GATE_BLOCK + 3 * D_MODEL // FILL_BLOCK
VMEM_LIMIT_BYTES = 60 * 1024 * 1024

NEG_INF = float("-inf")


def _sigmoid(x):
    return 1.0 / (1.0 + jnp.exp(-x))


def _silu(x):
    return x * _sigmoid(x)


def _dot(a, b):
    return jnp.dot(a, b, preferred_element_type=F32)


def _modulated_norm(x, g, scale, shift):
    ms = jnp.mean(x * x, axis=-1, keepdims=True)
    y = x * lax.rsqrt(ms + EPS) * g
    return (y * (1.0 + scale) + shift).astype(BF16)


def _lane_bcast(a, col):
    return jnp.broadcast_to(a[:, col:col + 1], a.shape)


def _ada_kernel(c_ref, w_ref, b_ref, o_ref):
    act = _silu(c_ref[...])
    hi = act.astype(BF16)
    lo = (act - hi.astype(F32)).astype(BF16)
    w = w_ref[...].astype(BF16)
    o_ref[...] = _dot(hi, w) + _dot(lo, w) + b_ref[...]


def _ada_call(c, w_ada, b_ada):
    depth, d, d3 = w_ada.shape
    bsz = c.shape[0]
    nblk = d3 // d
    return pl.pallas_call(
        _ada_kernel,
        grid=(depth, nblk),
        in_specs=[
            pl.BlockSpec((bsz, d), lambda l, n: (0, 0)),
            pl.BlockSpec((None, d, d), lambda l, n: (l, 0, n)),
            pl.BlockSpec((None, 1, d), lambda l, n: (l, 0, n)),
        ],
        out_specs=pl.BlockSpec((None, bsz, d), lambda l, n: (l, 0, n)),
        out_shape=jax.ShapeDtypeStruct((depth, bsz, d3), F32),
        compiler_params=pltpu.CompilerParams(
            dimension_semantics=("parallel", "parallel"), vmem_limit_bytes=VMEM_LIMIT_BYTES),
        name="ada",
    )(c, w_ada, b_ada.reshape(depth, 1, d3))


def _w_in_prep_kernel(wa_ref, wb_ref, o_ref):
    o_ref[:, :FILL_BLOCK] = wa_ref[0].T.astype(o_ref.dtype)
    o_ref[:, FILL_BLOCK:] = wb_ref[0].T.astype(o_ref.dtype)


def _w_in_prep_call(w_in):
    depth, d, width = w_in.shape
    assert width == W_IN_BLOCKS * FILL_BLOCK + SSM_HEADS and (W_IN_BLOCKS + 1) % 2 == 0
    w_t = jnp.swapaxes(w_in, 1, 2)
    dt_start = SC_BLOCK * FILL_BLOCK
    per_block = FILL_BLOCK // SSM_HEADS
    start = lambda n: SSM_HEADS * jnp.where(n == W_IN_BLOCKS, dt_start // SSM_HEADS,
                                            n * per_block + jnp.where(n >= SC_BLOCK, 1, 0))
    rows = lambda which: pl.BlockSpec((pl.Element(1), pl.Element(FILL_BLOCK), pl.Element(d)),
                                      lambda l, n: (l, start(2 * n + which), 0))
    return pl.pallas_call(
        _w_in_prep_kernel,
        grid=(depth, (W_IN_BLOCKS + 1) // 2),
        in_specs=[rows(0), rows(1)],
        out_specs=pl.BlockSpec((None, d, 2 * FILL_BLOCK), lambda l, n: (l, 0, n)),
        out_shape=jax.ShapeDtypeStruct((depth, d, (W_IN_BLOCKS + 1) * FILL_BLOCK), BF16),
        compiler_params=pltpu.CompilerParams(
            dimension_semantics=("parallel", "parallel"), vmem_limit_bytes=VMEM_LIMIT_BYTES),
        name="w_in_prep",
    )(w_t, w_t)


def _rope_kernel(pos_ref, inv_ref, cos_ref, sin_ref):
    seq = pos_ref.shape[0]
    half_seq, half = seq // 2, RET_DIM // 2
    first = lax.broadcasted_iota(jnp.int32, (half_seq, RET_DIM), 1) < half
    pos = jnp.where(first, pos_ref[0:half_seq, :], pos_ref[half_seq:seq, :])
    ang = pos.astype(F32) * inv_ref[...]
    cs, sn = jnp.cos(ang), jnp.sin(ang)
    cs_sw, sn_sw = pltpu.roll(cs, half, axis=1), pltpu.roll(sn, half, axis=1)
    cos_ref[0:half_seq, :] = jnp.where(first, cs, cs_sw)
    cos_ref[half_seq:seq, :] = jnp.where(first, cs_sw, cs)
    sin_ref[0:half_seq, :] = jnp.where(first, -sn, sn_sw)
    sin_ref[half_seq:seq, :] = jnp.where(first, -sn_sw, sn)


def _rope_call(positions):
    bsz, seq = positions.shape
    half = RET_DIM // 2
    inv_freq = ROPE_THETA ** (-jnp.arange(0, RET_DIM, 2, dtype=F32) / RET_DIM)
    inv2 = jnp.concatenate([inv_freq, inv_freq]).reshape(1, 2 * half)
    tbl = jax.ShapeDtypeStruct((bsz, seq, RET_DIM), F32)
    return pl.pallas_call(
        _rope_kernel,
        grid=(bsz,),
        in_specs=[
            pl.BlockSpec((None, seq, 1), lambda b: (b, 0, 0)),
            pl.BlockSpec((1, RET_DIM), lambda b: (0, 0)),
        ],
        out_specs=[pl.BlockSpec((None, seq, RET_DIM), lambda b: (b, 0, 0))] * 2,
        out_shape=[tbl, tbl],
        compiler_params=pltpu.CompilerParams(
            dimension_semantics=("parallel",), vmem_limit_bytes=VMEM_LIMIT_BYTES),
        name="rope_tables",
    )(positions.reshape(bsz, seq, 1), inv2)


def _run_fillers(fillers, slots_left):
    n = -(-len(fillers) // max(slots_left, 1))
    for _ in range(min(n, len(fillers))):
        fillers.pop(0)()


def _interleave(a, b):
    paired = [t for duo in zip(a, b) for t in duo]
    return paired + a[len(b):] + b[len(a):]


def _proj_block(dst_ref, h, w_ref, dst_lo, act=None):
    def run():
        r = _dot(h, w_ref[...])
        dst_ref[:, dst_lo:dst_lo + w_ref.shape[1]] = r if act is None else act(r)
    return run


def _retention_unit(hd, c, qkvg_ref, cos_ref, sin_ref, state_ref, y_ref):
    def run():
        row = lax.broadcasted_iota(jnp.int32, (CHUNK, CHUNK), 0)
        col = lax.broadcasted_iota(jnp.int32, (CHUNK, CHUNK), 1)
        rowf = row.astype(F32)
        half = RET_DIM // 2
        log_g = math.log(1.0 - 2.0 ** (-5.0 - hd))
        decay_mask = jnp.exp(jnp.where(row >= col, (row - col).astype(F32) * log_g, NEG_INF))
        q_decay = jnp.exp((rowf + 1.0) * log_g)
        k_decay = jnp.exp((CHUNK - 1.0 - rowf) * log_g)
        chunk_decay = math.exp(CHUNK * log_g)
        rows = slice(c * CHUNK, (c + 1) * CHUNK)
        lo = hd * RET_DIM
        cos = cos_ref[rows, :]
        sin = sin_ref[rows, :]

        def rot(t):
            return t * cos + pltpu.roll(t, half, axis=1) * sin

        qc = (rot(qkvg_ref[rows, lo:lo + RET_DIM]) * (RET_DIM ** -0.5)).astype(BF16)
        kc = rot(qkvg_ref[rows, RET_WIDTH + lo:RET_WIDTH + lo + RET_DIM])
        vc = qkvg_ref[rows, 2 * RET_WIDTH + lo:2 * RET_WIDTH + lo + RET_DIM].astype(BF16)
        gate = qkvg_ref[rows, 3 * RET_WIDTH + lo:3 * RET_WIDTH + lo + RET_DIM]
        state = state_ref[hd]
        scores = lax.dot_general(qc, kc.astype(BF16), (((1,), (1,)), ((), ())), preferred_element_type=F32)
        inner = _dot((scores * decay_mask).astype(BF16), vc)
        cross = _dot(qc, state.astype(BF16)) * q_decay
        kd = (kc * k_decay).astype(BF16)
        state_ref[hd] = chunk_decay * state + lax.dot_general(kd, vc, (((0,), (0,)), ((), ())),
                                                              preferred_element_type=F32)
        o = inner + cross
        mu = jnp.mean(o, axis=-1, keepdims=True)
        oc = o - mu
        var = jnp.mean(oc * oc, axis=-1, keepdims=True)
        y = oc * lax.rsqrt(var + EPS) * _silu(gate)
        y_ref[rows, lo:lo + RET_DIM] = y.astype(y_ref.dtype)
    return run


def _block_diag_pair(slab):
    lane = lax.broadcasted_iota(jnp.int32, slab.shape, 1)
    first = lane < SSM_HEAD_DIM
    top = jnp.where(first, slab, 0.0)
    bot = jnp.where(first, 0.0, slab)
    return jnp.concatenate([top, bot], axis=0).astype(BF16)


def _ssd_conv(xbc_ref, xact_ref, convw_ref, convb_ref, tile, fillers):
    nblk = SSM_XBC // FILL_BLOCK
    for n in range(nblk):
        cols = slice(n * FILL_BLOCK, (n + 1) * FILL_BLOCK)
        conv = convb_ref[:, cols]
        for kk in range(SSM_CONV):
            off = SUBLANES - (SSM_CONV - 1) + kk
            conv = conv + convw_ref[kk:kk + 1, cols] * xbc_ref[off:off + tile, cols]
        xact_ref[:, cols] = _silu(conv)
        _run_fillers(fillers, nblk - n)
    xbc_ref[0:SUBLANES, :] = xbc_ref[tile:tile + SUBLANES, :]


T_ACS, T_WT, T_EACS = 0, 1, 2


def _cumsum_rows(x):
    row = lax.broadcasted_iota(jnp.int32, x.shape, 0)
    k = 1
    while k < x.shape[0]:
        x = x + jnp.where(row >= k, pltpu.roll(x, k, axis=0), 0.0)
        k *= 2
    return x


def _ssd_terms_task(c, dt_ref, dtb_ref, alog_ref, term_ref, trans_ref):
    def run():
        rows = slice(c * CHUNK, (c + 1) * CHUNK)
        dt_in = dt_ref[rows, :] + dtb_ref[...]
        dt_c = jnp.maximum(dt_in, 0.0) + jnp.log1p(jnp.exp(-jnp.abs(dt_in)))
        a_cs = _cumsum_rows(dt_c * (-jnp.exp(alog_ref[...])))
        a_last = a_cs[CHUNK - 1:CHUNK, :]
        term_ref[c, T_ACS] = a_cs
        term_ref[c, T_WT] = (jnp.exp(a_last - a_cs) * dt_c).T
        term_ref[c, T_EACS] = jnp.exp(a_cs)
        trans_ref[c, 0] = a_cs.T
        trans_ref[c, 1] = dt_c.T
    return run


def _ssd_head_task(c, heads, term_ref, trans_ref, ld_ref, eb_ref, cd_ref):
    def run():
        row = lax.broadcasted_iota(jnp.int32, (CHUNK, CHUNK), 0)
        col = lax.broadcasted_iota(jnp.int32, (CHUNK, CHUNK), 1)
        tril = row >= col
        lane1 = lax.broadcasted_iota(jnp.int32, (1, LANES), 1)
        a_cs = term_ref[c, T_ACS]
        e_acs = term_ref[c, T_EACS]
        cdec = e_acs[CHUNK - 1:CHUNK, :]
        for hh in heads:
            seg = _lane_bcast(a_cs, hh) - trans_ref[c, 0, hh:hh + 1, :]
            ld_ref[c * SSM_HEADS + hh] = jnp.exp(jnp.where(tril, seg, NEG_INF)) * trans_ref[c, 1, hh:hh + 1, :]
            eb_ref[c * SSM_HEADS + hh] = _lane_bcast(e_acs, hh)
        for h1 in heads[::2]:
            cd_ref[c, h1 // 2:h1 // 2 + 1, :] = jnp.where(
                lane1 < SSM_HEAD_DIM, _lane_bcast(cdec, h1), _lane_bcast(cdec, h1 + 1))
    return run


def _ssd_scan(xact_ref, z_ref, dskip_ref, ng_ref, term_ref, ld_ref, eb_ref, cd_ref,
              state_ref, ychunk_ref, y_ref, tile, fillers):
    pair = 2 * SSM_HEAD_DIM
    heads_per_group = SSM_HEADS // SSM_GROUPS
    n_chunks = tile // CHUNK
    slots = n_chunks * SSM_HEADS // 2

    for c in range(n_chunks):
        rows = slice(c * CHUNK, (c + 1) * CHUNK)
        for g in range(SSM_GROUPS):
            b_lo = SSM_WIDTH + g * SSM_STATE
            c_lo = SSM_WIDTH + SSM_GROUPS * SSM_STATE + g * SSM_STATE
            b_g = xact_ref[rows, b_lo:b_lo + SSM_STATE]
            c_g = xact_ref[rows, c_lo:c_lo + SSM_STATE]
            b_t = b_g.T
            cb = _dot(c_g.astype(BF16), b_t.astype(BF16))
            for p in range(heads_per_group // 2):
                h1 = g * heads_per_group + 2 * p
                lo = h1 * SSM_HEAD_DIM
                xs_pair = xact_ref[rows, lo:lo + pair]
                rhs = _block_diag_pair(xs_pair)
                m_parts, s_parts, o_parts = [], [], []
                for hh in (h1, h1 + 1):
                    m_parts.append(cb * ld_ref[c * SSM_HEADS + hh])
                    s_parts.append(b_t * term_ref[c, T_WT, hh:hh + 1, :])
                    o_parts.append(c_g * eb_ref[c * SSM_HEADS + hh])
                y_diag = _dot(jnp.concatenate(m_parts, axis=1).astype(BF16), rhs)
                s_new = _dot(jnp.concatenate(s_parts, axis=1).astype(BF16), rhs)
                s_prev = state_ref[:, lo:lo + pair]
                y_off = _dot(jnp.concatenate(o_parts, axis=1).astype(BF16), _block_diag_pair(s_prev))
                state_ref[:, lo:lo + pair] = cd_ref[c, h1 // 2:h1 // 2 + 1, :] * s_prev + s_new
                ychunk_ref[:, lo:lo + pair] = y_diag + y_off + dskip_ref[:, lo:lo + pair] * xs_pair
                _run_fillers(fillers, slots)
                slots -= 1
        y = ychunk_ref[...] * _silu(z_ref[rows, :])
        ms = jnp.mean(y * y, axis=-1, keepdims=True)
        y_ref[rows, :] = (y * lax.rsqrt(ms + EPS) * ng_ref[...]).astype(y_ref.dtype)


def _short_conv_task(cp_ref, scw_ref, u_ref, y_ref, tile):
    def run():
        width = u_ref.shape[1]
        u_ref[SUBLANES:SUBLANES + tile, :] = cp_ref[:, width:2 * width] * cp_ref[:, 2 * width:3 * width]
        conv = jnp.zeros((tile, width), F32)
        for kk in range(SC_CONV):
            off = SUBLANES - (SC_CONV - 1) + kk
            conv = conv + scw_ref[kk:kk + 1, :] * u_ref[off:off + tile, :]
        u_ref[0:SUBLANES, :] = u_ref[tile:tile + SUBLANES, :]
        y_ref[...] = (cp_ref[:, :width] * conv * _silu(cp_ref[:, 3 * width:])).astype(y_ref.dtype)
    return run


def _layer_kernel(x_ref, g_ref, scale_ref, shift_ref, rgate_ref, cos_ref, sin_ref, *refs, tile, final):
    win = refs[:W_IN_BLOCKS]
    (wdt_ref, convw_ref, convb_ref, dtb_ref, alog_ref, dskip_ref, sng_ref, scw_ref,
     wbr_ret_ref, wbr_ssm_ref, wbr_sc_ref, wout_ref, fg_ref,
     o_ref,
     rstate_ref, xbc_ref, sstate_ref, ychunk_ref, u_ref, yret_ref, yssm_ref, ysc_ref,
     qkvg_ref, z_ref, dt_ref, xact_ref, cp_ref, gates_ref, acc_ref,
     term_ref, trans_ref, ld_ref, eb_ref, cd_ref) = refs[W_IN_BLOCKS:]

    @pl.when(pl.program_id(1) == 0)
    def _():
        rstate_ref[...] = jnp.zeros_like(rstate_ref)
        sstate_ref[...] = jnp.zeros_like(sstate_ref)
        xbc_ref[0:SUBLANES, :] = jnp.zeros((SUBLANES, xbc_ref.shape[1]), F32)
        u_ref[0:SUBLANES, :] = jnp.zeros((SUBLANES, u_ref.shape[1]), F32)

    d = x_ref.shape[1]
    h = _modulated_norm(x_ref[...], g_ref[...], scale_ref[...], shift_ref[...])
    blk = FILL_BLOCK

    def proj(dst_ref, first_block, n, act=None):
        return _proj_block(dst_ref, h, win[first_block + n], n * blk, act)

    for n in range(SSM_XBC // blk):
        xbc_ref[SUBLANES:SUBLANES + tile, n * blk:(n + 1) * blk] = _dot(h, win[XBC_BLOCK + n][...])
    n_chunks = tile // CHUNK
    fillers = [_proj_block(dt_ref, h, wdt_ref, 0)]
    fillers += [_ssd_terms_task(c, dt_ref, dtb_ref, alog_ref, term_ref, trans_ref) for c in range(n_chunks)]
    ret_proj = [proj(qkvg_ref, RET_BLOCK, n) for n in range(4 * RET_WIDTH // blk)]
    quad = 4
    head_tasks = [_ssd_head_task(c, list(range(q, q + quad)), term_ref, trans_ref, ld_ref, eb_ref, cd_ref)
                  for c in range(n_chunks) for q in range(0, SSM_HEADS, quad)]
    fillers += _interleave(ret_proj, head_tasks)
    _ssd_conv(xbc_ref, xact_ref, convw_ref, convb_ref, tile, fillers)

    def ret_merge():
        acc_ref[...] = gates_ref[:, :d] * _dot(yret_ref[...], wbr_ret_ref[...])

    def sc_merge():
        acc_ref[...] += gates_ref[:, 2 * d:] * _dot(ysc_ref[...], wbr_sc_ref[...])

    dense = [proj(z_ref, Z_BLOCK, n) for n in range(SSM_WIDTH // blk)]
    dense += [proj(cp_ref, SC_BLOCK, n) for n in range(cp_ref.shape[1] // blk)]
    dense += [proj(gates_ref, GATE_BLOCK, n, _sigmoid) for n in range(3 * d // blk)]
    ret_units = [_retention_unit(hd, c, qkvg_ref, cos_ref, sin_ref, rstate_ref, yret_ref)
                 for c in range(tile // CHUNK) for hd in range(RET_HEADS)]
    fillers = _interleave(dense, ret_units)
    fillers += [ret_merge, _short_conv_task(cp_ref, scw_ref, u_ref, ysc_ref, tile), sc_merge]
    _ssd_scan(xact_ref, z_ref, dskip_ref, sng_ref, term_ref, ld_ref, eb_ref, cd_ref,
              sstate_ref, ychunk_ref, yssm_ref, tile, fillers)
    _run_fillers(fillers, 1)

    merged = acc_ref[...] + gates_ref[:, d:2 * d] * _dot(yssm_ref[...], wbr_ssm_ref[...])
    out = _dot(merged.astype(BF16), wout_ref[...])
    xn = x_ref[...] + rgate_ref[...] * out
    if final:
        ms = jnp.mean(xn * xn, axis=-1, keepdims=True)
        xn = xn * lax.rsqrt(ms + EPS) * fg_ref[...]
    o_ref[...] = xn


def _layer_call(x, ada5, norm_g3, cos2, sin2, weights, smalls, layer, tile, final):
    bsz, seq, d = x.shape
    row_spec = lambda idx: pl.BlockSpec((None, None, None, 1, d), lambda b, j: (layer, b, idx, 0, 0))
    per_layer = lambda a: pl.BlockSpec((None,) + a.shape[1:], lambda b, j: (layer,) + (0,) * (a.ndim - 1),
                                       pipeline_mode=pl.Buffered(1))
    tok = lambda w: pl.BlockSpec((None, tile, w), lambda b, j: (b, j, 0))
    w_blocks, wbr_ret, wbr_ssm, wbr_sc, w_out = weights
    conv_w, conv_b, dt_bias, a_log, d_skip, ssm_ng, sc_w, final_g = smalls
    width = sc_w.shape[2]
    w_in_block = lambda cols, n: pl.BlockSpec((None, d, cols), lambda b, j: (layer, 0, n),
                                              pipeline_mode=pl.Buffered(1))
    dt_block = W_IN_BLOCKS * FILL_BLOCK // DT_PAD
    return pl.pallas_call(
        functools.partial(_layer_kernel, tile=tile, final=final),
        grid=(bsz, seq // tile),
        in_specs=[
            tok(d),
            per_layer(norm_g3),
            row_spec(1),
            row_spec(0),
            row_spec(2),
            tok(RET_DIM), tok(RET_DIM),
        ] + [w_in_block(FILL_BLOCK, n) for n in range(W_IN_BLOCKS)] + [
            w_in_block(DT_PAD, dt_block),
            per_layer(conv_w), per_layer(conv_b), per_layer(dt_bias), per_layer(a_log), per_layer(d_skip),
            per_layer(ssm_ng), per_layer(sc_w),
            per_layer(wbr_ret), per_layer(wbr_ssm), per_layer(wbr_sc), per_layer(w_out),
            pl.BlockSpec(final_g.shape, lambda b, j: (0, 0), pipeline_mode=pl.Buffered(1)),
        ],
        out_specs=tok(d),
        out_shape=jax.ShapeDtypeStruct((bsz, seq, d), F32),
        scratch_shapes=[
            pltpu.VMEM((RET_HEADS, RET_DIM, RET_DIM), F32),
            pltpu.VMEM((tile + SUBLANES, SSM_XBC), F32),
            pltpu.VMEM((SSM_STATE, SSM_WIDTH), F32),
            pltpu.VMEM((CHUNK, SSM_WIDTH), F32),
            pltpu.VMEM((tile + SUBLANES, width), F32),
            pltpu.VMEM((tile, RET_WIDTH), BF16),
            pltpu.VMEM((tile, SSM_WIDTH), BF16),
            pltpu.VMEM((tile, width), BF16),
            pltpu.VMEM((tile, 4 * RET_WIDTH), F32),
            pltpu.VMEM((tile, SSM_WIDTH), F32),
            pltpu.VMEM((tile, DT_PAD), F32),
            pltpu.VMEM((tile, SSM_XBC), F32),
            pltpu.VMEM((tile, 4 * width), F32),
            pltpu.VMEM((tile, 3 * d), F32),
            pltpu.VMEM((tile, d), F32),
            pltpu.VMEM((tile // CHUNK, 3, CHUNK, DT_PAD), F32),
            pltpu.VMEM((tile // CHUNK, 2, DT_PAD, CHUNK), F32),
            pltpu.VMEM((tile // CHUNK * SSM_HEADS, CHUNK, CHUNK), F32),
            pltpu.VMEM((tile // CHUNK * SSM_HEADS, CHUNK, LANES), F32),
            pltpu.VMEM((tile // CHUNK, SSM_HEADS // 2, LANES), F32),
        ],
        compiler_params=pltpu.CompilerParams(
            dimension_semantics=("parallel", "arbitrary"), vmem_limit_bytes=VMEM_LIMIT_BYTES),
        name="layer",
    )(x, norm_g3, ada5, ada5, ada5, cos2, sin2, *([w_blocks] * (W_IN_BLOCKS + 1)),
      conv_w, conv_b, dt_bias, a_log, d_skip, ssm_ng, sc_w, wbr_ret, wbr_ssm, wbr_sc, w_out, final_g)


def kernel(x, c, positions, norm_g, w_ada, b_ada, w_in, ssm_conv_w, ssm_conv_b, ssm_dt_bias, ssm_a_log,
           ssm_d, ssm_norm_g, sc_conv_w, w_br_ret, w_br_ssm, w_br_sc, w_out, final_norm_g):
    bsz, seq, d = x.shape
    depth = w_in.shape[0]
    sc_width = sc_conv_w.shape[2]
    tile = 256
    assert seq % tile == 0 and tile % CHUNK == 0

    sizes = (RET_WIDTH, RET_WIDTH, RET_WIDTH, RET_WIDTH, SSM_WIDTH, SSM_XBC, SSM_HEADS,
             sc_width, sc_width, sc_width, sc_width, 3 * d)
    offs = [0]
    for s in sizes:
        offs.append(offs[-1] + s)
    assert offs[-1] == w_in.shape[2] and d == D_MODEL and sc_width == SC_WIDTH

    ada = _ada_call(c, w_ada, b_ada)
    ada5 = ada.reshape(depth, bsz, 3, 1, d)
    cos2, sin2 = _rope_call(positions)
    norm_g3 = norm_g.reshape(depth, 1, d)

    dt_pad = DT_PAD - SSM_HEADS
    assert offs[6] == SC_BLOCK * FILL_BLOCK
    weights = (_w_in_prep_call(w_in), w_br_ret.astype(BF16), w_br_ssm.astype(BF16), w_br_sc.astype(BF16), w_out.astype(BF16))
    pad_heads = lambda v: jnp.pad(v, ((0, 0), (0, dt_pad))).reshape(depth, 1, DT_PAD)
    smalls = (
        ssm_conv_w, ssm_conv_b.reshape(depth, 1, SSM_XBC),
        pad_heads(ssm_dt_bias), pad_heads(ssm_a_log),
        jnp.repeat(ssm_d, SSM_HEAD_DIM, axis=1).reshape(depth, 1, SSM_WIDTH),
        ssm_norm_g.reshape(depth, 1, SSM_WIDTH), sc_conv_w, final_norm_g.reshape(1, d),
    )
    for layer in range(depth):
        x = _layer_call(x, ada5, norm_g3, cos2, sin2, weights, smalls, layer, tile,
                        final=(layer == depth - 1))
    return x
```

```python
import functools
import math

import jax
import jax.numpy as jnp
from jax import lax
from jax.experimental import pallas as pl
from jax.experimental.pallas import tpu as pltpu

F32 = jnp.float32
BF16 = jnp.bfloat16

EPS = 1e-6
RET_HEADS = 4
RET_DIM = 128
RET_WIDTH = RET_HEADS * RET_DIM
ROPE_THETA = 10000.0
SSM_HEADS = 16
SSM_HEAD_DIM = 64
SSM_GROUPS = 2
SSM_STATE = 128
SSM_CONV = 4
SSM_WIDTH = SSM_HEADS * SSM_HEAD_DIM
SSM_XBC = SSM_WIDTH + 2 * SSM_GROUPS * SSM_STATE
SC_CONV = 3
CHUNK = 128
LANES = 128
SUBLANES = 8
DT_PAD = LANES
FILL_BLOCK = 512
D_MODEL = 1024
SC_WIDTH = D_MODEL // 2
RET_BLOCK = 0
Z_BLOCK = RET_BLOCK + 4 * RET_WIDTH // FILL_BLOCK
XBC_BLOCK = Z_BLOCK + SSM_WIDTH // FILL_BLOCK
SC_BLOCK = XBC_BLOCK + SSM_XBC // FILL_BLOCK
GATE_BLOCK = SC_BLOCK + 4 * SC_WIDTH // FILL_BLOCK
W_IN_BLOCKS = GATE_BLOCK + 3 * D_MODEL // FILL_BLOCK
VMEM_LIMIT_BYTES = 60 * 1024 * 1024

NEG_INF = float("-inf")


def _sigmoid(x):
    return 1.0 / (1.0 + jnp.exp(-x))


def _silu(x):
    return x * _sigmoid(x)


def _dot(a, b):
    return jnp.dot(a, b, preferred_element_type=F32)


def _modulated_norm(x, g, scale, shift):
    ms = jnp.mean(x * x, axis=-1, keepdims=True)
    y = x * lax.rsqrt(ms + EPS) * g
    return (y * (1.0 + scale) + shift).astype(BF16)


def _lane_bcast(a, col):
    return jnp.broadcast_to(a[:, col:col + 1], a.shape)


def _ada_kernel(c_ref, w_ref, b_ref, o_ref):
    act = _silu(c_ref[...])
    hi = act.astype(BF16)
    lo = (act - hi.astype(F32)).astype(BF16)
    w = w_ref[...].astype(BF16)
    o_ref[...] = _dot(hi, w) + _dot(lo, w) + b_ref[...]


def _ada_call(c, w_ada, b_ada):
    depth, d, d3 = w_ada.shape
    bsz = c.shape[0]
    nblk = d3 // d
    return pl.pallas_call(
        _ada_kernel,
        grid=(depth, nblk),
        in_specs=[
            pl.BlockSpec((bsz, d), lambda l, n: (0, 0)),
            pl.BlockSpec((None, d, d), lambda l, n: (l, 0, n)),
            pl.BlockSpec((None, 1, d), lambda l, n: (l, 0, n)),
        ],
        out_specs=pl.BlockSpec((None, bsz, d), lambda l, n: (l, 0, n)),
        out_shape=jax.ShapeDtypeStruct((depth, bsz, d3), F32),
        compiler_params=pltpu.CompilerParams(
            dimension_semantics=("parallel", "parallel"), vmem_limit_bytes=VMEM_LIMIT_BYTES),
        name="ada",
    )(c, w_ada, b_ada.reshape(depth, 1, d3))


def _w_in_prep_kernel(wa_ref, wb_ref, o_ref):
    o_ref[:, :FILL_BLOCK] = wa_ref[0].T.astype(o_ref.dtype)
    o_ref[:, FILL_BLOCK:] = wb_ref[0].T.astype(o_ref.dtype)


def _w_in_prep_call(w_in):
    depth, d, width = w_in.shape
    assert width == W_IN_BLOCKS * FILL_BLOCK + SSM_HEADS and (W_IN_BLOCKS + 1) % 2 == 0
    w_t = jnp.swapaxes(w_in, 1, 2)
    dt_start = SC_BLOCK * FILL_BLOCK
    per_block = FILL_BLOCK // SSM_HEADS
    start = lambda n: SSM_HEADS * jnp.where(n == W_IN_BLOCKS, dt_start // SSM_HEADS,
                                            n * per_block + jnp.where(n >= SC_BLOCK, 1, 0))
    rows = lambda which: pl.BlockSpec((pl.Element(1), pl.Element(FILL_BLOCK), pl.Element(d)),
                                      lambda l, n: (l, start(2 * n + which), 0))
    return pl.pallas_call(
        _w_in_prep_kernel,
        grid=(depth, (W_IN_BLOCKS + 1) // 2),
        in_specs=[rows(0), rows(1)],
        out_specs=pl.BlockSpec((None, d, 2 * FILL_BLOCK), lambda l, n: (l, 0, n)),
        out_shape=jax.ShapeDtypeStruct((depth, d, (W_IN_BLOCKS + 1) * FILL_BLOCK), BF16),
        compiler_params=pltpu.CompilerParams(
            dimension_semantics=("parallel", "parallel"), vmem_limit_bytes=VMEM_LIMIT_BYTES),
        name="w_in_prep",
    )(w_t, w_t)


def _rope_kernel(pos_ref, inv_ref, cos_ref, sin_ref):
    seq = pos_ref.shape[0]
    half_seq, half = seq // 2, RET_DIM // 2
    first = lax.broadcasted_iota(jnp.int32, (half_seq, RET_DIM), 1) < half
    pos = jnp.where(first, pos_ref[0:half_seq, :], pos_ref[half_seq:seq, :])
    ang = pos.astype(F32) * inv_ref[...]
    cs, sn = jnp.cos(ang), jnp.sin(ang)
    cs_sw, sn_sw = pltpu.roll(cs, half, axis=1), pltpu.roll(sn, half, axis=1)
    cos_ref[0:half_seq, :] = jnp.where(first, cs, cs_sw)
    cos_ref[half_seq:seq, :] = jnp.where(first, cs_sw, cs)
    sin_ref[0:half_seq, :] = jnp.where(first, -sn, sn_sw)
    sin_ref[half_seq:seq, :] = jnp.where(first, -sn_sw, sn)


def _rope_call(positions):
    bsz, seq = positions.shape
    half = RET_DIM // 2
    inv_freq = ROPE_THETA ** (-jnp.arange(0, RET_DIM, 2, dtype=F32) / RET_DIM)
    inv2 = jnp.concatenate([inv_freq, inv_freq]).reshape(1, 2 * half)
    tbl = jax.ShapeDtypeStruct((bsz, seq, RET_DIM), F32)
    return pl.pallas_call(
        _rope_kernel,
        grid=(bsz,),
        in_specs=[
            pl.BlockSpec((None, seq, 1), lambda b: (b, 0, 0)),
            pl.BlockSpec((1, RET_DIM), lambda b: (0, 0)),
        ],
        out_specs=[pl.BlockSpec((None, seq, RET_DIM), lambda b: (b, 0, 0))] * 2,
        out_shape=[tbl, tbl],
        compiler_params=pltpu.CompilerParams(
            dimension_semantics=("parallel",), vmem_limit_bytes=VMEM_LIMIT_BYTES),
        name="rope_tables",
    )(positions.reshape(bsz, seq, 1), inv2)


def _run_fillers(fillers, slots_left):
    n = -(-len(fillers) // max(slots_left, 1))
    for _ in range(min(n, len(fillers))):
        fillers.pop(0)()


def _interleave(a, b):
    paired = [t for duo in zip(a, b) for t in duo]
    return paired + a[len(b):] + b[len(a):]


def _proj_block(dst_ref, h, w_ref, dst_lo, act=None):
    def run():
        r = _dot(h, w_ref[...])
        dst_ref[:, dst_lo:dst_lo + w_ref.shape[1]] = r if act is None else act(r)
    return run


def _retention_unit(hd, c, qkvg_ref, cos_ref, sin_ref, state_ref, y_ref):
    def run():
        row = lax.broadcasted_iota(jnp.int32, (CHUNK, CHUNK), 0)
        col = lax.broadcasted_iota(jnp.int32, (CHUNK, CHUNK), 1)
        rowf = row.astype(F32)
        half = RET_DIM // 2
        log_g = math.log(1.0 - 2.0 ** (-5.0 - hd))
        decay_mask = jnp.exp(jnp.where(row >= col, (row - col).astype(F32) * log_g, NEG_INF))
        q_decay = jnp.exp((rowf + 1.0) * log_g)
        k_decay = jnp.exp((CHUNK - 1.0 - rowf) * log_g)
        chunk_decay = math.exp(CHUNK * log_g)
        rows = slice(c * CHUNK, (c + 1) * CHUNK)
        lo = hd * RET_DIM
        cos = cos_ref[rows, :]
        sin = sin_ref[rows, :]

        def rot(t):
            return t * cos + pltpu.roll(t, half, axis=1) * sin

        qc = (rot(qkvg_ref[rows, lo:lo + RET_DIM]) * (RET_DIM ** -0.5)).astype(BF16)
        kc = rot(qkvg_ref[rows, RET_WIDTH + lo:RET_WIDTH + lo + RET_DIM])
        vc = qkvg_ref[rows, 2 * RET_WIDTH + lo:2 * RET_WIDTH + lo + RET_DIM].astype(BF16)
        gate = qkvg_ref[rows, 3 * RET_WIDTH + lo:3 * RET_WIDTH + lo + RET_DIM]
        state = state_ref[hd]
        scores = lax.dot_general(qc, kc.astype(BF16), (((1,), (1,)), ((), ())), preferred_element_type=F32)
        inner = _dot((scores * decay_mask).astype(BF16), vc)
        cross = _dot(qc, state.astype(BF16)) * q_decay
        kd = (kc * k_decay).astype(BF16)
        state_ref[hd] = chunk_decay * state + lax.dot_general(kd, vc, (((0,), (0,)), ((), ())),
                                                              preferred_element_type=F32)
        o = inner + cross
        mu = jnp.mean(o, axis=-1, keepdims=True)
        oc = o - mu
        var = jnp.mean(oc * oc, axis=-1, keepdims=True)
        y = oc * lax.rsqrt(var + EPS) * _silu(gate)
        y_ref[rows, lo:lo + RET_DIM] = y.astype(y_ref.dtype)
    return run


def _block_diag_pair(slab):
    lane = lax.broadcasted_iota(jnp.int32, slab.shape, 1)
    first = lane < SSM_HEAD_DIM
    top = jnp.where(first, slab, 0.0)
    bot = jnp.where(first, 0.0, slab)
    return jnp.concatenate([top, bot], axis=0).astype(BF16)


def _ssd_conv(xbc_ref, xact_ref, convw_ref, convb_ref, tile, fillers):
    nblk = SSM_XBC // FILL_BLOCK
    for n in range(nblk):
        cols = slice(n * FILL_BLOCK, (n + 1) * FILL_BLOCK)
        conv = convb_ref[:, cols]
        for kk in range(SSM_CONV):
            off = SUBLANES - (SSM_CONV - 1) + kk
            conv = conv + convw_ref[kk:kk + 1, cols] * xbc_ref[off:off + tile, cols]
        xact_ref[:, cols] = _silu(conv)
        _run_fillers(fillers, nblk - n)
    xbc_ref[0:SUBLANES, :] = xbc_ref[tile:tile + SUBLANES, :]


T_ACS, T_WT, T_EACS = 0, 1, 2


def _cumsum_rows(x):
    row = lax.broadcasted_iota(jnp.int32, x.shape, 0)
    k = 1
    while k < x.shape[0]:
        x = x + jnp.where(row >= k, pltpu.roll(x, k, axis=0), 0.0)
        k *= 2
    return x


def _ssd_terms_task(c, dt_ref, dtb_ref, alog_ref, term_ref, trans_ref):
    def run():
        rows = slice(c * CHUNK, (c + 1) * CHUNK)
        dt_in = dt_ref[rows, :] + dtb_ref[...]
        dt_c = jnp.maximum(dt_in, 0.0) + jnp.log1p(jnp.exp(-jnp.abs(dt_in)))
        a_cs = _cumsum_rows(dt_c * (-jnp.exp(alog_ref[...])))
        a_last = a_cs[CHUNK - 1:CHUNK, :]
        term_ref[c, T_ACS] = a_cs
        term_ref[c, T_WT] = (jnp.exp(a_last - a_cs) * dt_c).T
        term_ref[c, T_EACS] = jnp.exp(a_cs)
        trans_ref[c, 0] = a_cs.T
        trans_ref[c, 1] = dt_c.T
    return run


def _ssd_head_task(c, heads, term_ref, trans_ref, ld_ref, eb_ref, cd_ref):
    def run():
        row = lax.broadcasted_iota(jnp.int32, (CHUNK, CHUNK), 0)
        col = lax.broadcasted_iota(jnp.int32, (CHUNK, CHUNK), 1)
        tril = row >= col
        lane1 = lax.broadcasted_iota(jnp.int32, (1, LANES), 1)
        a_cs = term_ref[c, T_ACS]
        e_acs = term_ref[c, T_EACS]
        cdec = e_acs[CHUNK - 1:CHUNK, :]
        for hh in heads:
            seg = _lane_bcast(a_cs, hh) - trans_ref[c, 0, hh:hh + 1, :]
            ld_ref[c * SSM_HEADS + hh] = jnp.exp(jnp.where(tril, seg, NEG_INF)) * trans_ref[c, 1, hh:hh + 1, :]
            eb_ref[c * SSM_HEADS + hh] = _lane_bcast(e_acs, hh)
        for h1 in heads[::2]:
            cd_ref[c, h1 // 2:h1 // 2 + 1, :] = jnp.where(
                lane1 < SSM_HEAD_DIM, _lane_bcast(cdec, h1), _lane_bcast(cdec, h1 + 1))
    return run


def _ssd_scan(xact_ref, z_ref, dskip_ref, ng_ref, term_ref, ld_ref, eb_ref, cd_ref,
              state_ref, ychunk_ref, y_ref, tile, fillers):
    pair = 2 * SSM_HEAD_DIM
    heads_per_group = SSM_HEADS // SSM_GROUPS
    n_chunks = tile // CHUNK
    slots = n_chunks * SSM_HEADS // 2

    for c in range(n_chunks):
        rows = slice(c * CHUNK, (c + 1) * CHUNK)
        for g in range(SSM_GROUPS):
            b_lo = SSM_WIDTH + g * SSM_STATE
            c_lo = SSM_WIDTH + SSM_GROUPS * SSM_STATE + g * SSM_STATE
            b_g = xact_ref[rows, b_lo:b_lo + SSM_STATE]
            c_g = xact_ref[rows, c_lo:c_lo + SSM_STATE]
            b_t = b_g.T
            cb = _dot(c_g.astype(BF16), b_t.astype(BF16))
            for p in range(heads_per_group // 2):
                h1 = g * heads_per_group + 2 * p
                lo = h1 * SSM_HEAD_DIM
                xs_pair = xact_ref[rows, lo:lo + pair]
                rhs = _block_diag_pair(xs_pair)
                m_parts, s_parts, o_parts = [], [], []
                for hh in (h1, h1 + 1):
                    m_parts.append(cb * ld_ref[c * SSM_HEADS + hh])
                    s_parts.append(b_t * term_ref[c, T_WT, hh:hh + 1, :])
                    o_parts.append(c_g * eb_ref[c * SSM_HEADS + hh])
                y_diag = _dot(jnp.concatenate(m_parts, axis=1).astype(BF16), rhs)
                s_new = _dot(jnp.concatenate(s_parts, axis=1).astype(BF16), rhs)
                s_prev = state_ref[:, lo:lo + pair]
                y_off = _dot(jnp.concatenate(o_parts, axis=1).astype(BF16), _block_diag_pair(s_prev))
                state_ref[:, lo:lo + pair] = cd_ref[c, h1 // 2:h1 // 2 + 1, :] * s_prev + s_new
                ychunk_ref[:, lo:lo + pair] = y_diag + y_off + dskip_ref[:, lo:lo + pair] * xs_pair
                _run_fillers(fillers, slots)
                slots -= 1
        y = ychunk_ref[...] * _silu(z_ref[rows, :])
        ms = jnp.mean(y * y, axis=-1, keepdims=True)
        y_ref[rows, :] = (y * lax.rsqrt(ms + EPS) * ng_ref[...]).astype(y_ref.dtype)


def _short_conv_task(cp_ref, scw_ref, u_ref, y_ref, tile):
    def run():
        width = u_ref.shape[1]
        u_ref[SUBLANES:SUBLANES + tile, :] = cp_ref[:, width:2 * width] * cp_ref[:, 2 * width:3 * width]
        conv = jnp.zeros((tile, width), F32)
        for kk in range(SC_CONV):
            off = SUBLANES - (SC_CONV - 1) + kk
            conv = conv + scw_ref[kk:kk + 1, :] * u_ref[off:off + tile, :]
        u_ref[0:SUBLANES, :] = u_ref[tile:tile + SUBLANES, :]
        y_ref[...] = (cp_ref[:, :width] * conv * _silu(cp_ref[:, 3 * width:])).astype(y_ref.dtype)
    return run


def _layer_kernel(x_ref, g_ref, scale_ref, shift_ref, rgate_ref, cos_ref, sin_ref, win_ref,
                  convw_ref, convb_ref, dtb_ref, alog_ref, dskip_ref, sng_ref, scw_ref,
                  wbr_ret_ref, wbr_ssm_ref, wbr_sc_ref, wout_ref, fg_ref,
                  o_ref,
                  rstate_ref, xbc_ref, sstate_ref, ychunk_ref, u_ref, yret_ref, yssm_ref, ysc_ref,
                  qkvg_ref, z_ref, dt_ref, xact_ref, cp_ref, gates_ref, acc_ref,
                  term_ref, trans_ref, ld_ref, eb_ref, cd_ref,
                  *, tile, final):
    win = [win_ref.at[:, n * FILL_BLOCK:(n + 1) * FILL_BLOCK] for n in range(W_IN_BLOCKS)]
    wdt_ref = win_ref.at[:, W_IN_BLOCKS * FILL_BLOCK:W_IN_BLOCKS * FILL_BLOCK + DT_PAD]

    @pl.when(pl.program_id(1) == 0)
    def _():
        rstate_ref[...] = jnp.zeros_like(rstate_ref)
        sstate_ref[...] = jnp.zeros_like(sstate_ref)
        xbc_ref[0:SUBLANES, :] = jnp.zeros((SUBLANES, xbc_ref.shape[1]), F32)
        u_ref[0:SUBLANES, :] = jnp.zeros((SUBLANES, u_ref.shape[1]), F32)

    d = x_ref.shape[1]
    h = _modulated_norm(x_ref[...], g_ref[...], scale_ref[...], shift_ref[...])
    blk = FILL_BLOCK

    def proj(dst_ref, first_block, n, act=None):
        return _proj_block(dst_ref, h, win[first_block + n], n * blk, act)

    for n in range(SSM_XBC // blk):
        xbc_ref[SUBLANES:SUBLANES + tile, n * blk:(n + 1) * blk] = _dot(h, win[XBC_BLOCK + n][...])
    n_chunks = tile // CHUNK
    fillers = [_proj_block(dt_ref, h, wdt_ref, 0)]
    fillers += [_ssd_terms_task(c, dt_ref, dtb_ref, alog_ref, term_ref, trans_ref) for c in range(n_chunks)]
    ret_proj = [proj(qkvg_ref, RET_BLOCK, n) for n in range(4 * RET_WIDTH // blk)]
    quad = 4
    head_tasks = [_ssd_head_task(c, list(range(q, q + quad)), term_ref, trans_ref, ld_ref, eb_ref, cd_ref)
                  for c in range(n_chunks) for q in range(0, SSM_HEADS, quad)]
    fillers += _interleave(ret_proj, head_tasks)
    _ssd_conv(xbc_ref, xact_ref, convw_ref, convb_ref, tile, fillers)

    def ret_merge():
        acc_ref[...] = gates_ref[:, :d] * _dot(yret_ref[...], wbr_ret_ref[...])

    def sc_merge():
        acc_ref[...] += gates_ref[:, 2 * d:] * _dot(ysc_ref[...], wbr_sc_ref[...])

    dense = [proj(z_ref, Z_BLOCK, n) for n in range(SSM_WIDTH // blk)]
    dense += [proj(cp_ref, SC_BLOCK, n) for n in range(cp_ref.shape[1] // blk)]
    dense += [proj(gates_ref, GATE_BLOCK, n, _sigmoid) for n in range(3 * d // blk)]
    ret_units = [_retention_unit(hd, c, qkvg_ref, cos_ref, sin_ref, rstate_ref, yret_ref)
                 for c in range(tile // CHUNK) for hd in range(RET_HEADS)]
    fillers = _interleave(dense, ret_units)
    fillers += [ret_merge, _short_conv_task(cp_ref, scw_ref, u_ref, ysc_ref, tile), sc_merge]
    _ssd_scan(xact_ref, z_ref, dskip_ref, sng_ref, term_ref, ld_ref, eb_ref, cd_ref,
              sstate_ref, ychunk_ref, yssm_ref, tile, fillers)
    _run_fillers(fillers, 1)

    merged = acc_ref[...] + gates_ref[:, d:2 * d] * _dot(yssm_ref[...], wbr_ssm_ref[...])
    out = _dot(merged.astype(BF16), wout_ref[...])
    xn = x_ref[...] + rgate_ref[...] * out
    if final:
        ms = jnp.mean(xn * xn, axis=-1, keepdims=True)
        xn = xn * lax.rsqrt(ms + EPS) * fg_ref[...]
    o_ref[...] = xn


def _layer_call(x, ada5, norm_g3, cos2, sin2, weights, smalls, layer, tile, final):
    bsz, seq, d = x.shape
    row_spec = lambda idx: pl.BlockSpec((None, None, None, 1, d), lambda b, j: (layer, b, idx, 0, 0))
    per_layer = lambda a: pl.BlockSpec((None,) + a.shape[1:], lambda b, j: (layer,) + (0,) * (a.ndim - 1),
                                       pipeline_mode=pl.Buffered(1))
    tok = lambda w: pl.BlockSpec((None, tile, w), lambda b, j: (b, j, 0))
    w_blocks, wbr_ret, wbr_ssm, wbr_sc, w_out = weights
    conv_w, conv_b, dt_bias, a_log, d_skip, ssm_ng, sc_w, final_g = smalls
    width = sc_w.shape[2]
    return pl.pallas_call(
        functools.partial(_layer_kernel, tile=tile, final=final),
        grid=(bsz, seq // tile),
        in_specs=[
            tok(d),
            per_layer(norm_g3),
            row_spec(1),
            row_spec(0),
            row_spec(2),
            tok(RET_DIM), tok(RET_DIM),
            per_layer(w_blocks),
            per_layer(conv_w), per_layer(conv_b), per_layer(dt_bias), per_layer(a_log), per_layer(d_skip),
            per_layer(ssm_ng), per_layer(sc_w),
            per_layer(wbr_ret), per_layer(wbr_ssm), per_layer(wbr_sc), per_layer(w_out),
            pl.BlockSpec(final_g.shape, lambda b, j: (0, 0), pipeline_mode=pl.Buffered(1)),
        ],
        out_specs=tok(d),
        out_shape=jax.ShapeDtypeStruct((bsz, seq, d), F32),
        scratch_shapes=[
            pltpu.VMEM((RET_HEADS, RET_DIM, RET_DIM), F32),
            pltpu.VMEM((tile + SUBLANES, SSM_XBC), F32),
            pltpu.VMEM((SSM_STATE, SSM_WIDTH), F32),
            pltpu.VMEM((CHUNK, SSM_WIDTH), F32),
            pltpu.VMEM((tile + SUBLANES, width), F32),
            pltpu.VMEM((tile, RET_WIDTH), BF16),
            pltpu.VMEM((tile, SSM_WIDTH), BF16),
            pltpu.VMEM((tile, width), BF16),
            pltpu.VMEM((tile, 4 * RET_WIDTH), F32),
            pltpu.VMEM((tile, SSM_WIDTH), F32),
            pltpu.VMEM((tile, DT_PAD), F32),
            pltpu.VMEM((tile, SSM_XBC), F32),
            pltpu.VMEM((tile, 4 * width), F32),
            pltpu.VMEM((tile, 3 * d), F32),
            pltpu.VMEM((tile, d), F32),
            pltpu.VMEM((tile // CHUNK, 3, CHUNK, DT_PAD), F32),
            pltpu.VMEM((tile // CHUNK, 2, DT_PAD, CHUNK), F32),
            pltpu.VMEM((tile // CHUNK * SSM_HEADS, CHUNK, CHUNK), F32),
            pltpu.VMEM((tile // CHUNK * SSM_HEADS, CHUNK, LANES), F32),
            pltpu.VMEM((tile // CHUNK, SSM_HEADS // 2, LANES), F32),
        ],
        compiler_params=pltpu.CompilerParams(
            dimension_semantics=("parallel", "arbitrary"), vmem_limit_bytes=VMEM_LIMIT_BYTES),
        name="layer",
    )(x, norm_g3, ada5, ada5, ada5, cos2, sin2, w_blocks,
      conv_w, conv_b, dt_bias, a_log, d_skip, ssm_ng, sc_w, wbr_ret, wbr_ssm, wbr_sc, w_out, final_g)


def kernel(x, c, positions, norm_g, w_ada, b_ada, w_in, ssm_conv_w, ssm_conv_b, ssm_dt_bias, ssm_a_log,
           ssm_d, ssm_norm_g, sc_conv_w, w_br_ret, w_br_ssm, w_br_sc, w_out, final_norm_g):
    bsz, seq, d = x.shape
    depth = w_in.shape[0]
    sc_width = sc_conv_w.shape[2]
    tile = 256
    assert seq % tile == 0 and tile % CHUNK == 0

    sizes = (RET_WIDTH, RET_WIDTH, RET_WIDTH, RET_WIDTH, SSM_WIDTH, SSM_XBC, SSM_HEADS,
             sc_width, sc_width, sc_width, sc_width, 3 * d)
    offs = [0]
    for s in sizes:
        offs.append(offs[-1] + s)
    assert offs[-1] == w_in.shape[2] and d == D_MODEL and sc_width == SC_WIDTH

    ada = _ada_call(c, w_ada, b_ada)
    ada5 = ada.reshape(depth, bsz, 3, 1, d)
    cos2, sin2 = _rope_call(positions)
    norm_g3 = norm_g.reshape(depth, 1, d)

    dt_pad = DT_PAD - SSM_HEADS
    assert offs[6] == SC_BLOCK * FILL_BLOCK
    weights = (_w_in_prep_call(w_in), w_br_ret.astype(BF16), w_br_ssm.astype(BF16), w_br_sc.astype(BF16), w_out.astype(BF16))
    pad_heads = lambda v: jnp.pad(v, ((0, 0), (0, dt_pad))).reshape(depth, 1, DT_PAD)
    smalls = (
        ssm_conv_w, ssm_conv_b.reshape(depth, 1, SSM_XBC),
        pad_heads(ssm_dt_bias), pad_heads(ssm_a_log),
        jnp.repeat(ssm_d, SSM_HEAD_DIM, axis=1).reshape(depth, 1, SSM_WIDTH),
        ssm_norm_g.reshape(depth, 1, SSM_WIDTH), sc_conv_w, final_norm_g.reshape(1, d),
    )
    for layer in range(depth):
        x = _layer_call(x, ada5, norm_g3, cos2, sin2, weights, smalls, layer, tile,
                        final=(layer == depth - 1))
    return x
```

```python
import functools
import math

import jax
import jax.numpy as jnp
from jax import lax
from jax.experimental import pallas as pl
from jax.experimental.pallas import tpu as pltpu

F32 = jnp.float32
BF16 = jnp.bfloat16

EPS = 1e-6
RET_HEADS = 4
RET_DIM = 128
RET_WIDTH = RET_HEADS * RET_DIM
ROPE_THETA = 10000.0
SSM_HEADS = 16
SSM_HEAD_DIM = 64
SSM_GROUPS = 2
SSM_STATE = 128
SSM_CONV = 4
SSM_WIDTH = SSM_HEADS * SSM_HEAD_DIM
SSM_XBC = SSM_WIDTH + 2 * SSM_GROUPS * SSM_STATE
SC_CONV = 3
CHUNK = 128
LANES = 128
SUBLANES = 8
DT_PAD = LANES
FILL_BLOCK = 512
D_MODEL = 1024
SC_WIDTH = D_MODEL // 2
RET_BLOCK = 0
Z_BLOCK = RET_BLOCK + 4 * RET_WIDTH // FILL_BLOCK
XBC_BLOCK = Z_BLOCK + SSM_WIDTH // FILL_BLOCK
SC_BLOCK = XBC_BLOCK + SSM_XBC // FILL_BLOCK
GATE_BLOCK = SC_BLOCK + 4 * SC_WIDTH // FILL_BLOCK
W_IN_BLOCKS = GATE_BLOCK + 3 * D_MODEL // FILL_BLOCK
P_NORM_G, P_CONV_W, P_CONV_B, P_DT_BIAS, P_A_LOG, P_DSKIP, P_SSM_NG, P_SC_W, P_FINAL_G = 0, 1, 5, 6, 7, 8, 9, 10, 13
PACK_ROWS = 16
ADA_ROWS = SUBLANES
VMEM_LIMIT_BYTES = 60 * 1024 * 1024

NEG_INF = float("-inf")


def _sigmoid(x):
    return 1.0 / (1.0 + jnp.exp(-x))


def _silu(x):
    return x * _sigmoid(x)


def _dot(a, b):
    return jnp.dot(a, b, preferred_element_type=F32)


def _modulated_norm(x, g, scale, shift):
    ms = jnp.mean(x * x, axis=-1, keepdims=True)
    y = x * lax.rsqrt(ms + EPS) * g
    return (y * (1.0 + scale) + shift).astype(BF16)


def _lane_bcast(a, col):
    return jnp.broadcast_to(a[:, col:col + 1], a.shape)


def _ada_kernel(c_ref, w_ref, b_ref, o_ref):
    act = _silu(c_ref[...])
    hi = act.astype(BF16)
    lo = (act - hi.astype(F32)).astype(BF16)
    w = w_ref[...].astype(BF16)
    o_ref[...] = _dot(hi, w) + _dot(lo, w) + b_ref[...]


def _ada_call(c, w_ada, b_ada):
    depth, d, d3 = w_ada.shape
    bsz = c.shape[0]
    nblk = d3 // d
    return pl.pallas_call(
        _ada_kernel,
        grid=(depth, nblk),
        in_specs=[
            pl.BlockSpec((bsz, d), lambda l, n: (0, 0)),
            pl.BlockSpec((None, d, d), lambda l, n: (l, 0, n)),
            pl.BlockSpec((None, 1, d), lambda l, n: (l, 0, n)),
        ],
        out_specs=pl.BlockSpec((None, bsz, d), lambda l, n: (l, 0, n)),
        out_shape=jax.ShapeDtypeStruct((depth, bsz, d3), F32),
        compiler_params=pltpu.CompilerParams(
            dimension_semantics=("parallel", "parallel"), vmem_limit_bytes=VMEM_LIMIT_BYTES),
        name="ada",
    )(c, w_ada, b_ada.reshape(depth, 1, d3))


def _w_in_prep_kernel(wa_ref, wb_ref, o_ref):
    o_ref[:, :FILL_BLOCK] = wa_ref[0].T.astype(o_ref.dtype)
    o_ref[:, FILL_BLOCK:] = wb_ref[0].T.astype(o_ref.dtype)


def _w_in_prep_call(w_in):
    depth, d, width = w_in.shape
    assert width == W_IN_BLOCKS * FILL_BLOCK + SSM_HEADS and (W_IN_BLOCKS + 1) % 2 == 0
    w_t = jnp.swapaxes(w_in, 1, 2)
    dt_start = SC_BLOCK * FILL_BLOCK
    per_block = FILL_BLOCK // SSM_HEADS
    start = lambda n: SSM_HEADS * jnp.where(n == W_IN_BLOCKS, dt_start // SSM_HEADS,
                                            n * per_block + jnp.where(n >= SC_BLOCK, 1, 0))
    rows = lambda which: pl.BlockSpec((pl.Element(1), pl.Element(FILL_BLOCK), pl.Element(d)),
                                      lambda l, n: (l, start(2 * n + which), 0))
    return pl.pallas_call(
        _w_in_prep_kernel,
        grid=(depth, (W_IN_BLOCKS + 1) // 2),
        in_specs=[rows(0), rows(1)],
        out_specs=pl.BlockSpec((None, d, 2 * FILL_BLOCK), lambda l, n: (l, 0, n)),
        out_shape=jax.ShapeDtypeStruct((depth, d, (W_IN_BLOCKS + 1) * FILL_BLOCK), BF16),
        compiler_params=pltpu.CompilerParams(
            dimension_semantics=("parallel", "parallel"), vmem_limit_bytes=VMEM_LIMIT_BYTES),
        name="w_in_prep",
    )(w_t, w_t)


def _rope_kernel(pos_ref, inv_ref, cos_ref, sin_ref):
    seq = pos_ref.shape[0]
    half_seq, half = seq // 2, RET_DIM // 2
    first = lax.broadcasted_iota(jnp.int32, (half_seq, RET_DIM), 1) < half
    pos = jnp.where(first, pos_ref[0:half_seq, :], pos_ref[half_seq:seq, :])
    ang = pos.astype(F32) * inv_ref[...]
    cs, sn = jnp.cos(ang), jnp.sin(ang)
    cs_sw, sn_sw = pltpu.roll(cs, half, axis=1), pltpu.roll(sn, half, axis=1)
    cos_ref[0:half_seq, :] = jnp.where(first, cs, cs_sw)
    cos_ref[half_seq:seq, :] = jnp.where(first, cs_sw, cs)
    sin_ref[0:half_seq, :] = jnp.where(first, -sn, sn_sw)
    sin_ref[half_seq:seq, :] = jnp.where(first, -sn_sw, sn)


def _rope_call(positions):
    bsz, seq = positions.shape
    half = RET_DIM // 2
    inv_freq = ROPE_THETA ** (-jnp.arange(0, RET_DIM, 2, dtype=F32) / RET_DIM)
    inv2 = jnp.concatenate([inv_freq, inv_freq]).reshape(1, 2 * half)
    tbl = jax.ShapeDtypeStruct((bsz, seq, RET_DIM), F32)
    return pl.pallas_call(
        _rope_kernel,
        grid=(bsz,),
        in_specs=[
            pl.BlockSpec((None, seq, 1), lambda b: (b, 0, 0)),
            pl.BlockSpec((1, RET_DIM), lambda b: (0, 0)),
        ],
        out_specs=[pl.BlockSpec((None, seq, RET_DIM), lambda b: (b, 0, 0))] * 2,
        out_shape=[tbl, tbl],
        compiler_params=pltpu.CompilerParams(
            dimension_semantics=("parallel",), vmem_limit_bytes=VMEM_LIMIT_BYTES),
        name="rope_tables",
    )(positions.reshape(bsz, seq, 1), inv2)


def _run_fillers(fillers, slots_left):
    n = -(-len(fillers) // max(slots_left, 1))
    for _ in range(min(n, len(fillers))):
        fillers.pop(0)()


def _interleave(a, b):
    paired = [t for duo in zip(a, b) for t in duo]
    return paired + a[len(b):] + b[len(a):]


def _proj_block(dst_ref, h, w_ref, dst_lo, act=None):
    def run():
        r = _dot(h, w_ref[...])
        dst_ref[:, dst_lo:dst_lo + w_ref.shape[1]] = r if act is None else act(r)
    return run


def _retention_unit(hd, c, qkvg_ref, cos_ref, sin_ref, state_ref, y_ref):
    def run():
        row = lax.broadcasted_iota(jnp.int32, (CHUNK, CHUNK), 0)
        col = lax.broadcasted_iota(jnp.int32, (CHUNK, CHUNK), 1)
        rowf = row.astype(F32)
        half = RET_DIM // 2
        log_g = math.log(1.0 - 2.0 ** (-5.0 - hd))
        decay_mask = jnp.exp(jnp.where(row >= col, (row - col).astype(F32) * log_g, NEG_INF))
        q_decay = jnp.exp((rowf + 1.0) * log_g)
        k_decay = jnp.exp((CHUNK - 1.0 - rowf) * log_g)
        chunk_decay = math.exp(CHUNK * log_g)
        rows = slice(c * CHUNK, (c + 1) * CHUNK)
        lo = hd * RET_DIM
        cos = cos_ref[rows, :]
        sin = sin_ref[rows, :]

        def rot(t):
            return t * cos + pltpu.roll(t, half, axis=1) * sin

        qc = (rot(qkvg_ref[rows, lo:lo + RET_DIM]) * (RET_DIM ** -0.5)).astype(BF16)
        kc = rot(qkvg_ref[rows, RET_WIDTH + lo:RET_WIDTH + lo + RET_DIM])
        vc = qkvg_ref[rows, 2 * RET_WIDTH + lo:2 * RET_WIDTH + lo + RET_DIM].astype(BF16)
        gate = qkvg_ref[rows, 3 * RET_WIDTH + lo:3 * RET_WIDTH + lo + RET_DIM]
        state = state_ref[hd]
        scores = lax.dot_general(qc, kc.astype(BF16), (((1,), (1,)), ((), ())), preferred_element_type=F32)
        inner = _dot((scores * decay_mask).astype(BF16), vc)
        cross = _dot(qc, state.astype(BF16)) * q_decay
        kd = (kc * k_decay).astype(BF16)
        state_ref[hd] = chunk_decay * state + lax.dot_general(kd, vc, (((0,), (0,)), ((), ())),
                                                              preferred_element_type=F32)
        o = inner + cross
        mu = jnp.mean(o, axis=-1, keepdims=True)
        oc = o - mu
        var = jnp.mean(oc * oc, axis=-1, keepdims=True)
        y = oc * lax.rsqrt(var + EPS) * _silu(gate)
        y_ref[rows, lo:lo + RET_DIM] = y.astype(y_ref.dtype)
    return run


def _block_diag_pair(slab):
    lane = lax.broadcasted_iota(jnp.int32, slab.shape, 1)
    first = lane < SSM_HEAD_DIM
    top = jnp.where(first, slab, 0.0)
    bot = jnp.where(first, 0.0, slab)
    return jnp.concatenate([top, bot], axis=0).astype(BF16)


def _ssd_conv(xbc_ref, xact_ref, convw_ref, convb_ref, tile, fillers):
    nblk = SSM_XBC // FILL_BLOCK
    for n in range(nblk):
        cols = slice(n * FILL_BLOCK, (n + 1) * FILL_BLOCK)
        conv = convb_ref[:, cols]
        for kk in range(SSM_CONV):
            off = SUBLANES - (SSM_CONV - 1) + kk
            conv = conv + convw_ref[kk:kk + 1, cols] * xbc_ref[off:off + tile, cols]
        xact_ref[:, cols] = _silu(conv)
        _run_fillers(fillers, nblk - n)
    xbc_ref[0:SUBLANES, :] = xbc_ref[tile:tile + SUBLANES, :]


T_ACS, T_WT, T_EACS = 0, 1, 2


def _cumsum_rows(x):
    row = lax.broadcasted_iota(jnp.int32, x.shape, 0)
    k = 1
    while k < x.shape[0]:
        x = x + jnp.where(row >= k, pltpu.roll(x, k, axis=0), 0.0)
        k *= 2
    return x


def _ssd_terms_task(c, dt_ref, dtb_ref, alog_ref, term_ref, trans_ref):
    def run():
        rows = slice(c * CHUNK, (c + 1) * CHUNK)
        dt_in = dt_ref[rows, :] + dtb_ref[...]
        dt_c = jnp.maximum(dt_in, 0.0) + jnp.log1p(jnp.exp(-jnp.abs(dt_in)))
        a_cs = _cumsum_rows(dt_c * (-jnp.exp(alog_ref[...])))
        a_last = a_cs[CHUNK - 1:CHUNK, :]
        term_ref[c, T_ACS] = a_cs
        term_ref[c, T_WT] = (jnp.exp(a_last - a_cs) * dt_c).T
        term_ref[c, T_EACS] = jnp.exp(a_cs)
        trans_ref[c, 0] = a_cs.T
        trans_ref[c, 1] = dt_c.T
    return run


def _ssd_head_task(c, heads, term_ref, trans_ref, ld_ref, eb_ref, cd_ref):
    def run():
        row = lax.broadcasted_iota(jnp.int32, (CHUNK, CHUNK), 0)
        col = lax.broadcasted_iota(jnp.int32, (CHUNK, CHUNK), 1)
        tril = row >= col
        lane1 = lax.broadcasted_iota(jnp.int32, (1, LANES), 1)
        a_cs = term_ref[c, T_ACS]
        e_acs = term_ref[c, T_EACS]
        cdec = e_acs[CHUNK - 1:CHUNK, :]
        for hh in heads:
            seg = _lane_bcast(a_cs, hh) - trans_ref[c, 0, hh:hh + 1, :]
            ld_ref[c * SSM_HEADS + hh] = jnp.exp(jnp.where(tril, seg, NEG_INF)) * trans_ref[c, 1, hh:hh + 1, :]
            eb_ref[c * SSM_HEADS + hh] = _lane_bcast(e_acs, hh)
        for h1 in heads[::2]:
            cd_ref[c, h1 // 2:h1 // 2 + 1, :] = jnp.where(
                lane1 < SSM_HEAD_DIM, _lane_bcast(cdec, h1), _lane_bcast(cdec, h1 + 1))
    return run


def _ssd_scan(xact_ref, z_ref, dskip_ref, ng_ref, term_ref, ld_ref, eb_ref, cd_ref,
              state_ref, ychunk_ref, y_ref, tile, fillers):
    pair = 2 * SSM_HEAD_DIM
    heads_per_group = SSM_HEADS // SSM_GROUPS
    n_chunks = tile // CHUNK
    slots = n_chunks * SSM_HEADS // 2

    for c in range(n_chunks):
        rows = slice(c * CHUNK, (c + 1) * CHUNK)
        for g in range(SSM_GROUPS):
            b_lo = SSM_WIDTH + g * SSM_STATE
            c_lo = SSM_WIDTH + SSM_GROUPS * SSM_STATE + g * SSM_STATE
            b_g = xact_ref[rows, b_lo:b_lo + SSM_STATE]
            c_g = xact_ref[rows, c_lo:c_lo + SSM_STATE]
            b_t = b_g.T
            cb = _dot(c_g.astype(BF16), b_t.astype(BF16))
            for p in range(heads_per_group // 2):
                h1 = g * heads_per_group + 2 * p
                lo = h1 * SSM_HEAD_DIM
                xs_pair = xact_ref[rows, lo:lo + pair]
                rhs = _block_diag_pair(xs_pair)
                m_parts, s_parts, o_parts = [], [], []
                for hh in (h1, h1 + 1):
                    m_parts.append(cb * ld_ref[c * SSM_HEADS + hh])
                    s_parts.append(b_t * term_ref[c, T_WT, hh:hh + 1, :])
                    o_parts.append(c_g * eb_ref[c * SSM_HEADS + hh])
                y_diag = _dot(jnp.concatenate(m_parts, axis=1).astype(BF16), rhs)
                s_new = _dot(jnp.concatenate(s_parts, axis=1).astype(BF16), rhs)
                s_prev = state_ref[:, lo:lo + pair]
                y_off = _dot(jnp.concatenate(o_parts, axis=1).astype(BF16), _block_diag_pair(s_prev))
                state_ref[:, lo:lo + pair] = cd_ref[c, h1 // 2:h1 // 2 + 1, :] * s_prev + s_new
                ychunk_ref[:, lo:lo + pair] = y_diag + y_off + dskip_ref[:, lo:lo + pair] * xs_pair
                _run_fillers(fillers, slots)
                slots -= 1
        y = ychunk_ref[...] * _silu(z_ref[rows, :])
        ms = jnp.mean(y * y, axis=-1, keepdims=True)
        y_ref[rows, :] = (y * lax.rsqrt(ms + EPS) * ng_ref[...]).astype(y_ref.dtype)


def _short_conv_task(cp_ref, scw_ref, u_ref, y_ref, tile):
    def run():
        width = u_ref.shape[1]
        u_ref[SUBLANES:SUBLANES + tile, :] = cp_ref[:, width:2 * width] * cp_ref[:, 2 * width:3 * width]
        conv = jnp.zeros((tile, width), F32)
        for kk in range(SC_CONV):
            off = SUBLANES - (SC_CONV - 1) + kk
            conv = conv + scw_ref[kk:kk + 1, :] * u_ref[off:off + tile, :]
        u_ref[0:SUBLANES, :] = u_ref[tile:tile + SUBLANES, :]
        y_ref[...] = (cp_ref[:, :width] * conv * _silu(cp_ref[:, 3 * width:])).astype(y_ref.dtype)
    return run


def _layer_kernel(x_ref, ada_ref, cos_ref, sin_ref, *refs, tile, final):
    win = refs[:W_IN_BLOCKS]
    (wdt_ref, p_ref, wbr_ret_ref, wbr_ssm_ref, wbr_sc_ref, wout_ref,
     o_ref,
     rstate_ref, xbc_ref, sstate_ref, ychunk_ref, u_ref, yret_ref, yssm_ref, ysc_ref,
     qkvg_ref, z_ref, dt_ref, xact_ref, cp_ref, gates_ref, acc_ref,
     term_ref, trans_ref, ld_ref, eb_ref, cd_ref) = refs[W_IN_BLOCKS:]
    shift_ref, scale_ref, rgate_ref = (ada_ref.at[r:r + 1, :] for r in range(3))
    row = lambda r, width, n=1: p_ref.at[r:r + n, :width]
    g_ref, fg_ref = row(P_NORM_G, D_MODEL), row(P_FINAL_G, D_MODEL)
    convw_ref, convb_ref = row(P_CONV_W, SSM_XBC, SSM_CONV), row(P_CONV_B, SSM_XBC)
    dtb_ref, alog_ref = row(P_DT_BIAS, DT_PAD), row(P_A_LOG, DT_PAD)
    dskip_ref, sng_ref = row(P_DSKIP, SSM_WIDTH), row(P_SSM_NG, SSM_WIDTH)
    scw_ref = row(P_SC_W, SC_WIDTH, SC_CONV)

    @pl.when(pl.program_id(1) == 0)
    def _():
        rstate_ref[...] = jnp.zeros_like(rstate_ref)
        sstate_ref[...] = jnp.zeros_like(sstate_ref)
        xbc_ref[0:SUBLANES, :] = jnp.zeros((SUBLANES, xbc_ref.shape[1]), F32)
        u_ref[0:SUBLANES, :] = jnp.zeros((SUBLANES, u_ref.shape[1]), F32)

    d = x_ref.shape[1]
    h = _modulated_norm(x_ref[...], g_ref[...], scale_ref[...], shift_ref[...])
    blk = FILL_BLOCK

    def proj(dst_ref, first_block, n, act=None):
        return _proj_block(dst_ref, h, win[first_block + n], n * blk, act)

    for n in range(SSM_XBC // blk):
        xbc_ref[SUBLANES:SUBLANES + tile, n * blk:(n + 1) * blk] = _dot(h, win[XBC_BLOCK + n][...])
    n_chunks = tile // CHUNK
    fillers = [_proj_block(dt_ref, h, wdt_ref, 0)]
    fillers += [_ssd_terms_task(c, dt_ref, dtb_ref, alog_ref, term_ref, trans_ref) for c in range(n_chunks)]
    ret_proj = [proj(qkvg_ref, RET_BLOCK, n) for n in range(4 * RET_WIDTH // blk)]
    quad = 4
    head_tasks = [_ssd_head_task(c, list(range(q, q + quad)), term_ref, trans_ref, ld_ref, eb_ref, cd_ref)
                  for c in range(n_chunks) for q in range(0, SSM_HEADS, quad)]
    fillers += _interleave(ret_proj, head_tasks)
    _ssd_conv(xbc_ref, xact_ref, convw_ref, convb_ref, tile, fillers)

    def ret_merge():
        acc_ref[...] = gates_ref[:, :d] * _dot(yret_ref[...], wbr_ret_ref[...])

    def sc_merge():
        acc_ref[...] += gates_ref[:, 2 * d:] * _dot(ysc_ref[...], wbr_sc_ref[...])

    dense = [proj(z_ref, Z_BLOCK, n) for n in range(SSM_WIDTH // blk)]
    dense += [proj(cp_ref, SC_BLOCK, n) for n in range(cp_ref.shape[1] // blk)]
    dense += [proj(gates_ref, GATE_BLOCK, n, _sigmoid) for n in range(3 * d // blk)]
    ret_units = [_retention_unit(hd, c, qkvg_ref, cos_ref, sin_ref, rstate_ref, yret_ref)
                 for c in range(tile // CHUNK) for hd in range(RET_HEADS)]
    fillers = _interleave(dense, ret_units)
    fillers += [ret_merge, _short_conv_task(cp_ref, scw_ref, u_ref, ysc_ref, tile), sc_merge]
    _ssd_scan(xact_ref, z_ref, dskip_ref, sng_ref, term_ref, ld_ref, eb_ref, cd_ref,
              sstate_ref, ychunk_ref, yssm_ref, tile, fillers)
    _run_fillers(fillers, 1)

    merged = acc_ref[...] + gates_ref[:, d:2 * d] * _dot(yssm_ref[...], wbr_ssm_ref[...])
    out = _dot(merged.astype(BF16), wout_ref[...])
    xn = x_ref[...] + rgate_ref[...] * out
    if final:
        ms = jnp.mean(xn * xn, axis=-1, keepdims=True)
        xn = xn * lax.rsqrt(ms + EPS) * fg_ref[...]
    o_ref[...] = xn


def _layer_call(x, ada8, cos2, sin2, weights, pack, layer, tile, final):
    bsz, seq, d = x.shape
    per_layer = lambda a: pl.BlockSpec((None,) + a.shape[1:], lambda b, j: (layer,) + (0,) * (a.ndim - 1),
                                       pipeline_mode=pl.Buffered(1))
    tok = lambda w: pl.BlockSpec((None, tile, w), lambda b, j: (b, j, 0))
    w_blocks, wbr_ret, wbr_ssm, wbr_sc, w_out = weights
    width = SC_WIDTH
    w_in_block = lambda cols, n: pl.BlockSpec((None, d, cols), lambda b, j: (layer, 0, n),
                                              pipeline_mode=pl.Buffered(1))
    dt_block = W_IN_BLOCKS * FILL_BLOCK // DT_PAD
    return pl.pallas_call(
        functools.partial(_layer_kernel, tile=tile, final=final),
        grid=(bsz, seq // tile),
        in_specs=[
            tok(d),
            pl.BlockSpec((None, None, ADA_ROWS, d), lambda b, j: (layer, b, 0, 0)),
            tok(RET_DIM), tok(RET_DIM),
        ] + [w_in_block(FILL_BLOCK, n) for n in range(W_IN_BLOCKS)] + [
            w_in_block(DT_PAD, dt_block),
            per_layer(pack),
            per_layer(wbr_ret), per_layer(wbr_ssm), per_layer(wbr_sc), per_layer(w_out),
        ],
        out_specs=tok(d),
        out_shape=jax.ShapeDtypeStruct((bsz, seq, d), F32),
        scratch_shapes=[
            pltpu.VMEM((RET_HEADS, RET_DIM, RET_DIM), F32),
            pltpu.VMEM((tile + SUBLANES, SSM_XBC), F32),
            pltpu.VMEM((SSM_STATE, SSM_WIDTH), F32),
            pltpu.VMEM((CHUNK, SSM_WIDTH), F32),
            pltpu.VMEM((tile + SUBLANES, width), F32),
            pltpu.VMEM((tile, RET_WIDTH), BF16),
            pltpu.VMEM((tile, SSM_WIDTH), BF16),
            pltpu.VMEM((tile, width), BF16),
            pltpu.VMEM((tile, 4 * RET_WIDTH), F32),
            pltpu.VMEM((tile, SSM_WIDTH), F32),
            pltpu.VMEM((tile, DT_PAD), F32),
            pltpu.VMEM((tile, SSM_XBC), F32),
            pltpu.VMEM((tile, 4 * width), F32),
            pltpu.VMEM((tile, 3 * d), F32),
            pltpu.VMEM((tile, d), F32),
            pltpu.VMEM((tile // CHUNK, 3, CHUNK, DT_PAD), F32),
            pltpu.VMEM((tile // CHUNK, 2, DT_PAD, CHUNK), F32),
            pltpu.VMEM((tile // CHUNK * SSM_HEADS, CHUNK, CHUNK), F32),
            pltpu.VMEM((tile // CHUNK * SSM_HEADS, CHUNK, LANES), F32),
            pltpu.VMEM((tile // CHUNK, SSM_HEADS, LANES), F32),
        ],
        compiler_params=pltpu.CompilerParams(
            dimension_semantics=("parallel", "arbitrary"), vmem_limit_bytes=VMEM_LIMIT_BYTES),
        name="layer",
    )(x, ada8, cos2, sin2, *([w_blocks] * (W_IN_BLOCKS + 1)), pack, wbr_ret, wbr_ssm, wbr_sc, w_out)


def kernel(x, c, positions, norm_g, w_ada, b_ada, w_in, ssm_conv_w, ssm_conv_b, ssm_dt_bias, ssm_a_log,
           ssm_d, ssm_norm_g, sc_conv_w, w_br_ret, w_br_ssm, w_br_sc, w_out, final_norm_g):
    bsz, seq, d = x.shape
    depth = w_in.shape[0]
    sc_width = sc_conv_w.shape[2]
    tile = 256
    assert seq % tile == 0 and tile % CHUNK == 0

    sizes = (RET_WIDTH, RET_WIDTH, RET_WIDTH, RET_WIDTH, SSM_WIDTH, SSM_XBC, SSM_HEADS,
             sc_width, sc_width, sc_width, sc_width, 3 * d)
    offs = [0]
    for s in sizes:
        offs.append(offs[-1] + s)
    assert offs[-1] == w_in.shape[2] and d == D_MODEL and sc_width == SC_WIDTH

    ada = _ada_call(c, w_ada, b_ada)
    ada8 = jnp.pad(ada.reshape(depth, bsz, 3, d), ((0, 0), (0, 0), (0, ADA_ROWS - 3), (0, 0)))
    cos2, sin2 = _rope_call(positions)

    assert offs[6] == SC_BLOCK * FILL_BLOCK
    weights = (_w_in_prep_call(w_in), w_br_ret.astype(BF16), w_br_ssm.astype(BF16), w_br_sc.astype(BF16), w_out.astype(BF16))
    rows = lambda v: jnp.pad(v.reshape(depth, -1, v.shape[-1]), ((0, 0), (0, 0), (0, SSM_XBC - v.shape[-1])))
    pack = jnp.concatenate(
        [rows(norm_g), rows(ssm_conv_w), rows(ssm_conv_b), rows(ssm_dt_bias), rows(ssm_a_log),
         rows(jnp.repeat(ssm_d, SSM_HEAD_DIM, axis=1)), rows(ssm_norm_g), rows(sc_conv_w),
         rows(jnp.broadcast_to(final_norm_g, (depth, d)))], axis=1)
    pack = jnp.pad(pack, ((0, 0), (0, PACK_ROWS - pack.shape[1]), (0, 0)))
    for layer in range(depth):
        x = _layer_call(x, ada8, cos2, sin2, weights, pack, layer, tile, final=(layer == depth - 1))
    return x
```

```python
import functools
import math

import jax
import jax.numpy as jnp
from jax import lax
from jax.experimental import pallas as pl
from jax.experimental.pallas import tpu as pltpu

F32 = jnp.float32
BF16 = jnp.bfloat16

EPS = 1e-6
RET_HEADS = 4
RET_DIM = 128
RET_WIDTH = RET_HEADS * RET_DIM
ROPE_THETA = 10000.0
SSM_HEADS = 16
SSM_HEAD_DIM = 64
SSM_GROUPS = 2
SSM_STATE = 128
SSM_CONV = 4
SSM_WIDTH = SSM_HEADS * SSM_HEAD_DIM
SSM_XBC = SSM_WIDTH + 2 * SSM_GROUPS * SSM_STATE
SC_CONV = 3
CHUNK = 128
LANES = 128
SUBLANES = 8
DT_PAD = LANES
FILL_BLOCK = 512
D_MODEL = 1024
SC_WIDTH = D_MODEL // 2
RET_BLOCK = 0
Z_BLOCK = RET_BLOCK + 4 * RET_WIDTH // FILL_BLOCK
XBC_BLOCK = Z_BLOCK + SSM_WIDTH // FILL_BLOCK
SC_BLOCK = XBC_BLOCK + SSM_XBC // FILL_BLOCK
GATE_BLOCK = SC_BLOCK + 4 * SC_WIDTH // FILL_BLOCK
W_IN_BLOCKS = GATE_BLOCK + 3 * D_MODEL // FILL_BLOCK
P_NORM_G, P_CONV_W, P_CONV_B, P_DT_BIAS, P_A_LOG, P_DSKIP, P_SSM_NG, P_SC_W, P_FINAL_G = 0, 1, 5, 6, 7, 8, 9, 10, 13
PACK_ROWS = 16
ADA_ROWS = SUBLANES
VMEM_LIMIT_BYTES = 60 * 1024 * 1024

NEG_INF = float("-inf")


def _sigmoid(x):
    return 1.0 / (1.0 + jnp.exp(-x))


def _silu(x):
    return x * _sigmoid(x)


def _dot(a, b):
    return jnp.dot(a, b, preferred_element_type=F32)


def _modulated_norm(x, g, scale, shift):
    ms = jnp.mean(x * x, axis=-1, keepdims=True)
    y = x * lax.rsqrt(ms + EPS) * g
    return (y * (1.0 + scale) + shift).astype(BF16)


def _lane_bcast(a, col):
    return jnp.broadcast_to(a[:, col:col + 1], a.shape)


def _ada_kernel(c_ref, w_ref, b_ref, o_ref):
    act = _silu(c_ref[...])
    hi = act.astype(BF16)
    lo = (act - hi.astype(F32)).astype(BF16)
    w = w_ref[...].astype(BF16)
    o_ref[...] = _dot(hi, w) + _dot(lo, w) + b_ref[...]


def _ada_call(c, w_ada, b_ada):
    depth, d, d3 = w_ada.shape
    bsz = c.shape[0]
    nblk = d3 // d
    return pl.pallas_call(
        _ada_kernel,
        grid=(depth, nblk),
        in_specs=[
            pl.BlockSpec((bsz, d), lambda l, n: (0, 0)),
            pl.BlockSpec((None, d, d), lambda l, n: (l, 0, n)),
            pl.BlockSpec((None, 1, d), lambda l, n: (l, 0, n)),
        ],
        out_specs=pl.BlockSpec((None, bsz, d), lambda l, n: (l, 0, n)),
        out_shape=jax.ShapeDtypeStruct((depth, bsz, d3), F32),
        compiler_params=pltpu.CompilerParams(
            dimension_semantics=("parallel", "parallel"), vmem_limit_bytes=VMEM_LIMIT_BYTES),
        name="ada",
    )(c, w_ada, b_ada.reshape(depth, 1, d3))


def _w_in_prep_kernel(wa_ref, wb_ref, o_ref):
    o_ref[:, :FILL_BLOCK] = wa_ref[0].T.astype(o_ref.dtype)
    o_ref[:, FILL_BLOCK:] = wb_ref[0].T.astype(o_ref.dtype)


def _w_in_prep_call(w_in):
    depth, d, width = w_in.shape
    assert width == W_IN_BLOCKS * FILL_BLOCK + SSM_HEADS and (W_IN_BLOCKS + 1) % 2 == 0
    w_t = jnp.swapaxes(w_in, 1, 2)
    dt_start = SC_BLOCK * FILL_BLOCK
    per_block = FILL_BLOCK // SSM_HEADS
    start = lambda n: SSM_HEADS * jnp.where(n == W_IN_BLOCKS, dt_start // SSM_HEADS,
                                            n * per_block + jnp.where(n >= SC_BLOCK, 1, 0))
    rows = lambda which: pl.BlockSpec((pl.Element(1), pl.Element(FILL_BLOCK), pl.Element(d)),
                                      lambda l, n: (l, start(2 * n + which), 0))
    return pl.pallas_call(
        _w_in_prep_kernel,
        grid=(depth, (W_IN_BLOCKS + 1) // 2),
        in_specs=[rows(0), rows(1)],
        out_specs=pl.BlockSpec((None, d, 2 * FILL_BLOCK), lambda l, n: (l, 0, n)),
        out_shape=jax.ShapeDtypeStruct((depth, d, (W_IN_BLOCKS + 1) * FILL_BLOCK), BF16),
        compiler_params=pltpu.CompilerParams(
            dimension_semantics=("parallel", "parallel"), vmem_limit_bytes=VMEM_LIMIT_BYTES),
        name="w_in_prep",
    )(w_t, w_t)


def _rope_kernel(pos_ref, inv_ref, cos_ref, sin_ref):
    seq = pos_ref.shape[0]
    half_seq, half = seq // 2, RET_DIM // 2
    first = lax.broadcasted_iota(jnp.int32, (half_seq, RET_DIM), 1) < half
    pos = jnp.where(first, pos_ref[0:half_seq, :], pos_ref[half_seq:seq, :])
    ang = pos.astype(F32) * inv_ref[...]
    cs, sn = jnp.cos(ang), jnp.sin(ang)
    cs_sw, sn_sw = pltpu.roll(cs, half, axis=1), pltpu.roll(sn, half, axis=1)
    cos_ref[0:half_seq, :] = jnp.where(first, cs, cs_sw)
    cos_ref[half_seq:seq, :] = jnp.where(first, cs_sw, cs)
    sin_ref[0:half_seq, :] = jnp.where(first, -sn, sn_sw)
    sin_ref[half_seq:seq, :] = jnp.where(first, -sn_sw, sn)


def _rope_call(positions):
    bsz, seq = positions.shape
    half = RET_DIM // 2
    inv_freq = ROPE_THETA ** (-jnp.arange(0, RET_DIM, 2, dtype=F32) / RET_DIM)
    inv2 = jnp.concatenate([inv_freq, inv_freq]).reshape(1, 2 * half)
    tbl = jax.ShapeDtypeStruct((bsz, seq, RET_DIM), F32)
    return pl.pallas_call(
        _rope_kernel,
        grid=(bsz,),
        in_specs=[
            pl.BlockSpec((None, seq, 1), lambda b: (b, 0, 0)),
            pl.BlockSpec((1, RET_DIM), lambda b: (0, 0)),
        ],
        out_specs=[pl.BlockSpec((None, seq, RET_DIM), lambda b: (b, 0, 0))] * 2,
        out_shape=[tbl, tbl],
        compiler_params=pltpu.CompilerParams(
            dimension_semantics=("parallel",), vmem_limit_bytes=VMEM_LIMIT_BYTES),
        name="rope_tables",
    )(positions.reshape(bsz, seq, 1), inv2)


def _run_fillers(fillers, slots_left):
    n = -(-len(fillers) // max(slots_left, 1))
    for _ in range(min(n, len(fillers))):
        fillers.pop(0)()


def _interleave(a, b):
    paired = [t for duo in zip(a, b) for t in duo]
    return paired + a[len(b):] + b[len(a):]


def _proj_block(dst_ref, h, w_ref, dst_lo, act=None):
    def run():
        r = _dot(h, w_ref[...])
        dst_ref[:, dst_lo:dst_lo + w_ref.shape[1]] = r if act is None else act(r)
    return run


def _retention_unit(hd, c, qkvg_ref, cos_ref, sin_ref, state_ref, y_ref):
    def run():
        row = lax.broadcasted_iota(jnp.int32, (CHUNK, CHUNK), 0)
        col = lax.broadcasted_iota(jnp.int32, (CHUNK, CHUNK), 1)
        rowf = row.astype(F32)
        half = RET_DIM // 2
        log_g = math.log(1.0 - 2.0 ** (-5.0 - hd))
        decay_mask = jnp.exp(jnp.where(row >= col, (row - col).astype(F32) * log_g, NEG_INF))
        q_decay = jnp.exp((rowf + 1.0) * log_g)
        k_decay = jnp.exp((CHUNK - 1.0 - rowf) * log_g)
        chunk_decay = math.exp(CHUNK * log_g)
        rows = slice(c * CHUNK, (c + 1) * CHUNK)
        lo = hd * RET_DIM
        cos = cos_ref[rows, :]
        sin = sin_ref[rows, :]

        def rot(t):
            return t * cos + pltpu.roll(t, half, axis=1) * sin

        qc = (rot(qkvg_ref[rows, lo:lo + RET_DIM]) * (RET_DIM ** -0.5)).astype(BF16)
        kc = rot(qkvg_ref[rows, RET_WIDTH + lo:RET_WIDTH + lo + RET_DIM])
        vc = qkvg_ref[rows, 2 * RET_WIDTH + lo:2 * RET_WIDTH + lo + RET_DIM].astype(BF16)
        gate = qkvg_ref[rows, 3 * RET_WIDTH + lo:3 * RET_WIDTH + lo + RET_DIM]
        state = state_ref[hd]
        scores = lax.dot_general(qc, kc.astype(BF16), (((1,), (1,)), ((), ())), preferred_element_type=F32)
        inner = _dot((scores * decay_mask).astype(BF16), vc)
        cross = _dot(qc, state.astype(BF16)) * q_decay
        kd = (kc * k_decay).astype(BF16)
        state_ref[hd] = chunk_decay * state + lax.dot_general(kd, vc, (((0,), (0,)), ((), ())),
                                                              preferred_element_type=F32)
        o = inner + cross
        mu = jnp.mean(o, axis=-1, keepdims=True)
        oc = o - mu
        var = jnp.mean(oc * oc, axis=-1, keepdims=True)
        y = oc * lax.rsqrt(var + EPS) * _silu(gate)
        y_ref[rows, lo:lo + RET_DIM] = y.astype(y_ref.dtype)
    return run


def _block_diag_pair(slab):
    lane = lax.broadcasted_iota(jnp.int32, slab.shape, 1)
    first = lane < SSM_HEAD_DIM
    top = jnp.where(first, slab, 0.0)
    bot = jnp.where(first, 0.0, slab)
    return jnp.concatenate([top, bot], axis=0).astype(BF16)


def _ssd_conv(xbc_ref, xact_ref, convw_ref, convb_ref, tile, fillers):
    nblk = SSM_XBC // FILL_BLOCK
    for n in range(nblk):
        cols = slice(n * FILL_BLOCK, (n + 1) * FILL_BLOCK)
        conv = convb_ref[:, cols]
        for kk in range(SSM_CONV):
            off = SUBLANES - (SSM_CONV - 1) + kk
            conv = conv + convw_ref[kk:kk + 1, cols] * xbc_ref[off:off + tile, cols]
        xact_ref[:, cols] = _silu(conv)
        _run_fillers(fillers, nblk - n)
    xbc_ref[0:SUBLANES, :] = xbc_ref[tile:tile + SUBLANES, :]


T_ACS, T_WT, T_EACS = 0, 1, 2


def _cumsum_rows(x):
    row = lax.broadcasted_iota(jnp.int32, x.shape, 0)
    k = 1
    while k < x.shape[0]:
        x = x + jnp.where(row >= k, pltpu.roll(x, k, axis=0), 0.0)
        k *= 2
    return x


def _ssd_terms_task(c, dt_ref, dtb_ref, alog_ref, term_ref, trans_ref):
    def run():
        rows = slice(c * CHUNK, (c + 1) * CHUNK)
        dt_in = dt_ref[rows, :] + dtb_ref[...]
        dt_c = jnp.maximum(dt_in, 0.0) + jnp.log1p(jnp.exp(-jnp.abs(dt_in)))
        a_cs = _cumsum_rows(dt_c * (-jnp.exp(alog_ref[...])))
        a_last = a_cs[CHUNK - 1:CHUNK, :]
        term_ref[c, T_ACS] = a_cs
        term_ref[c, T_WT] = (jnp.exp(a_last - a_cs) * dt_c).T
        term_ref[c, T_EACS] = jnp.exp(a_cs)
        trans_ref[c, 0] = a_cs.T
        trans_ref[c, 1] = dt_c.T
    return run


def _ssd_head_task(c, heads, term_ref, trans_ref, ld_ref, eb_ref, cd_ref):
    def run():
        row = lax.broadcasted_iota(jnp.int32, (CHUNK, CHUNK), 0)
        col = lax.broadcasted_iota(jnp.int32, (CHUNK, CHUNK), 1)
        tril = row >= col
        lane1 = lax.broadcasted_iota(jnp.int32, (1, LANES), 1)
        a_cs = term_ref[c, T_ACS]
        e_acs = term_ref[c, T_EACS]
        cdec = e_acs[CHUNK - 1:CHUNK, :]
        for hh in heads:
            seg = _lane_bcast(a_cs, hh) - trans_ref[c, 0, hh:hh + 1, :]
            ld_ref[c * SSM_HEADS + hh] = jnp.exp(jnp.where(tril, seg, NEG_INF)) * trans_ref[c, 1, hh:hh + 1, :]
            eb_ref[c * SSM_HEADS + hh] = _lane_bcast(e_acs, hh)
        for h1 in heads[::2]:
            cd_ref[c, h1 // 2:h1 // 2 + 1, :] = jnp.where(
                lane1 < SSM_HEAD_DIM, _lane_bcast(cdec, h1), _lane_bcast(cdec, h1 + 1))
    return run


def _ssd_scan(xact_ref, z_ref, dskip_ref, ng_ref, term_ref, ld_ref, eb_ref, cd_ref,
              state_ref, ychunk_ref, y_ref, tile, fillers):
    pair = 2 * SSM_HEAD_DIM
    heads_per_group = SSM_HEADS // SSM_GROUPS
    n_chunks = tile // CHUNK
    slots = n_chunks * SSM_HEADS // 2

    for c in range(n_chunks):
        rows = slice(c * CHUNK, (c + 1) * CHUNK)
        for g in range(SSM_GROUPS):
            b_lo = SSM_WIDTH + g * SSM_STATE
            c_lo = SSM_WIDTH + SSM_GROUPS * SSM_STATE + g * SSM_STATE
            b_g = xact_ref[rows, b_lo:b_lo + SSM_STATE]
            c_g = xact_ref[rows, c_lo:c_lo + SSM_STATE]
            b_t = b_g.T
            cb = _dot(c_g.astype(BF16), b_t.astype(BF16))
            for p in range(heads_per_group // 2):
                h1 = g * heads_per_group + 2 * p
                lo = h1 * SSM_HEAD_DIM
                xs_pair = xact_ref[rows, lo:lo + pair]
                rhs = _block_diag_pair(xs_pair)
                m_parts, s_parts, o_parts = [], [], []
                for hh in (h1, h1 + 1):
                    m_parts.append(cb * ld_ref[c * SSM_HEADS + hh])
                    s_parts.append(b_t * term_ref[c, T_WT, hh:hh + 1, :])
                    o_parts.append(c_g * eb_ref[c * SSM_HEADS + hh])
                both = _dot(jnp.concatenate([jnp.concatenate(m_parts, axis=1).astype(BF16),
                                             jnp.concatenate(s_parts, axis=1).astype(BF16)], axis=0), rhs)
                y_diag, s_new = both[:CHUNK], both[CHUNK:]
                s_prev = state_ref[:, lo:lo + pair]
                y_off = _dot(jnp.concatenate(o_parts, axis=1).astype(BF16), _block_diag_pair(s_prev))
                state_ref[:, lo:lo + pair] = cd_ref[c, h1 // 2:h1 // 2 + 1, :] * s_prev + s_new
                ychunk_ref[:, lo:lo + pair] = y_diag + y_off + dskip_ref[:, lo:lo + pair] * xs_pair
                _run_fillers(fillers, slots)
                slots -= 1
        y = ychunk_ref[...] * _silu(z_ref[rows, :])
        ms = jnp.mean(y * y, axis=-1, keepdims=True)
        y_ref[rows, :] = (y * lax.rsqrt(ms + EPS) * ng_ref[...]).astype(y_ref.dtype)


def _short_conv_task(cp_ref, scw_ref, u_ref, y_ref, tile):
    def run():
        width = u_ref.shape[1]
        u_ref[SUBLANES:SUBLANES + tile, :] = cp_ref[:, width:2 * width] * cp_ref[:, 2 * width:3 * width]
        conv = jnp.zeros((tile, width), F32)
        for kk in range(SC_CONV):
            off = SUBLANES - (SC_CONV - 1) + kk
            conv = conv + scw_ref[kk:kk + 1, :] * u_ref[off:off + tile, :]
        u_ref[0:SUBLANES, :] = u_ref[tile:tile + SUBLANES, :]
        y_ref[...] = (cp_ref[:, :width] * conv * _silu(cp_ref[:, 3 * width:])).astype(y_ref.dtype)
    return run


def _layer_kernel(x_ref, ada_ref, cos_ref, sin_ref, *refs, tile, final):
    win = refs[:W_IN_BLOCKS]
    (wdt_ref, p_ref, wbr_ret_ref, wbr_ssm_ref, wbr_sc_ref, wout_ref,
     o_ref,
     rstate_ref, xbc_ref, sstate_ref, ychunk_ref, u_ref, yret_ref, yssm_ref, ysc_ref,
     qkvg_ref, z_ref, dt_ref, xact_ref, cp_ref, gates_ref, acc_ref,
     term_ref, trans_ref, ld_ref, eb_ref, cd_ref) = refs[W_IN_BLOCKS:]
    shift_ref, scale_ref, rgate_ref = (ada_ref.at[r:r + 1, :] for r in range(3))
    row = lambda r, width, n=1: p_ref.at[r:r + n, :width]
    g_ref, fg_ref = row(P_NORM_G, D_MODEL), row(P_FINAL_G, D_MODEL)
    convw_ref, convb_ref = row(P_CONV_W, SSM_XBC, SSM_CONV), row(P_CONV_B, SSM_XBC)
    dtb_ref, alog_ref = row(P_DT_BIAS, DT_PAD), row(P_A_LOG, DT_PAD)
    dskip_ref, sng_ref = row(P_DSKIP, SSM_WIDTH), row(P_SSM_NG, SSM_WIDTH)
    scw_ref = row(P_SC_W, SC_WIDTH, SC_CONV)

    @pl.when(pl.program_id(1) == 0)
    def _():
        rstate_ref[...] = jnp.zeros_like(rstate_ref)
        sstate_ref[...] = jnp.zeros_like(sstate_ref)
        xbc_ref[0:SUBLANES, :] = jnp.zeros((SUBLANES, xbc_ref.shape[1]), F32)
        u_ref[0:SUBLANES, :] = jnp.zeros((SUBLANES, u_ref.shape[1]), F32)

    d = x_ref.shape[1]
    h = _modulated_norm(x_ref[...], g_ref[...], scale_ref[...], shift_ref[...])
    blk = FILL_BLOCK

    def proj(dst_ref, first_block, n, act=None):
        return _proj_block(dst_ref, h, win[first_block + n], n * blk, act)

    for n in range(SSM_XBC // blk):
        xbc_ref[SUBLANES:SUBLANES + tile, n * blk:(n + 1) * blk] = _dot(h, win[XBC_BLOCK + n][...])
    n_chunks = tile // CHUNK
    fillers = [_proj_block(dt_ref, h, wdt_ref, 0)]
    fillers += [_ssd_terms_task(c, dt_ref, dtb_ref, alog_ref, term_ref, trans_ref) for c in range(n_chunks)]
    ret_proj = [proj(qkvg_ref, RET_BLOCK, n) for n in range(4 * RET_WIDTH // blk)]
    quad = 4
    head_tasks = [_ssd_head_task(c, list(range(q, q + quad)), term_ref, trans_ref, ld_ref, eb_ref, cd_ref)
                  for c in range(n_chunks) for q in range(0, SSM_HEADS, quad)]
    fillers += _interleave(ret_proj, head_tasks)
    _ssd_conv(xbc_ref, xact_ref, convw_ref, convb_ref, tile, fillers)

    def ret_merge():
        acc_ref[...] = gates_ref[:, :d] * _dot(yret_ref[...], wbr_ret_ref[...])

    def sc_merge():
        acc_ref[...] += gates_ref[:, 2 * d:] * _dot(ysc_ref[...], wbr_sc_ref[...])

    dense = [proj(z_ref, Z_BLOCK, n) for n in range(SSM_WIDTH // blk)]
    dense += [proj(cp_ref, SC_BLOCK, n) for n in range(cp_ref.shape[1] // blk)]
    dense += [proj(gates_ref, GATE_BLOCK, n, _sigmoid) for n in range(3 * d // blk)]
    ret_units = [_retention_unit(hd, c, qkvg_ref, cos_ref, sin_ref, rstate_ref, yret_ref)
                 for c in range(tile // CHUNK) for hd in range(RET_HEADS)]
    fillers = _interleave(dense, ret_units)
    fillers += [ret_merge, _short_conv_task(cp_ref, scw_ref, u_ref, ysc_ref, tile), sc_merge]
    _ssd_scan(xact_ref, z_ref, dskip_ref, sng_ref, term_ref, ld_ref, eb_ref, cd_ref,
              sstate_ref, ychunk_ref, yssm_ref, tile, fillers)
    _run_fillers(fillers, 1)

    merged = acc_ref[...] + gates_ref[:, d:2 * d] * _dot(yssm_ref[...], wbr_ssm_ref[...])
    out = _dot(merged.astype(BF16), wout_ref[...])
    xn = x_ref[...] + rgate_ref[...] * out
    if final:
        ms = jnp.mean(xn * xn, axis=-1, keepdims=True)
        xn = xn * lax.rsqrt(ms + EPS) * fg_ref[...]
    o_ref[...] = xn


def _layer_call(x, ada8, cos2, sin2, weights, pack, layer, tile, final):
    bsz, seq, d = x.shape
    per_layer = lambda a: pl.BlockSpec((None,) + a.shape[1:], lambda b, j: (layer,) + (0,) * (a.ndim - 1),
                                       pipeline_mode=pl.Buffered(1))
    tok = lambda w: pl.BlockSpec((None, tile, w), lambda b, j: (b, j, 0))
    w_blocks, wbr_ret, wbr_ssm, wbr_sc, w_out = weights
    width = SC_WIDTH
    w_in_block = lambda cols, n: pl.BlockSpec((None, d, cols), lambda b, j: (layer, 0, n),
                                              pipeline_mode=pl.Buffered(1))
    dt_block = W_IN_BLOCKS * FILL_BLOCK // DT_PAD
    return pl.pallas_call(
        functools.partial(_layer_kernel, tile=tile, final=final),
        grid=(bsz, seq // tile),
        in_specs=[
            tok(d),
            pl.BlockSpec((None, None, ADA_ROWS, d), lambda b, j: (layer, b, 0, 0)),
            tok(RET_DIM), tok(RET_DIM),
        ] + [w_in_block(FILL_BLOCK, n) for n in range(W_IN_BLOCKS)] + [
            w_in_block(DT_PAD, dt_block),
            per_layer(pack),
            per_layer(wbr_ret), per_layer(wbr_ssm), per_layer(wbr_sc), per_layer(w_out),
        ],
        out_specs=tok(d),
        out_shape=jax.ShapeDtypeStruct((bsz, seq, d), F32),
        scratch_shapes=[
            pltpu.VMEM((RET_HEADS, RET_DIM, RET_DIM), F32),
            pltpu.VMEM((tile + SUBLANES, SSM_XBC), F32),
            pltpu.VMEM((SSM_STATE, SSM_WIDTH), F32),
            pltpu.VMEM((CHUNK, SSM_WIDTH), F32),
            pltpu.VMEM((tile + SUBLANES, width), F32),
            pltpu.VMEM((tile, RET_WIDTH), BF16),
            pltpu.VMEM((tile, SSM_WIDTH), BF16),
            pltpu.VMEM((tile, width), BF16),
            pltpu.VMEM((tile, 4 * RET_WIDTH), F32),
            pltpu.VMEM((tile, SSM_WIDTH), F32),
            pltpu.VMEM((tile, DT_PAD), F32),
            pltpu.VMEM((tile, SSM_XBC), F32),
            pltpu.VMEM((tile, 4 * width), F32),
            pltpu.VMEM((tile, 3 * d), F32),
            pltpu.VMEM((tile, d), F32),
            pltpu.VMEM((tile // CHUNK, 3, CHUNK, DT_PAD), F32),
            pltpu.VMEM((tile // CHUNK, 2, DT_PAD, CHUNK), F32),
            pltpu.VMEM((tile // CHUNK * SSM_HEADS, CHUNK, CHUNK), F32),
            pltpu.VMEM((tile // CHUNK * SSM_HEADS, CHUNK, LANES), F32),
            pltpu.VMEM((tile // CHUNK, SSM_HEADS, LANES), F32),
        ],
        compiler_params=pltpu.CompilerParams(
            dimension_semantics=("parallel", "arbitrary"), vmem_limit_bytes=VMEM_LIMIT_BYTES),
        name="layer",
    )(x, ada8, cos2, sin2, *([w_blocks] * (W_IN_BLOCKS + 1)), pack, wbr_ret, wbr_ssm, wbr_sc, w_out)


def kernel(x, c, positions, norm_g, w_ada, b_ada, w_in, ssm_conv_w, ssm_conv_b, ssm_dt_bias, ssm_a_log,
           ssm_d, ssm_norm_g, sc_conv_w, w_br_ret, w_br_ssm, w_br_sc, w_out, final_norm_g):
    bsz, seq, d = x.shape
    depth = w_in.shape[0]
    sc_width = sc_conv_w.shape[2]
    tile = 256
    assert seq % tile == 0 and tile % CHUNK == 0

    sizes = (RET_WIDTH, RET_WIDTH, RET_WIDTH, RET_WIDTH, SSM_WIDTH, SSM_XBC, SSM_HEADS,
             sc_width, sc_width, sc_width, sc_width, 3 * d)
    offs = [0]
    for s in sizes:
        offs.append(offs[-1] + s)
    assert offs[-1] == w_in.shape[2] and d == D_MODEL and sc_width == SC_WIDTH

    ada = _ada_call(c, w_ada, b_ada)
    ada8 = jnp.pad(ada.reshape(depth, bsz, 3, d), ((0, 0), (0, 0), (0, ADA_ROWS - 3), (0, 0)))
    cos2, sin2 = _rope_call(positions)

    assert offs[6] == SC_BLOCK * FILL_BLOCK
    weights = (_w_in_prep_call(w_in), w_br_ret.astype(BF16), w_br_ssm.astype(BF16), w_br_sc.astype(BF16), w_out.astype(BF16))
    rows = lambda v: jnp.pad(v.reshape(depth, -1, v.shape[-1]), ((0, 0), (0, 0), (0, SSM_XBC - v.shape[-1])))
    pack = jnp.concatenate(
        [rows(norm_g), rows(ssm_conv_w), rows(ssm_conv_b), rows(ssm_dt_bias), rows(ssm_a_log),
         rows(jnp.repeat(ssm_d, SSM_HEAD_DIM, axis=1)), rows(ssm_norm_g), rows(sc_conv_w),
         rows(jnp.broadcast_to(final_norm_g, (depth, d)))], axis=1)
    pack = jnp.pad(pack, ((0, 0), (0, PACK_ROWS - pack.shape[1]), (0, 0)))
    for layer in range(depth):
        x = _layer_call(x, ada8, cos2, sin2, weights, pack, layer, tile, final=(layer == depth - 1))
    return x
```
